```python
import math
import jax, jax.numpy as jnp
from jax import lax
import numpy as np

D_MODEL = 2048
BATCH = 2
SEQ = 4096
DEPTH = 2

N_MIXERS = 2
PLE_DIM = 256
NORM_EPS = 1e-6

SSM_EXPAND = 2
D_INNER = SSM_EXPAND * D_MODEL
SSM_HEAD_DIM = 64
SSM_HEADS = D_INNER // SSM_HEAD_DIM
SSM_GROUPS = 8
SSM_STATE = 128
CONV_WIDTH = 4
SSD_CHUNK = 128
CONV_CH = D_INNER + 2 * SSM_GROUPS * SSM_STATE
SSM_IN_DIM = 2 * D_INNER + 2 * SSM_GROUPS * SSM_STATE + SSM_HEADS

ATTN_HEADS = 16
ATTN_HEAD_DIM = D_MODEL // ATTN_HEADS
MOBA_BLOCK = 256
MOBA_TOPK = 3
MOBA_Q_CHUNK = 16

FFN_HIDDEN = ((8 * D_MODEL + 3 * 256 - 1) // (3 * 256)) * 256

N_SSM_LAYERS = (DEPTH + 1) // 2
N_ATTN_LAYERS = DEPTH // 2

kernel_name = "hybrid_mamba2_moba_block"


def rmsnorm(x, g):
    xf = x.astype(jnp.float32)
    y = xf * lax.rsqrt(jnp.mean(xf * xf, axis=-1, keepdims=True) + NORM_EPS)
    return (y * g.astype(jnp.float32)).astype(x.dtype)


def causal_dwconv(u, w, b):
    c = u.shape[-1]
    y = lax.conv_general_dilated(u, w[:, None, :].astype(u.dtype), window_strides=(1,),
                                 padding=[(CONV_WIDTH - 1, 0)],
                                 dimension_numbers=('NWC', 'WIO', 'NWC'),
                                 feature_group_count=c)
    return y + b.astype(u.dtype)


def segsum(a):
    t = a.shape[-1]
    cs = jnp.cumsum(a, axis=-1)
    diff = cs[..., :, None] - cs[..., None, :]
    mask = jnp.tril(jnp.ones((t, t), dtype=bool))
    return jnp.where(mask, diff, -jnp.inf)


def ssd_chunked(xdt, a, bm, cm):
    b, l, h, p = xdt.shape
    g, n = bm.shape[2], bm.shape[3]
    r = h // g
    nc = l // SSD_CHUNK
    xc = xdt.reshape(b, nc, SSD_CHUNK, g, r, p)
    ac = a.astype(jnp.float32).reshape(b, nc, SSD_CHUNK, g, r).transpose(0, 3, 4, 1, 2)
    bc = bm.reshape(b, nc, SSD_CHUNK, g, n)
    cc = cm.reshape(b, nc, SSD_CHUNK, g, n)
    a_cs = jnp.cumsum(ac, axis=-1)
    decay = jnp.exp(segsum(ac))
    cb = jnp.einsum('bclgn,bcsgn->bcgls', cc, bc)
    y_diag = jnp.einsum('bcgls,bgrcls,bcsgrp->bclgrp', cb, decay, xc)
    decay_states = jnp.exp(a_cs[..., -1:] - a_cs)
    chunk_states = jnp.einsum('bclgn,bgrcl,bclgrp->bcgrpn', bc, decay_states, xc).astype(jnp.float32)
    chunk_decay = jnp.exp(a_cs[..., -1])

    def step(state, inp):
        s_c, d_c = inp
        new = state * d_c[..., None, None] + s_c
        return new, state

    h0 = jnp.zeros((b, g, r, p, n), jnp.float32)
    _, states_in = lax.scan(step, h0, (jnp.moveaxis(chunk_states, 1, 0), jnp.moveaxis(chunk_decay, 3, 0)))
    y_off = jnp.einsum('bclgn,cbgrpn,bgrcl->bclgrp', cc, states_in, jnp.exp(a_cs))
    return (y_diag + y_off).reshape(b, l, h, p)


def mamba2_mixer(u, w_in, conv_w, conv_b, dt_bias, a_log, d_skip, norm_g, w_out):
    b, l, _ = u.shape
    zxbcdt = u @ w_in
    z, xbc, dt = jnp.split(zxbcdt, [D_INNER, D_INNER + CONV_CH], axis=-1)
    xbc = jax.nn.silu(causal_dwconv(xbc, conv_w, conv_b))
    xs, bm, cm = jnp.split(xbc, [D_INNER, D_INNER + SSM_GROUPS * SSM_STATE], axis=-1)
    xs = xs.reshape(b, l, SSM_HEADS, SSM_HEAD_DIM)
    bm = bm.reshape(b, l, SSM_GROUPS, SSM_STATE)
    cm = cm.reshape(b, l, SSM_GROUPS, SSM_STATE)
    dt = jax.nn.softplus(dt.astype(jnp.float32) + dt_bias.astype(jnp.float32))
    a = -jnp.exp(a_log.astype(jnp.float32))
    y = ssd_chunked(xs * dt[..., None].astype(xs.dtype), dt * a, bm, cm)
    y = (y + d_skip[:, None] * xs).reshape(b, l, D_INNER)
    yg = (y * jax.nn.silu(z)).astype(jnp.float32).reshape(b, l, SSM_GROUPS, D_INNER // SSM_GROUPS)
    yg = yg * lax.rsqrt(jnp.mean(yg * yg, axis=-1, keepdims=True) + NORM_EPS)
    y = (yg.reshape(b, l, D_INNER) * norm_g.astype(jnp.float32)).astype(u.dtype)
    return y @ w_out


def moba_attention(u, w_qkv, w_o):
    b, l, _ = u.shape
    hh, dh, bs, qc = ATTN_HEADS, ATTN_HEAD_DIM, MOBA_BLOCK, MOBA_Q_CHUNK
    qkv = (u @ w_qkv).reshape(b, l, 3, hh, dh)
    q = qkv[:, :, 0].transpose(0, 2, 1, 3)
    k = qkv[:, :, 1].transpose(0, 2, 1, 3)
    v = qkv[:, :, 2].transpose(0, 2, 1, 3)
    nb = -(-l // bs)
    lp = nb * bs
    pad = ((0, 0), (0, 0), (0, lp - l), (0, 0))
    q, k, v = jnp.pad(q, pad), jnp.pad(k, pad), jnp.pad(v, pad)
    k_blk = k.reshape(b, hh, nb, bs, dh)
    v_blk = v.reshape(b, hh, nb, bs, dh)
    k_mean = jnp.mean(k_blk.astype(jnp.float32), axis=3)
    gate = jnp.einsum('bhtd,bhnd->bhtn', q.astype(jnp.float32), k_mean)
    q_block = jnp.arange(lp) // bs
    past = jnp.arange(nb)[None, :] < q_block[:, None]
    gate = jnp.where(past, gate, -jnp.inf)
    kk = min(MOBA_TOPK, nb)
    sel_val, sel_idx = lax.top_k(gate, kk)
    sel_ok = sel_val > -jnp.inf

    nq = lp // qc

    def to_chunks(t):
        t = t.reshape(b, hh, nq, qc, *t.shape[3:])
        return jnp.moveaxis(t, 2, 0)

    scale = dh ** -0.5
    bi = jnp.arange(b)[:, None, None, None]
    hi = jnp.arange(hh)[None, :, None, None]

    def attend(args):
        c, q_c, idx_c, ok_c = args
        t_pos = c * qc + jnp.arange(qc)
        own = (c * qc) // bs
        k_own = lax.dynamic_index_in_dim(k_blk, own, axis=2, keepdims=False)
        v_own = lax.dynamic_index_in_dim(v_blk, own, axis=2, keepdims=False)
        s_pos = own * bs + jnp.arange(bs)
        s_own = jnp.einsum('bhqd,bhsd->bhqs', q_c, k_own).astype(jnp.float32) * scale
        s_own = jnp.where(s_pos[None, :] <= t_pos[:, None], s_own, -jnp.inf)
        k_sel = k_blk[bi, hi, idx_c]
        v_sel = v_blk[bi, hi, idx_c]
        s_sel = jnp.einsum('bhqd,bhqksd->bhqks', q_c, k_sel).astype(jnp.float32) * scale
        s_sel = jnp.where(ok_c[..., None], s_sel, -jnp.inf)
        logits = jnp.concatenate([s_own, s_sel.reshape(b, hh, qc, kk * bs)], axis=-1)
        pr = jax.nn.softmax(logits, axis=-1).astype(v_blk.dtype)
        p_own = pr[..., :bs]
        p_sel = pr[..., bs:].reshape(b, hh, qc, kk, bs)
        return (jnp.einsum('bhqs,bhsd->bhqd', p_own, v_own)
                + jnp.einsum('bhqks,bhqksd->bhqd', p_sel, v_sel))

    o = lax.map(attend, (jnp.arange(nq), to_chunks(q), to_chunks(sel_idx), to_chunks(sel_ok)))
    o = jnp.moveaxis(o, 0, 2).reshape(b, hh, lp, dh)[:, :, :l]
    o = o.transpose(0, 2, 1, 3).reshape(b, l, hh * dh)
    return o @ w_o


def swiglu(u, w_gate, w_up, w_down):
    return (jax.nn.silu(u @ w_gate) * (u @ w_up)) @ w_down


def setup_inputs(seed: int = 0) -> dict:
    key = jax.random.key(seed)
    ks = jax.random.split(key, 24)
    f32 = jnp.float32

    def nrm(k, shape, fan_in):
        return jax.random.normal(k, shape, f32) * (fan_in ** -0.5)

    def gain(k, shape):
        return 1.0 + 0.05 * jax.random.normal(k, shape, f32)

    ns, na = N_SSM_LAYERS, N_ATTN_LAYERS
    dt0 = jnp.exp(jax.random.uniform(ks[8], (ns, SSM_HEADS), f32, math.log(1e-3), math.log(1e-1)))
    dt_bias = dt0 + jnp.log(-jnp.expm1(-dt0))
    a_log = jnp.log(jax.random.uniform(ks[9], (ns, SSM_HEADS), f32, 1.0, 16.0))
    return {
        "x": jax.random.normal(ks[0], (BATCH, SEQ, D_MODEL), f32),
        "p": jax.random.normal(ks[1], (DEPTH, BATCH, SEQ, PLE_DIM), f32),
        "mix_norm_g": gain(ks[2], (DEPTH, D_MODEL)),
        "ffn_norm_g": gain(ks[3], (DEPTH, D_MODEL)),
        "ple_norm_g": gain(ks[4], (DEPTH, D_MODEL)),
        "ssm_w_in": nrm(ks[5], (ns, D_MODEL, SSM_IN_DIM), D_MODEL),
        "ssm_conv_w": nrm(ks[6], (ns, CONV_WIDTH, CONV_CH), CONV_WIDTH),
        "ssm_conv_b": 0.01 * jax.random.normal(ks[7], (ns, CONV_CH), f32),
        "ssm_dt_bias": dt_bias,
        "ssm_a_log": a_log,
        "ssm_d": 1.0 + 0.1 * jax.random.normal(ks[10], (ns, SSM_HEADS), f32),
        "ssm_norm_g": gain(ks[11], (ns, D_INNER)),
        "ssm_w_out": nrm(ks[12], (ns, D_INNER, D_MODEL), D_INNER),
        "attn_w_qkv": nrm(ks[13], (na, D_MODEL, 3 * ATTN_HEADS * ATTN_HEAD_DIM), D_MODEL),
        "attn_w_o": nrm(ks[14], (na, ATTN_HEADS * ATTN_HEAD_DIM, D_MODEL), D_MODEL),
        "ffn_w_gate": nrm(ks[15], (DEPTH, D_MODEL, FFN_HIDDEN), D_MODEL),
        "ffn_w_up": nrm(ks[16], (DEPTH, D_MODEL, FFN_HIDDEN), D_MODEL),
        "ffn_w_down": nrm(ks[17], (DEPTH, FFN_HIDDEN, D_MODEL), FFN_HIDDEN),
        "ple_w_proj": nrm(ks[18], (DEPTH, PLE_DIM, D_MODEL), PLE_DIM),
        "ple_w_gate": nrm(ks[19], (DEPTH, D_MODEL, D_MODEL), D_MODEL),
        "final_norm_g": gain(ks[20], (D_MODEL,)),
    }


def reference(x, p, mix_norm_g, ffn_norm_g, ple_norm_g, ssm_w_in, ssm_conv_w, ssm_conv_b,
              ssm_dt_bias, ssm_a_log, ssm_d, ssm_norm_g, ssm_w_out, attn_w_qkv, attn_w_o,
              ffn_w_gate, ffn_w_up, ffn_w_down, ple_w_proj, ple_w_gate, final_norm_g):
    h = x
    for i in range(DEPTH):
        u = rmsnorm(h, mix_norm_g[i])
        j = i // N_MIXERS
        if i % N_MIXERS == 0:
            mix = mamba2_mixer(u, ssm_w_in[j], ssm_conv_w[j], ssm_conv_b[j], ssm_dt_bias[j],
                               ssm_a_log[j], ssm_d[j], ssm_norm_g[j], ssm_w_out[j])
        else:
            mix = moba_attention(u, attn_w_qkv[j], attn_w_o[j])
        h = h + mix
        h = h + swiglu(rmsnorm(h, ffn_norm_g[i]), ffn_w_gate[i], ffn_w_up[i], ffn_w_down[i])
        gate = jax.nn.sigmoid(rmsnorm(h, ple_norm_g[i]) @ ple_w_gate[i])
        h = h + gate * (p[i] @ ple_w_proj[i])
    return rmsnorm(h, final_norm_g)
```

```python
import functools

import jax
import jax.numpy as jnp
from jax import lax
from jax.experimental import pallas as pl
from jax.experimental.pallas import tpu as pltpu

NORM_EPS = 1e-6

SSM_HEAD_DIM = 64
SSM_GROUPS = 8
SSM_STATE = 128
CONV_WIDTH = 4
SSD_CHUNK = 128

ATTN_HEADS = 16
MOBA_BLOCK = 256
MOBA_TOPK = 3

LANES = 128
SUBLANES = 8
VMEM_BYTES = 64 * 1024 * 1024
COMPILER_SCRATCH_BYTES = 12 * 1024 * 1024

F32 = jnp.float32
BF16 = jnp.bfloat16


def _nbytes(shape, dtype):
    n = 1
    for s in shape:
        n *= s
    return n * jnp.dtype(dtype).itemsize


def _vmem_limit(block_bytes, scratch_bytes):
    need = 2 * block_bytes + scratch_bytes + COMPILER_SCRATCH_BYTES
    return int(min(need, VMEM_BYTES - 4 * 1024 * 1024))


def _silu(v):
    return v * jax.nn.sigmoid(v)


def _softplus(v):
    return jnp.maximum(v, 0.0) + jnp.log1p(jnp.exp(-jnp.abs(v)))


def _rmsnorm_body(x_ref, g_ref, o_ref):
    x = x_ref[...]
    ms = jnp.mean(x * x, axis=-1, keepdims=True)
    o_ref[...] = (x * lax.rsqrt(ms + NORM_EPS) * g_ref[...]).astype(o_ref.dtype)


def _rmsnorm(x, g, out_dtype, tm=512):
    m, d = x.shape
    blocks = _nbytes((tm, d), x.dtype) + _nbytes((tm, d), out_dtype) + _nbytes((1, d), F32)
    return pl.pallas_call(
        _rmsnorm_body,
        grid=(m // tm,),
        in_specs=[pl.BlockSpec((tm, d), lambda i: (i, 0)),
                  pl.BlockSpec((1, d), lambda i: (0, 0))],
        out_specs=pl.BlockSpec((tm, d), lambda i: (i, 0)),
        out_shape=jax.ShapeDtypeStruct((m, d), out_dtype),
        compiler_params=pltpu.CompilerParams(
            dimension_semantics=("arbitrary",), vmem_limit_bytes=_vmem_limit(blocks, 0)),
        name="rmsnorm",
    )(x, g.reshape(1, d))


def _mm_body(*refs, w_x, n_x, n_extra, epilogue):
    n_w = len(w_x)
    x_refs = refs[:n_x]
    w_refs = refs[n_x:n_x + n_w]
    e_refs = refs[n_x + n_w:n_x + n_w + n_extra]
    o_ref = refs[n_x + n_w + n_extra]
    wb_refs = refs[n_x + n_w + n_extra + 1:]

    @pl.when(pl.program_id(1) == 0)
    def _():
        for w_ref, wb_ref in zip(w_refs, wb_refs):
            wb_ref[...] = w_ref[...].astype(BF16)

    xs = [x_ref[...].astype(BF16) for x_ref in x_refs]
    accs = [jnp.dot(xs[xi], wb_ref[...], preferred_element_type=F32)
            for xi, wb_ref in zip(w_x, wb_refs)]
    o_ref[...] = epilogue(accs, [e_ref[...] for e_ref in e_refs]).astype(o_ref.dtype)


def _fused_matmul(xs, ws, w_x, extras, epilogue, out_dtype, tm, tn, n_out=None,
                  single_buffer_weights=False, name="matmul"):
    m = xs[0].shape[0]
    n = ws[0][0].shape[2] if n_out is None else n_out
    grid = (n // tn, m // tm)
    in_specs, block_bytes, scratch, scratch_bytes = [], 0, [], 0
    for x in xs:
        k = x.shape[1]
        in_specs.append(pl.BlockSpec((tm, k), lambda j, i: (i, 0)))
        block_bytes += _nbytes((tm, k), x.dtype)
    w_mode = dict(pipeline_mode=pl.Buffered(1)) if single_buffer_weights else {}
    for w, layer in ws:
        k = w.shape[1]
        in_specs.append(pl.BlockSpec((None, k, tn), functools.partial(lambda j, i, l: (l, 0, j), l=layer),
                                     **w_mode))
        block_bytes += _nbytes((k, tn), F32) // (2 if single_buffer_weights else 1)
        scratch.append(pltpu.VMEM((k, tn), BF16))
        scratch_bytes += _nbytes((k, tn), BF16)
    for e in extras:
        in_specs.append(pl.BlockSpec((tm, tn), lambda j, i: (i, j)))
        block_bytes += _nbytes((tm, tn), e.dtype)
    block_bytes += _nbytes((tm, tn), out_dtype)
    body = functools.partial(_mm_body, w_x=tuple(w_x), n_x=len(xs), n_extra=len(extras),
                             epilogue=epilogue)
    return pl.pallas_call(
        body,
        grid=grid,
        in_specs=in_specs,
        out_specs=pl.BlockSpec((tm, tn), lambda j, i: (i, j)),
        out_shape=jax.ShapeDtypeStruct((m, n), out_dtype),
        scratch_shapes=scratch,
        compiler_params=pltpu.CompilerParams(
            dimension_semantics=("arbitrary", "arbitrary"),
            vmem_limit_bytes=_vmem_limit(block_bytes, scratch_bytes)),
        name=name,
    )(*xs, *[w for w, _ in ws], *extras)


def _ep_plain(accs, extras):
    return accs[0]


def _ep_residual(accs, extras):
    return extras[0] + accs[0]


def _ep_swiglu(accs, extras):
    return _silu(accs[0]) * accs[1]


def _ep_ple(accs, extras):
    return extras[0] + jax.nn.sigmoid(accs[0]) * accs[1]


def _dt_body(u_ref, w_ref, wt_ref, dt_ref, dtt_ref):
    u = u_ref[...]
    dt_ref[...] = jnp.dot(u, w_ref[...].astype(BF16), preferred_element_type=F32)
    dtt_ref[...] = lax.dot_general(wt_ref[...].astype(BF16), u, (((1,), (1,)), ((), ())),
                                   preferred_element_type=F32)


def _dt_proj(u, w_dt, tm=1024):
    m, k = u.shape
    w_pad = jnp.pad(w_dt, ((0, 0), (0, LANES - w_dt.shape[1])))
    blocks = _nbytes((tm, k), BF16) + 2 * _nbytes((k, LANES), F32) + 2 * _nbytes((tm, LANES), F32)
    return pl.pallas_call(
        _dt_body,
        grid=(m // tm,),
        in_specs=[pl.BlockSpec((tm, k), lambda i: (i, 0)),
                  pl.BlockSpec((k, LANES), lambda i: (0, 0)),
                  pl.BlockSpec((LANES, k), lambda i: (0, 0))],
        out_specs=[pl.BlockSpec((tm, LANES), lambda i: (i, 0)),
                   pl.BlockSpec((LANES, tm), lambda i: (0, i))],
        out_shape=[jax.ShapeDtypeStruct((m, LANES), F32), jax.ShapeDtypeStruct((LANES, m), F32)],
        compiler_params=pltpu.CompilerParams(
            dimension_semantics=("arbitrary",), vmem_limit_bytes=_vmem_limit(blocks, 0)),
        name="dt_proj",
    )(u, w_pad, w_pad.T)


def _split3(v):
    hi = v.astype(BF16)
    r1 = v - hi.astype(F32)
    mid = r1.astype(BF16)
    lo = (r1 - mid.astype(F32)).astype(BF16)
    return jnp.concatenate([hi, mid, lo], axis=1)


def _conv_silu(src_ref, halo_ref, buf_ref, w_ref, b_ref, g):
    q = SSD_CHUNK
    cur = src_ref[...].astype(F32)
    buf_ref[0:SUBLANES, :] = halo_ref[g]
    buf_ref[SUBLANES:SUBLANES + q, :] = cur
    halo_ref[g] = cur[q - SUBLANES:q, :]
    acc = b_ref[...] + w_ref[CONV_WIDTH - 1:CONV_WIDTH, :] * cur
    for k in range(CONV_WIDTH - 1):
        off = SUBLANES - (CONV_WIDTH - 1) + k
        acc = acc + w_ref[k:k + 1, :] * buf_ref[off:off + q, :]
    return _silu(acc)


def _ssd_body(z_ref, x_ref, b_ref, c_ref, dt_ref, dtt_ref,
              cwx_ref, cwb_ref, cwc_ref, cbx_ref, cbb_ref, cbc_ref,
              dtb_r_ref, dtb_c_ref, alog_r_ref, alog_c_ref, d_r_ref, ng_ref, e64_ref, e128_ref,
              y_ref,
              state_ref, hx_ref, hb_ref, hc_ref, xbuf_ref, bbuf_ref, cbuf_ref, dts_ref, cs_ref, cst_ref):
    t = pl.program_id(1)
    g = pl.program_id(2)
    q = SSD_CHUNK
    hpg = x_ref.shape[1] // SSM_HEAD_DIM
    row = lax.broadcasted_iota(jnp.int32, (q, q), 0)
    col = lax.broadcasted_iota(jnp.int32, (q, q), 1)
    tril = row >= col

    @pl.when(g == 0)
    def _():
        dts = _softplus(dt_ref[...] + dtb_r_ref[...])
        a3 = _split3(dts * (-jnp.exp(alog_r_ref[...])))
        lo_tri = tril.astype(BF16)
        cs = jnp.zeros((q, LANES), F32)
        for j in range(3):
            cs = cs + jnp.dot(lo_tri, a3[:, j * LANES:(j + 1) * LANES], preferred_element_type=F32)
        dts_ref[...] = dts
        cs_ref[...] = cs
        at = _softplus(dtt_ref[...] + dtb_c_ref[...]) * (-jnp.exp(alog_c_ref[...]))
        up_tri = (row <= col).astype(BF16)
        hi = at.astype(BF16)
        r1 = at - hi.astype(F32)
        mid = r1.astype(BF16)
        lo = (r1 - mid.astype(F32)).astype(BF16)
        cst_ref[...] = (jnp.dot(hi, up_tri, preferred_element_type=F32)
                        + jnp.dot(mid, up_tri, preferred_element_type=F32)
                        + jnp.dot(lo, up_tri, preferred_element_type=F32))

    @pl.when(t == 0)
    def _():
        state_ref[g] = jnp.zeros(state_ref.shape[1:], F32)
        hx_ref[g] = jnp.zeros(hx_ref.shape[1:], F32)
        hb_ref[g] = jnp.zeros(hb_ref.shape[1:], F32)
        hc_ref[g] = jnp.zeros(hc_ref.shape[1:], F32)

    xs = _conv_silu(x_ref, hx_ref, xbuf_ref, cwx_ref, cbx_ref, g)
    bm = _conv_silu(b_ref, hb_ref, bbuf_ref, cwb_ref, cbb_ref, g).astype(BF16)
    cm = _conv_silu(c_ref, hc_ref, cbuf_ref, cwc_ref, cbc_ref, g).astype(BF16)

    e64 = e64_ref[g]
    e128 = e128_ref[g]
    cs3 = _split3(cs_ref[...])
    dt_x = jnp.dot(_split3(dts_ref[...]), e64, preferred_element_type=F32)
    cs_x = jnp.dot(cs3, e64, preferred_element_type=F32)
    cs_c = jnp.dot(cs3, e128, preferred_element_type=F32)
    d_x = jnp.dot(_split3(jnp.broadcast_to(d_r_ref[...], (SUBLANES, LANES))), e64,
                  preferred_element_type=F32)[0:1, :]
    cs_last = cs_x[q - 1:q, :]

    xdt = xs * dt_x
    xdt_b = xdt.astype(BF16)
    cb = lax.dot_general(cm, bm, (((1,), (1,)), ((), ())), preferred_element_type=F32)
    lane = lax.broadcasted_iota(jnp.int32, (q, LANES), 1)
    y_pairs = []
    for j in range(hpg // 2):
        xp = xdt_b[:, j * LANES:(j + 1) * LANES]
        outs = []
        for hh in range(2):
            h = 2 * j + hh
            colb = cs_c[:, h * LANES:(h + 1) * LANES]
            rowb = cst_ref[pl.ds(g * hpg + h, 1), :]
            decay = jnp.exp(jnp.where(tril, colb - rowb, -jnp.inf))
            outs.append(jnp.dot((cb * decay).astype(BF16), xp, preferred_element_type=F32))
        y_pairs.append(jnp.where(lane < SSM_HEAD_DIM, outs[0], outs[1]))
    y = jnp.concatenate(y_pairs, axis=1)

    st = state_ref[g]
    y = y + jnp.dot(cm, st.astype(BF16), preferred_element_type=F32) * jnp.exp(cs_x)
    xsc = (xdt * jnp.exp(cs_last - cs_x)).astype(BF16)
    s_new = lax.dot_general(bm, xsc, (((0,), (0,)), ((), ())), preferred_element_type=F32)
    state_ref[g] = st * jnp.exp(cs_last) + s_new

    y = y + d_x * xs
    yg = y * _silu(z_ref[...].astype(F32))
    ms = jnp.mean(yg * yg, axis=-1, keepdims=True)
    y_ref[...] = (yg * lax.rsqrt(ms + NORM_EPS) * ng_ref[...]).astype(y_ref.dtype)


def _expansion_matrices(heads_per_group, width):
    k = lax.broadcasted_iota(jnp.int32, (SSM_GROUPS, 3 * LANES, heads_per_group * width), 1) % LANES
    c = lax.broadcasted_iota(jnp.int32, (SSM_GROUPS, 3 * LANES, heads_per_group * width), 2) // width
    gi = lax.broadcasted_iota(jnp.int32, (SSM_GROUPS, 3 * LANES, heads_per_group * width), 0)
    return (k == gi * heads_per_group + c).astype(BF16)


def _ssd_scan(zx, dt, dtt, conv_w, conv_b, dt_bias, a_log, d_skip, norm_g, batch):
    m = zx.shape[0]
    q = SSD_CHUNK
    gn = SSM_GROUPS * SSM_STATE
    d_inner = (zx.shape[1] - 2 * gn) // 2
    gw = d_inner // SSM_GROUPS
    hpg = gw // SSM_HEAD_DIM
    nt = m // batch // q
    xb0 = d_inner // gw
    bb0 = 2 * d_inner // SSM_STATE
    cb0 = bb0 + SSM_GROUPS
    cwb0 = d_inner // SSM_STATE
    cwc0 = cwb0 + SSM_GROUPS

    def pad_row(v):
        return jnp.pad(v, (0, LANES - v.shape[0])).reshape(1, LANES)

    def pad_col(v):
        return jnp.pad(v, (0, LANES - v.shape[0])).reshape(LANES, 1)

    e64 = _expansion_matrices(hpg, SSM_HEAD_DIM)
    e128 = _expansion_matrices(hpg, LANES)
    row_map = lambda b, t, g: (b * nt + t, 0)
    const2 = lambda b, t, g: (0, 0)
    const3 = lambda b, t, g: (0, 0, 0)
    in_specs = [
        pl.BlockSpec((q, gw), lambda b, t, g: (b * nt + t, g)),
        pl.BlockSpec((q, gw), lambda b, t, g: (b * nt + t, xb0 + g)),
        pl.BlockSpec((q, SSM_STATE), lambda b, t, g: (b * nt + t, bb0 + g)),
        pl.BlockSpec((q, SSM_STATE), lambda b, t, g: (b * nt + t, cb0 + g)),
        pl.BlockSpec((q, LANES), row_map),
        pl.BlockSpec((LANES, q), lambda b, t, g: (0, b * nt + t)),
        pl.BlockSpec((CONV_WIDTH, gw), lambda b, t, g: (0, g)),
        pl.BlockSpec((CONV_WIDTH, SSM_STATE), lambda b, t, g: (0, cwb0 + g)),
        pl.BlockSpec((CONV_WIDTH, SSM_STATE), lambda b, t, g: (0, cwc0 + g)),
        pl.BlockSpec((1, gw), lambda b, t, g: (0, g)),
        pl.BlockSpec((1, SSM_STATE), lambda b, t, g: (0, cwb0 + g)),
        pl.BlockSpec((1, SSM_STATE), lambda b, t, g: (0, cwc0 + g)),
        pl.BlockSpec((1, LANES), const2),
        pl.BlockSpec((LANES, 1), const2),
        pl.BlockSpec((1, LANES), const2),
        pl.BlockSpec((LANES, 1), const2),
        pl.BlockSpec((1, LANES), const2),
        pl.BlockSpec((1, gw), lambda b, t, g: (0, g)),
        pl.BlockSpec(e64.shape, const3),
        pl.BlockSpec(e128.shape, const3),
    ]
    scratch = [
        pltpu.VMEM((SSM_GROUPS, SSM_STATE, gw), F32),
        pltpu.VMEM((SSM_GROUPS, SUBLANES, gw), F32),
        pltpu.VMEM((SSM_GROUPS, SUBLANES, SSM_STATE), F32),
        pltpu.VMEM((SSM_GROUPS, SUBLANES, SSM_STATE), F32),
        pltpu.VMEM((SUBLANES + q, gw), F32),
        pltpu.VMEM((SUBLANES + q, SSM_STATE), F32),
        pltpu.VMEM((SUBLANES + q, SSM_STATE), F32),
        pltpu.VMEM((q, LANES), F32),
        pltpu.VMEM((q, LANES), F32),
        pltpu.VMEM((LANES, q), F32),
    ]
    block_bytes = (3 * _nbytes((q, gw), BF16) + 2 * _nbytes((q, SSM_STATE), BF16)
                   + 2 * _nbytes((q, LANES), F32) + _nbytes(e64.shape, BF16) + _nbytes(e128.shape, BF16)
                   + 16 * _nbytes((SUBLANES, gw), F32))
    scratch_bytes = (_nbytes((SSM_GROUPS, SSM_STATE + 3 * SUBLANES, gw), F32)
                     + _nbytes((SUBLANES + q, gw + 2 * SSM_STATE), F32) + 3 * _nbytes((q, LANES), F32))
    return pl.pallas_call(
        _ssd_body,
        grid=(batch, nt, SSM_GROUPS),
        in_specs=in_specs,
        out_specs=pl.BlockSpec((q, gw), lambda b, t, g: (b * nt + t, g)),
        out_shape=jax.ShapeDtypeStruct((m, d_inner), BF16),
        scratch_shapes=scratch,
        compiler_params=pltpu.CompilerParams(
            dimension_semantics=("arbitrary", "arbitrary", "arbitrary"),
            vmem_limit_bytes=_vmem_limit(block_bytes, scratch_bytes)),
        name="ssd_scan",
    )(zx, zx, zx, zx, dt, dtt, conv_w, conv_w, conv_w, conv_b, conv_b, conv_b,
      pad_row(dt_bias), pad_col(dt_bias), pad_row(a_log), pad_col(a_log), pad_row(d_skip),
      norm_g.reshape(1, d_inner), e64, e128)


def _moba_body(q_ref, k_ref, v_ref, o_ref, kmean_ref, m_ref, l_ref, acc_ref):
    qi = pl.program_id(2)
    bs = MOBA_BLOCK
    nb = k_ref.shape[0] // bs
    dh = q_ref.shape[1]
    scale = dh ** -0.5

    @pl.when(qi == 0)
    def _():
        for n in range(nb):
            kmean_ref[n:n + 1, :] = jnp.mean(k_ref[n * bs:(n + 1) * bs, :].astype(F32), axis=0, keepdims=True)

    q = q_ref[...]
    gate = lax.dot_general(kmean_ref[...], q.astype(F32), (((1,), (1,)), ((), ())),
                           precision=lax.Precision.HIGHEST, preferred_element_type=F32)
    blk = lax.broadcasted_iota(jnp.int32, (nb, bs), 0)
    gate = jnp.where(blk < qi, gate, -jnp.inf)
    rank = jnp.zeros((nb, bs), F32)
    for mm in range(nb):
        gm = gate[mm:mm + 1, :]
        beats = jnp.where(gm > gate, 1.0, jnp.where(jnp.logical_and(gm == gate, blk > mm), 1.0, 0.0))
        rank = rank + beats
    sel_t = jnp.where(jnp.logical_and(rank < MOBA_TOPK, gate > -jnp.inf), 1.0, 0.0)
    sel = jnp.concatenate([sel_t, jnp.zeros((LANES - nb, bs), F32)], axis=0).T

    def scores(kb):
        return lax.dot_general(q, kb, (((1,), (1,)), ((), ())), preferred_element_type=F32) * scale

    own = pl.multiple_of(qi * bs, bs)
    row = lax.broadcasted_iota(jnp.int32, (bs, bs), 0)
    col = lax.broadcasted_iota(jnp.int32, (bs, bs), 1)
    s = jnp.where(col <= row, scores(k_ref[pl.ds(own, bs), :]), -jnp.inf)
    m0 = jnp.max(s, axis=1, keepdims=True)
    p = jnp.exp(s - m0)
    m_ref[...] = m0
    l_ref[...] = jnp.sum(p, axis=1, keepdims=True)
    acc_ref[...] = jnp.dot(p.astype(BF16), v_ref[pl.ds(own, bs), :], preferred_element_type=F32)

    for n in range(nb - 1):
        @pl.when(n < qi)
        def _(n=n):
            s = jnp.where(sel[:, n:n + 1] > 0.5, scores(k_ref[n * bs:(n + 1) * bs, :]), -jnp.inf)
            m_prev = m_ref[...]
            m_new = jnp.maximum(m_prev, jnp.max(s, axis=1, keepdims=True))
            alpha = jnp.exp(m_prev - m_new)
            p = jnp.exp(s - m_new)
            l_ref[...] = alpha * l_ref[...] + jnp.sum(p, axis=1, keepdims=True)
            acc_ref[...] = alpha * acc_ref[...] + jnp.dot(p.astype(BF16), v_ref[n * bs:(n + 1) * bs, :],
                                                          preferred_element_type=F32)
            m_ref[...] = m_new

    o_ref[...] = (acc_ref[...] / l_ref[...]).astype(o_ref.dtype)


def _moba_attention(qkv, batch):
    m = qkv.shape[0]
    seq = m // batch
    dh = qkv.shape[1] // (3 * ATTN_HEADS)
    bs = MOBA_BLOCK
    nq = seq // bs
    block_bytes = 2 * _nbytes((bs, dh), BF16) + 2 * _nbytes((seq, dh), BF16)
    scratch_bytes = _nbytes((nq, dh), F32) + 3 * _nbytes((bs, LANES), F32)
    return pl.pallas_call(
        _moba_body,
        grid=(batch, ATTN_HEADS, nq),
        in_specs=[pl.BlockSpec((bs, dh), lambda b, h, i: (b * nq + i, h)),
                  pl.BlockSpec((seq, dh), lambda b, h, i: (b, ATTN_HEADS + h)),
                  pl.BlockSpec((seq, dh), lambda b, h, i: (b, 2 * ATTN_HEADS + h))],
        out_specs=pl.BlockSpec((bs, dh), lambda b, h, i: (b * nq + i, h)),
        out_shape=jax.ShapeDtypeStruct((m, ATTN_HEADS * dh), BF16),
        scratch_shapes=[pltpu.VMEM((nq, dh), F32), pltpu.VMEM((bs, 1), F32), pltpu.VMEM((bs, 1), F32),
                        pltpu.VMEM((bs, dh), F32)],
        compiler_params=pltpu.CompilerParams(
            dimension_semantics=("arbitrary", "arbitrary", "arbitrary"),
            vmem_limit_bytes=_vmem_limit(block_bytes, scratch_bytes)),
        name="moba_attention",
    )(qkv, qkv, qkv)


def _mamba2_mixer(h, u, w_in, layer, conv_w, conv_b, dt_bias, a_log, d_skip, norm_g, w_out, batch):
    d_inner = w_out.shape[1]
    n_zx = 2 * d_inner + 2 * SSM_GROUPS * SSM_STATE
    zx = _fused_matmul([u], [(w_in, layer)], [0], [], _ep_plain, BF16, tm=1024, tn=1024, n_out=n_zx,
                       name="ssm_in_proj")
    dt, dtt = _dt_proj(u, w_in[layer][:, n_zx:])
    y = _ssd_scan(zx, dt, dtt, conv_w, conv_b.reshape(1, -1), dt_bias, a_log, d_skip, norm_g, batch)
    return _fused_matmul([y], [(w_out, layer)], [0], [h], _ep_residual, F32, tm=1024, tn=512,
                         single_buffer_weights=True, name="ssm_out_proj")


def _moba_mixer(h, u, w_qkv, w_o, layer, batch):
    qkv = _fused_matmul([u], [(w_qkv, layer)], [0], [], _ep_plain, BF16, tm=1024, tn=1024, name="attn_qkv")
    o = _moba_attention(qkv, batch)
    return _fused_matmul([o], [(w_o, layer)], [0], [h], _ep_residual, F32, tm=1024, tn=512, name="attn_out")


def _ffn_and_ple(h, layer, ffn_norm_g, ple_norm_g, w_gate, w_up, w_down, p_l, w_pgate, w_pproj):
    u = _rmsnorm(h, ffn_norm_g[layer], BF16)
    a = _fused_matmul([u], [(w_gate, layer), (w_up, layer)], [0, 0], [], _ep_swiglu, BF16,
                      tm=1024, tn=512, name="ffn_up")
    h = _fused_matmul([a], [(w_down, layer)], [0], [h], _ep_residual, F32, tm=512, tn=512,
                      single_buffer_weights=True, name="ffn_down")
    u = _rmsnorm(h, ple_norm_g[layer], BF16)
    return _fused_matmul([u, p_l], [(w_pgate, layer), (w_pproj, layer)], [0, 1], [h], _ep_ple, F32,
                         tm=1024, tn=512, name="ple")


def kernel(x, p, mix_norm_g, ffn_norm_g, ple_norm_g, ssm_w_in, ssm_conv_w, ssm_conv_b, ssm_dt_bias, ssm_a_log, ssm_d, ssm_norm_g, ssm_w_out, attn_w_qkv, attn_w_o, ffn_w_gate, ffn_w_up, ffn_w_down, ple_w_proj, ple_w_gate, final_norm_g):
    batch, seq, d = x.shape
    m = batch * seq
    depth = p.shape[0]
    h = x.reshape(m, d)
    for i in range(depth):
        u = _rmsnorm(h, mix_norm_g[i], BF16)
        j = i // 2
        if i % 2 == 0:
            h = _mamba2_mixer(h, u, ssm_w_in, j, ssm_conv_w[j], ssm_conv_b[j], ssm_dt_bias[j], ssm_a_log[j],
                              ssm_d[j], ssm_norm_g[j], ssm_w_out, batch)
        else:
            h = _moba_mixer(h, u, attn_w_qkv, attn_w_o, j, batch)
        h = _ffn_and_ple(h, i, ffn_norm_g, ple_norm_g, ffn_w_gate, ffn_w_up, ffn_w_down,
                         p[i].reshape(m, -1), ple_w_gate, ple_w_proj)
    return _rmsnorm(h, final_norm_g, F32).reshape(batch, seq, d)
```

```python
import functools

import jax
import jax.numpy as jnp
from jax import lax
from jax.experimental import pallas as pl
from jax.experimental.pallas import tpu as pltpu

NORM_EPS = 1e-6

SSM_HEAD_DIM = 64
SSM_GROUPS = 8
SSM_STATE = 128
CONV_WIDTH = 4
SSD_CHUNK = 128

ATTN_HEADS = 16
MOBA_BLOCK = 256
MOBA_TOPK = 3
ATTN_CHUNK_BLOCKS = 4
MASK_BIAS = -1e30
LOG2_E = 1.4426950408889634

LANES = 128
SUBLANES = 8
VMEM_BYTES = 64 * 1024 * 1024
COMPILER_SCRATCH_BYTES = 12 * 1024 * 1024

F32 = jnp.float32
BF16 = jnp.bfloat16


def _nbytes(shape, dtype):
    n = 1
    for s in shape:
        n *= s
    return n * jnp.dtype(dtype).itemsize


def _vmem_limit(block_bytes, scratch_bytes):
    need = 2 * block_bytes + scratch_bytes + COMPILER_SCRATCH_BYTES
    return int(min(need, VMEM_BYTES - 4 * 1024 * 1024))


def _silu(v):
    return v * jax.nn.sigmoid(v)


def _softplus(v):
    return jnp.maximum(v, 0.0) + jnp.log1p(jnp.exp(-jnp.abs(v)))


def _rmsnorm_body(x_ref, g_ref, o_ref):
    x = x_ref[...]
    ms = jnp.mean(x * x, axis=-1, keepdims=True)
    o_ref[...] = (x * lax.rsqrt(ms + NORM_EPS) * g_ref[...]).astype(o_ref.dtype)


def _rmsnorm(x, g, out_dtype, tm=512):
    m, d = x.shape
    blocks = _nbytes((tm, d), x.dtype) + _nbytes((tm, d), out_dtype) + _nbytes((1, d), F32)
    return pl.pallas_call(
        _rmsnorm_body,
        grid=(m // tm,),
        in_specs=[pl.BlockSpec((tm, d), lambda i: (i, 0)),
                  pl.BlockSpec((1, d), lambda i: (0, 0))],
        out_specs=pl.BlockSpec((tm, d), lambda i: (i, 0)),
        out_shape=jax.ShapeDtypeStruct((m, d), out_dtype),
        compiler_params=pltpu.CompilerParams(
            dimension_semantics=("arbitrary",), vmem_limit_bytes=_vmem_limit(blocks, 0)),
        name="rmsnorm",
    )(x, g.reshape(1, d))


def _mm_body(*refs, w_x, n_x, n_extra, epilogue):
    n_w = len(w_x)
    x_refs = refs[:n_x]
    w_refs = refs[n_x:n_x + n_w]
    e_refs = refs[n_x + n_w:n_x + n_w + n_extra]
    o_ref = refs[n_x + n_w + n_extra]
    wb_refs = refs[n_x + n_w + n_extra + 1:]

    @pl.when(pl.program_id(1) == 0)
    def _():
        for w_ref, wb_ref in zip(w_refs, wb_refs):
            wb_ref[...] = w_ref[...].astype(BF16)

    xs = [x_ref[...].astype(BF16) for x_ref in x_refs]
    accs = [jnp.dot(xs[xi], wb_ref[...], preferred_element_type=F32)
            for xi, wb_ref in zip(w_x, wb_refs)]
    o_ref[...] = epilogue(accs, [e_ref[...] for e_ref in e_refs]).astype(o_ref.dtype)


def _fused_matmul(xs, ws, w_x, extras, epilogue, out_dtype, tm, tn, n_out=None,
                  single_buffer_weights=False, name="matmul"):
    m = xs[0].shape[0]
    n = ws[0][0].shape[2] if n_out is None else n_out
    grid = (n // tn, m // tm)
    in_specs, block_bytes, scratch, scratch_bytes = [], 0, [], 0
    for x in xs:
        k = x.shape[1]
        in_specs.append(pl.BlockSpec((tm, k), lambda j, i: (i, 0)))
        block_bytes += _nbytes((tm, k), x.dtype)
    w_mode = dict(pipeline_mode=pl.Buffered(1)) if single_buffer_weights else {}
    for w, layer in ws:
        k = w.shape[1]
        in_specs.append(pl.BlockSpec((None, k, tn), functools.partial(lambda j, i, l: (l, 0, j), l=layer),
                                     **w_mode))
        block_bytes += _nbytes((k, tn), F32) // (2 if single_buffer_weights else 1)
        scratch.append(pltpu.VMEM((k, tn), BF16))
        scratch_bytes += _nbytes((k, tn), BF16)
    for e in extras:
        in_specs.append(pl.BlockSpec((tm, tn), lambda j, i: (i, j)))
        block_bytes += _nbytes((tm, tn), e.dtype)
    block_bytes += _nbytes((tm, tn), out_dtype)
    body = functools.partial(_mm_body, w_x=tuple(w_x), n_x=len(xs), n_extra=len(extras),
                             epilogue=epilogue)
    return pl.pallas_call(
        body,
        grid=grid,
        in_specs=in_specs,
        out_specs=pl.BlockSpec((tm, tn), lambda j, i: (i, j)),
        out_shape=jax.ShapeDtypeStruct((m, n), out_dtype),
        scratch_shapes=scratch,
        compiler_params=pltpu.CompilerParams(
            dimension_semantics=("arbitrary", "arbitrary"),
            vmem_limit_bytes=_vmem_limit(block_bytes, scratch_bytes)),
        name=name,
    )(*xs, *[w for w, _ in ws], *extras)


def _ep_plain(accs, extras):
    return accs[0]


def _ep_residual(accs, extras):
    return extras[0] + accs[0]


def _ep_swiglu(accs, extras):
    return _silu(accs[0]) * accs[1]


def _ep_ple(accs, extras):
    return extras[0] + jax.nn.sigmoid(accs[0]) * accs[1]


def _dt_body(u_ref, w_ref, wt_ref, dt_ref, dtt_ref):
    u = u_ref[...]
    dt_ref[...] = jnp.dot(u, w_ref[...].astype(BF16), preferred_element_type=F32)
    dtt_ref[...] = lax.dot_general(wt_ref[...].astype(BF16), u, (((1,), (1,)), ((), ())),
                                   preferred_element_type=F32)


def _dt_proj(u, w_dt, tm=1024):
    m, k = u.shape
    w_pad = jnp.pad(w_dt, ((0, 0), (0, LANES - w_dt.shape[1])))
    blocks = _nbytes((tm, k), BF16) + 2 * _nbytes((k, LANES), F32) + 2 * _nbytes((tm, LANES), F32)
    return pl.pallas_call(
        _dt_body,
        grid=(m // tm,),
        in_specs=[pl.BlockSpec((tm, k), lambda i: (i, 0)),
                  pl.BlockSpec((k, LANES), lambda i: (0, 0)),
                  pl.BlockSpec((LANES, k), lambda i: (0, 0))],
        out_specs=[pl.BlockSpec((tm, LANES), lambda i: (i, 0)),
                   pl.BlockSpec((LANES, tm), lambda i: (0, i))],
        out_shape=[jax.ShapeDtypeStruct((m, LANES), F32), jax.ShapeDtypeStruct((LANES, m), F32)],
        compiler_params=pltpu.CompilerParams(
            dimension_semantics=("arbitrary",), vmem_limit_bytes=_vmem_limit(blocks, 0)),
        name="dt_proj",
    )(u, w_pad, w_pad.T)


def _split3(v):
    hi = v.astype(BF16)
    r1 = v - hi.astype(F32)
    mid = r1.astype(BF16)
    lo = (r1 - mid.astype(F32)).astype(BF16)
    return jnp.concatenate([hi, mid, lo], axis=1)


def _conv_silu(src_ref, halo_ref, buf_ref, w_ref, b_ref, g):
    q = SSD_CHUNK
    cur = src_ref[...].astype(F32)
    buf_ref[0:SUBLANES, :] = halo_ref[g]
    buf_ref[SUBLANES:SUBLANES + q, :] = cur
    halo_ref[g] = cur[q - SUBLANES:q, :]
    acc = b_ref[...] + w_ref[CONV_WIDTH - 1:CONV_WIDTH, :] * cur
    for k in range(CONV_WIDTH - 1):
        off = SUBLANES - (CONV_WIDTH - 1) + k
        acc = acc + w_ref[k:k + 1, :] * buf_ref[off:off + q, :]
    return _silu(acc)


def _ssd_body(z_ref, x_ref, b_ref, c_ref, dt_ref, dtt_ref,
              cwx_ref, cwb_ref, cwc_ref, cbx_ref, cbb_ref, cbc_ref,
              dtb_r_ref, dtb_c_ref, alog_r_ref, alog_c_ref, d_r_ref, ng_ref, e64_ref, e128_ref,
              y_ref,
              state_ref, hx_ref, hb_ref, hc_ref, xbuf_ref, bbuf_ref, cbuf_ref, dts_ref, cs_ref, cst_ref):
    t = pl.program_id(1)
    g = pl.program_id(2)
    q = SSD_CHUNK
    hpg = x_ref.shape[1] // SSM_HEAD_DIM
    row = lax.broadcasted_iota(jnp.int32, (q, q), 0)
    col = lax.broadcasted_iota(jnp.int32, (q, q), 1)
    tril = row >= col

    @pl.when(g == 0)
    def _():
        dts = _softplus(dt_ref[...] + dtb_r_ref[...])
        a3 = _split3(dts * (-jnp.exp(alog_r_ref[...])))
        lo_tri = tril.astype(BF16)
        cs = jnp.zeros((q, LANES), F32)
        for j in range(3):
            cs = cs + jnp.dot(lo_tri, a3[:, j * LANES:(j + 1) * LANES], preferred_element_type=F32)
        dts_ref[...] = dts
        cs_ref[...] = cs
        at = _softplus(dtt_ref[...] + dtb_c_ref[...]) * (-jnp.exp(alog_c_ref[...]))
        up_tri = (row <= col).astype(BF16)
        hi = at.astype(BF16)
        r1 = at - hi.astype(F32)
        mid = r1.astype(BF16)
        lo = (r1 - mid.astype(F32)).astype(BF16)
        cst_ref[...] = (jnp.dot(hi, up_tri, preferred_element_type=F32)
                        + jnp.dot(mid, up_tri, preferred_element_type=F32)
                        + jnp.dot(lo, up_tri, preferred_element_type=F32))

    @pl.when(t == 0)
    def _():
        state_ref[g] = jnp.zeros(state_ref.shape[1:], F32)
        hx_ref[g] = jnp.zeros(hx_ref.shape[1:], F32)
        hb_ref[g] = jnp.zeros(hb_ref.shape[1:], F32)
        hc_ref[g] = jnp.zeros(hc_ref.shape[1:], F32)

    xs = _conv_silu(x_ref, hx_ref, xbuf_ref, cwx_ref, cbx_ref, g)
    bm = _conv_silu(b_ref, hb_ref, bbuf_ref, cwb_ref, cbb_ref, g).astype(BF16)
    cm = _conv_silu(c_ref, hc_ref, cbuf_ref, cwc_ref, cbc_ref, g).astype(BF16)

    e64 = e64_ref[g]
    e128 = e128_ref[g]
    cs3 = _split3(cs_ref[...])
    dt_x = jnp.dot(_split3(dts_ref[...]), e64, preferred_element_type=F32)
    cs_x = jnp.dot(cs3, e64, preferred_element_type=F32)
    cs_c = jnp.dot(cs3, e128, preferred_element_type=F32)
    d_x = jnp.dot(_split3(jnp.broadcast_to(d_r_ref[...], (SUBLANES, LANES))), e64,
                  preferred_element_type=F32)[0:1, :]
    cs_last = cs_x[q - 1:q, :]

    xdt = xs * dt_x
    xdt_b = xdt.astype(BF16)
    cb = lax.dot_general(cm, bm, (((1,), (1,)), ((), ())), preferred_element_type=F32)
    lane = lax.broadcasted_iota(jnp.int32, (q, LANES), 1)
    y_pairs = []
    for j in range(hpg // 2):
        xp = xdt_b[:, j * LANES:(j + 1) * LANES]
        outs = []
        for hh in range(2):
            h = 2 * j + hh
            colb = cs_c[:, h * LANES:(h + 1) * LANES]
            rowb = cst_ref[pl.ds(g * hpg + h, 1), :]
            decay = jnp.exp(jnp.where(tril, colb - rowb, -jnp.inf))
            outs.append(jnp.dot((cb * decay).astype(BF16), xp, preferred_element_type=F32))
        y_pairs.append(jnp.where(lane < SSM_HEAD_DIM, outs[0], outs[1]))
    y = jnp.concatenate(y_pairs, axis=1)

    st = state_ref[g]
    y = y + jnp.dot(cm, st.astype(BF16), preferred_element_type=F32) * jnp.exp(cs_x)
    xsc = (xdt * jnp.exp(cs_last - cs_x)).astype(BF16)
    s_new = lax.dot_general(bm, xsc, (((0,), (0,)), ((), ())), preferred_element_type=F32)
    state_ref[g] = st * jnp.exp(cs_last) + s_new

    y = y + d_x * xs
    yg = y * _silu(z_ref[...].astype(F32))
    ms = jnp.mean(yg * yg, axis=-1, keepdims=True)
    y_ref[...] = (yg * lax.rsqrt(ms + NORM_EPS) * ng_ref[...]).astype(y_ref.dtype)


def _expansion_matrices(heads_per_group, width):
    k = lax.broadcasted_iota(jnp.int32, (SSM_GROUPS, 3 * LANES, heads_per_group * width), 1) % LANES
    c = lax.broadcasted_iota(jnp.int32, (SSM_GROUPS, 3 * LANES, heads_per_group * width), 2) // width
    gi = lax.broadcasted_iota(jnp.int32, (SSM_GROUPS, 3 * LANES, heads_per_group * width), 0)
    return (k == gi * heads_per_group + c).astype(BF16)


def _ssd_scan(zx, dt, dtt, conv_w, conv_b, dt_bias, a_log, d_skip, norm_g, batch):
    m = zx.shape[0]
    q = SSD_CHUNK
    gn = SSM_GROUPS * SSM_STATE
    d_inner = (zx.shape[1] - 2 * gn) // 2
    gw = d_inner // SSM_GROUPS
    hpg = gw // SSM_HEAD_DIM
    nt = m // batch // q
    xb0 = d_inner // gw
    bb0 = 2 * d_inner // SSM_STATE
    cb0 = bb0 + SSM_GROUPS
    cwb0 = d_inner // SSM_STATE
    cwc0 = cwb0 + SSM_GROUPS

    def pad_row(v):
        return jnp.pad(v, (0, LANES - v.shape[0])).reshape(1, LANES)

    def pad_col(v):
        return jnp.pad(v, (0, LANES - v.shape[0])).reshape(LANES, 1)

    e64 = _expansion_matrices(hpg, SSM_HEAD_DIM)
    e128 = _expansion_matrices(hpg, LANES)
    row_map = lambda b, t, g: (b * nt + t, 0)
    const2 = lambda b, t, g: (0, 0)
    const3 = lambda b, t, g: (0, 0, 0)
    in_specs = [
        pl.BlockSpec((q, gw), lambda b, t, g: (b * nt + t, g)),
        pl.BlockSpec((q, gw), lambda b, t, g: (b * nt + t, xb0 + g)),
        pl.BlockSpec((q, SSM_STATE), lambda b, t, g: (b * nt + t, bb0 + g)),
        pl.BlockSpec((q, SSM_STATE), lambda b, t, g: (b * nt + t, cb0 + g)),
        pl.BlockSpec((q, LANES), row_map),
        pl.BlockSpec((LANES, q), lambda b, t, g: (0, b * nt + t)),
        pl.BlockSpec((CONV_WIDTH, gw), lambda b, t, g: (0, g)),
        pl.BlockSpec((CONV_WIDTH, SSM_STATE), lambda b, t, g: (0, cwb0 + g)),
        pl.BlockSpec((CONV_WIDTH, SSM_STATE), lambda b, t, g: (0, cwc0 + g)),
        pl.BlockSpec((1, gw), lambda b, t, g: (0, g)),
        pl.BlockSpec((1, SSM_STATE), lambda b, t, g: (0, cwb0 + g)),
        pl.BlockSpec((1, SSM_STATE), lambda b, t, g: (0, cwc0 + g)),
        pl.BlockSpec((1, LANES), const2),
        pl.BlockSpec((LANES, 1), const2),
        pl.BlockSpec((1, LANES), const2),
        pl.BlockSpec((LANES, 1), const2),
        pl.BlockSpec((1, LANES), const2),
        pl.BlockSpec((1, gw), lambda b, t, g: (0, g)),
        pl.BlockSpec(e64.shape, const3),
        pl.BlockSpec(e128.shape, const3),
    ]
    scratch = [
        pltpu.VMEM((SSM_GROUPS, SSM_STATE, gw), F32),
        pltpu.VMEM((SSM_GROUPS, SUBLANES, gw), F32),
        pltpu.VMEM((SSM_GROUPS, SUBLANES, SSM_STATE), F32),
        pltpu.VMEM((SSM_GROUPS, SUBLANES, SSM_STATE), F32),
        pltpu.VMEM((SUBLANES + q, gw), F32),
        pltpu.VMEM((SUBLANES + q, SSM_STATE), F32),
        pltpu.VMEM((SUBLANES + q, SSM_STATE), F32),
        pltpu.VMEM((q, LANES), F32),
        pltpu.VMEM((q, LANES), F32),
        pltpu.VMEM((LANES, q), F32),
    ]
    block_bytes = (3 * _nbytes((q, gw), BF16) + 2 * _nbytes((q, SSM_STATE), BF16)
                   + 2 * _nbytes((q, LANES), F32) + _nbytes(e64.shape, BF16) + _nbytes(e128.shape, BF16)
                   + 16 * _nbytes((SUBLANES, gw), F32))
    scratch_bytes = (_nbytes((SSM_GROUPS, SSM_STATE + 3 * SUBLANES, gw), F32)
                     + _nbytes((SUBLANES + q, gw + 2 * SSM_STATE), F32) + 3 * _nbytes((q, LANES), F32))
    return pl.pallas_call(
        _ssd_body,
        grid=(batch, nt, SSM_GROUPS),
        in_specs=in_specs,
        out_specs=pl.BlockSpec((q, gw), lambda b, t, g: (b * nt + t, g)),
        out_shape=jax.ShapeDtypeStruct((m, d_inner), BF16),
        scratch_shapes=scratch,
        compiler_params=pltpu.CompilerParams(
            dimension_semantics=("arbitrary", "arbitrary", "arbitrary"),
            vmem_limit_bytes=_vmem_limit(block_bytes, scratch_bytes)),
        name="ssd_scan",
    )(zx, zx, zx, zx, dt, dtt, conv_w, conv_w, conv_w, conv_b, conv_b, conv_b,
      pad_row(dt_bias), pad_col(dt_bias), pad_row(a_log), pad_col(a_log), pad_row(d_skip),
      norm_g.reshape(1, d_inner), e64, e128)


def _fold_rows(v, op):
    r, c = v.shape
    v3 = v.reshape(r // SUBLANES, SUBLANES, c)
    return jnp.max(v3, axis=0) if op == "max" else jnp.sum(v3, axis=0)


def _moba_body(q_ref, k_ref, v_ref, o_ref, kmean_ref, kaug_ref, vt_ref, s_ref, m_ref, l_ref, acc_ref):
    qi = pl.program_id(2)
    bs = MOBA_BLOCK
    seq = k_ref.shape[0]
    nb = seq // bs
    dh = q_ref.shape[1]
    ck = ATTN_CHUNK_BLOCKS * bs
    c2 = dh ** -0.5 * LOG2_E
    nt = (((1,), (1,)), ((), ()))

    @pl.when(qi == 0)
    def _():
        lane = lax.broadcasted_iota(jnp.int32, (bs, LANES), 1)
        for n in range(nb):
            kb = k_ref[n * bs:(n + 1) * bs, :]
            kmean_ref[n:n + 1, :] = jnp.mean(kb.astype(F32), axis=0, keepdims=True)
            kaug_ref[n * bs:(n + 1) * bs, 0:dh] = kb
            kaug_ref[n * bs:(n + 1) * bs, dh:dh + LANES] = jnp.where(lane == n, 1.0, 0.0).astype(BF16)
            vt_ref[n] = v_ref[n * bs:(n + 1) * bs, :].astype(F32).T.astype(BF16)

    q = q_ref[...]
    gate = lax.dot_general(kmean_ref[...], q.astype(F32), nt,
                           precision=lax.Precision.HIGHEST, preferred_element_type=F32)
    blk = lax.broadcasted_iota(jnp.int32, (nb, bs), 0)
    gate = jnp.where(blk < qi, gate, -jnp.inf)
    rank = jnp.zeros((nb, bs), F32)
    for mm in range(nb):
        gm = gate[mm:mm + 1, :]
        beats = jnp.where(gm > gate, 1.0, jnp.where(jnp.logical_and(gm == gate, blk > mm), 1.0, 0.0))
        rank = rank + beats
    chosen = jnp.logical_and(rank < MOBA_TOPK, gate > -jnp.inf)
    bias_t = jnp.where(chosen, 0.0, MASK_BIAS)
    bias_q = jnp.concatenate([bias_t, jnp.zeros((LANES - nb, bs), F32)], axis=0).T
    q_aug = jnp.concatenate([q, bias_q.astype(BF16)], axis=1)

    own = pl.multiple_of(qi * bs, bs)
    key = lax.broadcasted_iota(jnp.int32, (bs, bs), 0)
    qry = lax.broadcasted_iota(jnp.int32, (bs, bs), 1)
    s = lax.dot_general(k_ref[pl.ds(own, bs), :], q, nt, preferred_element_type=F32) * c2
    s = jnp.where(key <= qry, s, MASK_BIAS)
    s_ref[seq:seq + bs, :] = s
    m_ref[...] = _fold_rows(s, "max")
    for c in range(nb // ATTN_CHUNK_BLOCKS):
        @pl.when(c * ATTN_CHUNK_BLOCKS < qi)
        def _(c=c):
            s = lax.dot_general(kaug_ref[c * ck:(c + 1) * ck, :], q_aug, nt, preferred_element_type=F32) * c2
            s_ref[c * ck:(c + 1) * ck, :] = s
            m_ref[...] = jnp.maximum(m_ref[...], _fold_rows(s, "max"))

    m = jnp.max(m_ref[...], axis=0, keepdims=True)
    p = jnp.exp2(s_ref[seq:seq + bs, :] - m)
    l_ref[...] = _fold_rows(p, "sum")
    acc_ref[...] = jnp.dot(vt_ref[qi], p.astype(BF16), preferred_element_type=F32)
    for c in range(nb // ATTN_CHUNK_BLOCKS):
        @pl.when(c * ATTN_CHUNK_BLOCKS < qi)
        def _(c=c):
            p = jnp.exp2(s_ref[c * ck:(c + 1) * ck, :] - m)
            l_ref[...] = l_ref[...] + _fold_rows(p, "sum")
            pb = p.astype(BF16)
            acc = acc_ref[...]
            for j in range(ATTN_CHUNK_BLOCKS):
                acc = acc + jnp.dot(vt_ref[c * ATTN_CHUNK_BLOCKS + j], pb[j * bs:(j + 1) * bs, :],
                                    preferred_element_type=F32)
            acc_ref[...] = acc

    l = jnp.sum(l_ref[...], axis=0, keepdims=True)
    o_ref[...] = (acc_ref[...] / l).T.astype(o_ref.dtype)


def _moba_attention(qkv, batch):
    m = qkv.shape[0]
    seq = m // batch
    dh = qkv.shape[1] // (3 * ATTN_HEADS)
    bs = MOBA_BLOCK
    nq = seq // bs
    assert dh == LANES and nq % ATTN_CHUNK_BLOCKS == 0 and nq <= LANES
    block_bytes = 2 * _nbytes((bs, dh), BF16) + 2 * _nbytes((seq, dh), BF16)
    scratch = [
        pltpu.VMEM((nq, dh), F32),
        pltpu.VMEM((seq, dh + LANES), BF16),
        pltpu.VMEM((nq, dh, bs), BF16),
        pltpu.VMEM((seq + bs, bs), F32),
        pltpu.VMEM((SUBLANES, bs), F32),
        pltpu.VMEM((SUBLANES, bs), F32),
        pltpu.VMEM((dh, bs), F32),
    ]
    scratch_bytes = (_nbytes((nq, dh), F32) + _nbytes((seq, dh + LANES), BF16) + _nbytes((nq, dh, bs), BF16)
                     + _nbytes((seq + bs, bs), F32) + 2 * _nbytes((SUBLANES, bs), F32) + _nbytes((dh, bs), F32))
    return pl.pallas_call(
        _moba_body,
        grid=(batch, ATTN_HEADS, nq),
        in_specs=[pl.BlockSpec((bs, dh), lambda b, h, i: (b * nq + i, h)),
                  pl.BlockSpec((seq, dh), lambda b, h, i: (b, ATTN_HEADS + h)),
                  pl.BlockSpec((seq, dh), lambda b, h, i: (b, 2 * ATTN_HEADS + h))],
        out_specs=pl.BlockSpec((bs, dh), lambda b, h, i: (b * nq + i, h)),
        out_shape=jax.ShapeDtypeStruct((m, ATTN_HEADS * dh), BF16),
        scratch_shapes=scratch,
        compiler_params=pltpu.CompilerParams(
            dimension_semantics=("arbitrary", "arbitrary", "arbitrary"),
            vmem_limit_bytes=_vmem_limit(block_bytes, scratch_bytes)),
        name="moba_attention",
    )(qkv, qkv, qkv)


def _mamba2_mixer(h, u, w_in, layer, conv_w, conv_b, dt_bias, a_log, d_skip, norm_g, w_out, batch):
    d_inner = w_out.shape[1]
    n_zx = 2 * d_inner + 2 * SSM_GROUPS * SSM_STATE
    zx = _fused_matmul([u], [(w_in, layer)], [0], [], _ep_plain, BF16, tm=1024, tn=1024, n_out=n_zx,
                       name="ssm_in_proj")
    dt, dtt = _dt_proj(u, w_in[layer][:, n_zx:])
    y = _ssd_scan(zx, dt, dtt, conv_w, conv_b.reshape(1, -1), dt_bias, a_log, d_skip, norm_g, batch)
    return _fused_matmul([y], [(w_out, layer)], [0], [h], _ep_residual, F32, tm=1024, tn=512,
                         single_buffer_weights=True, name="ssm_out_proj")


def _moba_mixer(h, u, w_qkv, w_o, layer, batch):
    qkv = _fused_matmul([u], [(w_qkv, layer)], [0], [], _ep_plain, BF16, tm=1024, tn=1024, name="attn_qkv")
    o = _moba_attention(qkv, batch)
    return _fused_matmul([o], [(w_o, layer)], [0], [h], _ep_residual, F32, tm=1024, tn=512, name="attn_out")


def _ffn_and_ple(h, layer, ffn_norm_g, ple_norm_g, w_gate, w_up, w_down, p_l, w_pgate, w_pproj):
    u = _rmsnorm(h, ffn_norm_g[layer], BF16)
    a = _fused_matmul([u], [(w_gate, layer), (w_up, layer)], [0, 0], [], _ep_swiglu, BF16,
                      tm=1024, tn=512, name="ffn_up")
    h = _fused_matmul([a], [(w_down, layer)], [0], [h], _ep_residual, F32, tm=512, tn=512,
                      single_buffer_weights=True, name="ffn_down")
    u = _rmsnorm(h, ple_norm_g[layer], BF16)
    return _fused_matmul([u, p_l], [(w_pgate, layer), (w_pproj, layer)], [0, 1], [h], _ep_ple, F32,
                         tm=1024, tn=512, name="ple")


def kernel(x, p, mix_norm_g, ffn_norm_g, ple_norm_g, ssm_w_in, ssm_conv_w, ssm_conv_b, ssm_dt_bias, ssm_a_log, ssm_d, ssm_norm_g, ssm_w_out, attn_w_qkv, attn_w_o, ffn_w_gate, ffn_w_up, ffn_w_down, ple_w_proj, ple_w_gate, final_norm_g):
    batch, seq, d = x.shape
    m = batch * seq
    depth = p.shape[0]
    h = x.reshape(m, d)
    for i in range(depth):
        u = _rmsnorm(h, mix_norm_g[i], BF16)
        j = i // 2
        if i % 2 == 0:
            h = _mamba2_mixer(h, u, ssm_w_in, j, ssm_conv_w[j], ssm_conv_b[j], ssm_dt_bias[j], ssm_a_log[j],
                              ssm_d[j], ssm_norm_g[j], ssm_w_out, batch)
        else:
            h = _moba_mixer(h, u, attn_w_qkv, attn_w_o, j, batch)
        h = _ffn_and_ple(h, i, ffn_norm_g, ple_norm_g, ffn_w_gate, ffn_w_up, ffn_w_down,
                         p[i].reshape(m, -1), ple_w_gate, ple_w_proj)
    return _rmsnorm(h, final_norm_g, F32).reshape(batch, seq, d)
```

```python
import functools

import jax
import jax.numpy as jnp
from jax import lax
from jax.experimental import pallas as pl
from jax.experimental.pallas import tpu as pltpu

NORM_EPS = 1e-6

SSM_HEAD_DIM = 64
SSM_GROUPS = 8
SSM_STATE = 128
CONV_WIDTH = 4
SSD_CHUNK = 128

ATTN_HEADS = 16
MOBA_BLOCK = 256
MOBA_TOPK = 3
ATTN_CHUNK_BLOCKS = 4
ATTN_HEADS_PER_STEP = 4
MASK_BIAS = -1e30
LOG2_E = 1.4426950408889634

LANES = 128
SUBLANES = 8
VMEM_BYTES = 64 * 1024 * 1024
COMPILER_SCRATCH_BYTES = 12 * 1024 * 1024

F32 = jnp.float32
BF16 = jnp.bfloat16


def _nbytes(shape, dtype):
    n = 1
    for s in shape:
        n *= s
    return n * jnp.dtype(dtype).itemsize


def _vmem_limit(block_bytes, scratch_bytes):
    need = 2 * block_bytes + scratch_bytes + COMPILER_SCRATCH_BYTES
    return int(min(need, VMEM_BYTES - 4 * 1024 * 1024))


def _silu(v):
    return v * jax.nn.sigmoid(v)


def _softplus(v):
    return jnp.maximum(v, 0.0) + jnp.log1p(jnp.exp(-jnp.abs(v)))


def _rmsnorm_rows(x, g):
    ms = jnp.mean(x * x, axis=-1, keepdims=True)
    return x * lax.rsqrt(ms + NORM_EPS) * g


def _rmsnorm_body(x_ref, g_ref, o_ref):
    o_ref[...] = _rmsnorm_rows(x_ref[...], g_ref[...]).astype(o_ref.dtype)


def _rmsnorm(x, g, out_dtype, tm=512):
    m, d = x.shape
    blocks = _nbytes((tm, d), x.dtype) + _nbytes((tm, d), out_dtype) + _nbytes((1, d), F32)
    return pl.pallas_call(
        _rmsnorm_body,
        grid=(m // tm,),
        in_specs=[pl.BlockSpec((tm, d), lambda i: (i, 0)),
                  pl.BlockSpec((1, d), lambda i: (0, 0))],
        out_specs=pl.BlockSpec((tm, d), lambda i: (i, 0)),
        out_shape=jax.ShapeDtypeStruct((m, d), out_dtype),
        compiler_params=pltpu.CompilerParams(
            dimension_semantics=("arbitrary",), vmem_limit_bytes=_vmem_limit(blocks, 0)),
        name="rmsnorm",
    )(x, g.reshape(1, d))


def _mm_body(*refs, normed, w_x, n_extra, epilogue):
    n_x, n_g, n_w = len(normed), sum(normed), len(w_x)
    x_refs = refs[:n_x]
    g_refs = list(refs[n_x:n_x + n_g])
    w_refs = refs[n_x + n_g:n_x + n_g + n_w]
    e_refs = refs[n_x + n_g + n_w:n_x + n_g + n_w + n_extra]
    o_ref = refs[n_x + n_g + n_w + n_extra]
    wb_refs = refs[n_x + n_g + n_w + n_extra + 1:]

    @pl.when(pl.program_id(1) == 0)
    def _():
        for w_ref, wb_ref in zip(w_refs, wb_refs):
            wb_ref[...] = w_ref[...].astype(BF16)

    xs = []
    for x_ref, is_normed in zip(x_refs, normed):
        x = x_ref[...]
        if is_normed:
            x = _rmsnorm_rows(x, g_refs.pop(0)[...])
        xs.append(x.astype(BF16))
    accs = [jnp.dot(xs[xi], wb_ref[...], preferred_element_type=F32)
            for xi, wb_ref in zip(w_x, wb_refs)]
    o_ref[...] = epilogue(accs, [e_ref[...] for e_ref in e_refs]).astype(o_ref.dtype)


def _fused_matmul(xs, gains, ws, w_x, extras, epilogue, out_dtype, tm, tn, n_out=None,
                  single_buffer_weights=False, name="matmul"):
    m = xs[0].shape[0]
    n = ws[0][0].shape[2] if n_out is None else n_out
    grid = (n // tn, m // tm)
    in_specs, block_bytes, scratch, scratch_bytes = [], 0, [], 0
    for x in xs:
        k = x.shape[1]
        in_specs.append(pl.BlockSpec((tm, k), lambda j, i: (i, 0)))
        block_bytes += _nbytes((tm, k), x.dtype)
    gain_rows = [g.reshape(1, -1) for g in gains if g is not None]
    for g in gain_rows:
        in_specs.append(pl.BlockSpec(g.shape, lambda j, i: (0, 0)))
    w_mode = dict(pipeline_mode=pl.Buffered(1)) if single_buffer_weights else {}
    for w, layer in ws:
        k = w.shape[1]
        in_specs.append(pl.BlockSpec((None, k, tn), functools.partial(lambda j, i, l: (l, 0, j), l=layer),
                                     **w_mode))
        block_bytes += _nbytes((k, tn), F32) // (2 if single_buffer_weights else 1)
        scratch.append(pltpu.VMEM((k, tn), BF16))
        scratch_bytes += _nbytes((k, tn), BF16)
    for e in extras:
        in_specs.append(pl.BlockSpec((tm, tn), lambda j, i: (i, j)))
        block_bytes += _nbytes((tm, tn), e.dtype)
    block_bytes += _nbytes((tm, tn), out_dtype)
    body = functools.partial(_mm_body, normed=tuple(g is not None for g in gains), w_x=tuple(w_x),
                             n_extra=len(extras), epilogue=epilogue)
    return pl.pallas_call(
        body,
        grid=grid,
        in_specs=in_specs,
        out_specs=pl.BlockSpec((tm, tn), lambda j, i: (i, j)),
        out_shape=jax.ShapeDtypeStruct((m, n), out_dtype),
        scratch_shapes=scratch,
        compiler_params=pltpu.CompilerParams(
            dimension_semantics=("arbitrary", "arbitrary"),
            vmem_limit_bytes=_vmem_limit(block_bytes, scratch_bytes)),
        name=name,
    )(*xs, *gain_rows, *[w for w, _ in ws], *extras)


def _ep_plain(accs, extras):
    return accs[0]


def _ep_residual(accs, extras):
    return extras[0] + accs[0]


def _ep_swiglu(accs, extras):
    return _silu(accs[0]) * accs[1]


def _ep_ple(accs, extras):
    return extras[0] + jax.nn.sigmoid(accs[0]) * accs[1]


def _dt_body(h_ref, g_ref, w_ref, dt_ref, dtt_ref):
    u = _rmsnorm_rows(h_ref[...], g_ref[...]).astype(BF16)
    dt = jnp.dot(u, w_ref[...].astype(BF16), preferred_element_type=F32)
    dt_ref[...] = dt
    dtt_ref[...] = dt.T


def _dt_proj(h, gain, w_dt, tm=1024):
    m, k = h.shape
    w_pad = jnp.pad(w_dt, ((0, 0), (0, LANES - w_dt.shape[1])))
    blocks = _nbytes((tm, k), F32) + _nbytes((k, LANES), F32) + 2 * _nbytes((tm, LANES), F32)
    return pl.pallas_call(
        _dt_body,
        grid=(m // tm,),
        in_specs=[pl.BlockSpec((tm, k), lambda i: (i, 0)),
                  pl.BlockSpec((1, k), lambda i: (0, 0)),
                  pl.BlockSpec((k, LANES), lambda i: (0, 0))],
        out_specs=[pl.BlockSpec((tm, LANES), lambda i: (i, 0)),
                   pl.BlockSpec((LANES, tm), lambda i: (0, i))],
        out_shape=[jax.ShapeDtypeStruct((m, LANES), F32), jax.ShapeDtypeStruct((LANES, m), F32)],
        compiler_params=pltpu.CompilerParams(
            dimension_semantics=("arbitrary",), vmem_limit_bytes=_vmem_limit(blocks, 0)),
        name="dt_proj",
    )(h, gain.reshape(1, k), w_pad)


def _split3(v):
    hi = v.astype(BF16)
    r1 = v - hi.astype(F32)
    mid = r1.astype(BF16)
    lo = (r1 - mid.astype(F32)).astype(BF16)
    return hi, mid, lo


def _conv_silu(src_ref, halo_ref, buf_ref, w_ref, b_ref, g):
    q = SSD_CHUNK
    cur = src_ref[...].astype(F32)
    buf_ref[0:SUBLANES, :] = halo_ref[g]
    buf_ref[SUBLANES:SUBLANES + q, :] = cur
    halo_ref[g] = cur[q - SUBLANES:q, :]
    acc = b_ref[...] + w_ref[CONV_WIDTH - 1:CONV_WIDTH, :] * cur
    for k in range(CONV_WIDTH - 1):
        off = SUBLANES - (CONV_WIDTH - 1) + k
        acc = acc + w_ref[k:k + 1, :] * buf_ref[off:off + q, :]
    return _silu(acc)


def _ssd_body(z_ref, x_ref, b_ref, c_ref, dt_ref, dtt_ref,
              cwx_ref, cwb_ref, cwc_ref, cbx_ref, cbb_ref, cbc_ref,
              dtb_r_ref, dtb_c_ref, alog_r_ref, alog_c_ref, d_r_ref, ng_ref, e64_ref, e128_ref,
              y_ref,
              state_ref, hx_ref, hb_ref, hc_ref, xbuf_ref, bbuf_ref, cbuf_ref, dts3_ref, cs3_ref, cst_ref):
    t = pl.program_id(1)
    g = pl.program_id(2)
    q = SSD_CHUNK
    hpg = x_ref.shape[1] // SSM_HEAD_DIM
    row = lax.broadcasted_iota(jnp.int32, (q, q), 0)
    col = lax.broadcasted_iota(jnp.int32, (q, q), 1)
    tril = row >= col

    @pl.when(g == 0)
    def _():
        dts = _softplus(dt_ref[...] + dtb_r_ref[...])
        lo_tri = tril.astype(BF16)
        cs = jnp.zeros((q, LANES), F32)
        for part in _split3(dts * (-jnp.exp(alog_r_ref[...]))):
            cs = cs + jnp.dot(lo_tri, part, preferred_element_type=F32)
        dts3_ref[...] = jnp.concatenate(_split3(dts), axis=1)
        cs3_ref[...] = jnp.concatenate(_split3(cs), axis=1)
        up_tri = (row <= col).astype(BF16)
        cst = jnp.zeros((LANES, q), F32)
        for part in _split3(_softplus(dtt_ref[...] + dtb_c_ref[...]) * (-jnp.exp(alog_c_ref[...]))):
            cst = cst + jnp.dot(part, up_tri, preferred_element_type=F32)
        cst_ref[...] = cst

    @pl.when(t == 0)
    def _():
        state_ref[g] = jnp.zeros(state_ref.shape[1:], F32)
        hx_ref[g] = jnp.zeros(hx_ref.shape[1:], F32)
        hb_ref[g] = jnp.zeros(hb_ref.shape[1:], F32)
        hc_ref[g] = jnp.zeros(hc_ref.shape[1:], F32)

    xs = _conv_silu(x_ref, hx_ref, xbuf_ref, cwx_ref, cbx_ref, g)
    bm = _conv_silu(b_ref, hb_ref, bbuf_ref, cwb_ref, cbb_ref, g).astype(BF16)
    cm = _conv_silu(c_ref, hc_ref, cbuf_ref, cwc_ref, cbc_ref, g).astype(BF16)

    e64 = e64_ref[g]
    e128 = e128_ref[g]
    cs3 = cs3_ref[...]
    dt_x = jnp.dot(dts3_ref[...], e64, preferred_element_type=F32)
    cs_x = jnp.dot(cs3, e64, preferred_element_type=F32)
    cs_c = jnp.dot(cs3, e128, preferred_element_type=F32)
    d3 = jnp.concatenate(_split3(jnp.broadcast_to(d_r_ref[...], (SUBLANES, LANES))), axis=1)
    d_x = jnp.dot(d3, e64, preferred_element_type=F32)[0:1, :]
    cs_last = cs_x[q - 1:q, :]

    xdt = xs * dt_x
    xdt_b = xdt.astype(BF16)
    cb = lax.dot_general(cm, bm, (((1,), (1,)), ((), ())), preferred_element_type=F32)
    lane = lax.broadcasted_iota(jnp.int32, (q, LANES), 1)
    y_pairs = []
    for j in range(hpg // 2):
        xp = xdt_b[:, j * LANES:(j + 1) * LANES]
        outs = []
        for hh in range(2):
            h = 2 * j + hh
            colb = cs_c[:, h * LANES:(h + 1) * LANES]
            rowb = cst_ref[pl.ds(g * hpg + h, 1), :]
            decay = jnp.exp(jnp.where(tril, colb - rowb, -jnp.inf))
            outs.append(jnp.dot((cb * decay).astype(BF16), xp, preferred_element_type=F32))
        y_pairs.append(jnp.where(lane < SSM_HEAD_DIM, outs[0], outs[1]))
    y = jnp.concatenate(y_pairs, axis=1)

    st = state_ref[g]
    y = y + jnp.dot(cm, st.astype(BF16), preferred_element_type=F32) * jnp.exp(cs_x)
    xsc = (xdt * jnp.exp(cs_last - cs_x)).astype(BF16)
    s_new = lax.dot_general(bm, xsc, (((0,), (0,)), ((), ())), preferred_element_type=F32)
    state_ref[g] = st * jnp.exp(cs_last) + s_new

    y = y + d_x * xs
    yg = y * _silu(z_ref[...].astype(F32))
    y_ref[...] = _rmsnorm_rows(yg, ng_ref[...]).astype(y_ref.dtype)


def _expansion_matrices(heads_per_group, width):
    k = lax.broadcasted_iota(jnp.int32, (SSM_GROUPS, 3 * LANES, heads_per_group * width), 1) % LANES
    c = lax.broadcasted_iota(jnp.int32, (SSM_GROUPS, 3 * LANES, heads_per_group * width), 2) // width
    gi = lax.broadcasted_iota(jnp.int32, (SSM_GROUPS, 3 * LANES, heads_per_group * width), 0)
    return (k == gi * heads_per_group + c).astype(BF16)


def _ssd_scan(zx, dt, dtt, conv_w, conv_b, dt_bias, a_log, d_skip, norm_g, batch):
    m = zx.shape[0]
    q = SSD_CHUNK
    gn = SSM_GROUPS * SSM_STATE
    d_inner = (zx.shape[1] - 2 * gn) // 2
    gw = d_inner // SSM_GROUPS
    hpg = gw // SSM_HEAD_DIM
    nt = m // batch // q
    xb0 = d_inner // gw
    bb0 = 2 * d_inner // SSM_STATE
    cb0 = bb0 + SSM_GROUPS
    cwb0 = d_inner // SSM_STATE
    cwc0 = cwb0 + SSM_GROUPS

    def pad_row(v):
        return jnp.pad(v, (0, LANES - v.shape[0])).reshape(1, LANES)

    def pad_col(v):
        return jnp.pad(v, (0, LANES - v.shape[0])).reshape(LANES, 1)

    e64 = _expansion_matrices(hpg, SSM_HEAD_DIM)
    e128 = _expansion_matrices(hpg, LANES)
    row_map = lambda b, t, g: (b * nt + t, 0)
    const2 = lambda b, t, g: (0, 0)
    const3 = lambda b, t, g: (0, 0, 0)
    in_specs = [
        pl.BlockSpec((q, gw), lambda b, t, g: (b * nt + t, g)),
        pl.BlockSpec((q, gw), lambda b, t, g: (b * nt + t, xb0 + g)),
        pl.BlockSpec((q, SSM_STATE), lambda b, t, g: (b * nt + t, bb0 + g)),
        pl.BlockSpec((q, SSM_STATE), lambda b, t, g: (b * nt + t, cb0 + g)),
        pl.BlockSpec((q, LANES), row_map),
        pl.BlockSpec((LANES, q), lambda b, t, g: (0, b * nt + t)),
        pl.BlockSpec((CONV_WIDTH, gw), lambda b, t, g: (0, g)),
        pl.BlockSpec((CONV_WIDTH, SSM_STATE), lambda b, t, g: (0, cwb0 + g)),
        pl.BlockSpec((CONV_WIDTH, SSM_STATE), lambda b, t, g: (0, cwc0 + g)),
        pl.BlockSpec((1, gw), lambda b, t, g: (0, g)),
        pl.BlockSpec((1, SSM_STATE), lambda b, t, g: (0, cwb0 + g)),
        pl.BlockSpec((1, SSM_STATE), lambda b, t, g: (0, cwc0 + g)),
        pl.BlockSpec((1, LANES), const2),
        pl.BlockSpec((LANES, 1), const2),
        pl.BlockSpec((1, LANES), const2),
        pl.BlockSpec((LANES, 1), const2),
        pl.BlockSpec((1, LANES), const2),
        pl.BlockSpec((1, gw), lambda b, t, g: (0, g)),
        pl.BlockSpec(e64.shape, const3),
        pl.BlockSpec(e128.shape, const3),
    ]
    scratch = [
        pltpu.VMEM((SSM_GROUPS, SSM_STATE, gw), F32),
        pltpu.VMEM((SSM_GROUPS, SUBLANES, gw), F32),
        pltpu.VMEM((SSM_GROUPS, SUBLANES, SSM_STATE), F32),
        pltpu.VMEM((SSM_GROUPS, SUBLANES, SSM_STATE), F32),
        pltpu.VMEM((SUBLANES + q, gw), F32),
        pltpu.VMEM((SUBLANES + q, SSM_STATE), F32),
        pltpu.VMEM((SUBLANES + q, SSM_STATE), F32),
        pltpu.VMEM((q, 3 * LANES), BF16),
        pltpu.VMEM((q, 3 * LANES), BF16),
        pltpu.VMEM((LANES, q), F32),
    ]
    block_bytes = (3 * _nbytes((q, gw), BF16) + 2 * _nbytes((q, SSM_STATE), BF16)
                   + 2 * _nbytes((q, LANES), F32) + _nbytes(e64.shape, BF16) + _nbytes(e128.shape, BF16)
                   + 16 * _nbytes((SUBLANES, gw), F32))
    scratch_bytes = (_nbytes((SSM_GROUPS, SSM_STATE + 3 * SUBLANES, gw), F32)
                     + _nbytes((SUBLANES + q, gw + 2 * SSM_STATE), F32) + 3 * _nbytes((q, LANES), F32))
    return pl.pallas_call(
        _ssd_body,
        grid=(batch, nt, SSM_GROUPS),
        in_specs=in_specs,
        out_specs=pl.BlockSpec((q, gw), lambda b, t, g: (b * nt + t, g)),
        out_shape=jax.ShapeDtypeStruct((m, d_inner), BF16),
        scratch_shapes=scratch,
        compiler_params=pltpu.CompilerParams(
            dimension_semantics=("arbitrary", "arbitrary", "arbitrary"),
            vmem_limit_bytes=_vmem_limit(block_bytes, scratch_bytes)),
        name="ssd_scan",
    )(zx, zx, zx, zx, dt, dtt, conv_w, conv_w, conv_w, conv_b, conv_b, conv_b,
      pad_row(dt_bias), pad_col(dt_bias), pad_row(a_log), pad_col(a_log), pad_row(d_skip),
      norm_g.reshape(1, d_inner), e64, e128)


def _fold_rows(v, op):
    r, c = v.shape
    v3 = v.reshape(r // SUBLANES, SUBLANES, c)
    return jnp.max(v3, axis=0) if op == "max" else jnp.sum(v3, axis=0)


def _moba_body(q_ref, k_ref, v_ref, o_ref,
               kmean_ref, kaug_ref, vt_ref, qaug_ref, s_ref, m_ref, l_ref, acc_ref):
    qi = pl.program_id(2)
    bs = MOBA_BLOCK
    dh = LANES
    seq = k_ref.shape[0]
    nb = seq // bs
    heads = range(q_ref.shape[1] // dh)
    cbk = ATTN_CHUNK_BLOCKS
    ck = cbk * bs
    c2 = dh ** -0.5 * LOG2_E
    nt = (((1,), (1,)), ((), ()))

    @pl.when(qi == 0)
    def _():
        lane = lax.broadcasted_iota(jnp.int32, (bs, LANES), 1)
        for hh in heads:
            for n in range(nb):
                kb = k_ref[n * bs:(n + 1) * bs, hh * dh:(hh + 1) * dh]
                kmean_ref[hh, n:n + 1, :] = jnp.mean(kb.astype(F32), axis=0, keepdims=True)
                kaug_ref[hh, n * bs:(n + 1) * bs, 0:dh] = kb
                kaug_ref[hh, n * bs:(n + 1) * bs, dh:dh + LANES] = jnp.where(lane == n, 1.0, 0.0).astype(BF16)
                vt_ref[hh, n] = v_ref[n * bs:(n + 1) * bs, hh * dh:(hh + 1) * dh].astype(F32).T.astype(BF16)

    own = pl.multiple_of(qi * bs, bs)
    blk = lax.broadcasted_iota(jnp.int32, (nb, bs), 0)
    key = lax.broadcasted_iota(jnp.int32, (bs, bs), 0)
    qry = lax.broadcasted_iota(jnp.int32, (bs, bs), 1)
    for hh in heads:
        q = q_ref[:, hh * dh:(hh + 1) * dh]
        gate = lax.dot_general(kmean_ref[hh], q.astype(F32), nt,
                               precision=lax.Precision.HIGHEST, preferred_element_type=F32)
        gate = jnp.where(blk < qi, gate, -jnp.inf)
        rank = jnp.zeros((nb, bs), F32)
        for mm in range(nb):
            gm = gate[mm:mm + 1, :]
            beats = jnp.where(gm > gate, 1.0, jnp.where(jnp.logical_and(gm == gate, blk > mm), 1.0, 0.0))
            rank = rank + beats
        chosen = jnp.logical_and(rank < MOBA_TOPK, gate > -jnp.inf)
        bias_t = jnp.where(chosen, 0.0, MASK_BIAS)
        bias_q = jnp.concatenate([bias_t, jnp.zeros((LANES - nb, bs), F32)], axis=0).T
        qaug_ref[hh, :, 0:dh] = q
        qaug_ref[hh, :, dh:dh + LANES] = bias_q.astype(BF16)
        s = lax.dot_general(k_ref[pl.ds(own, bs), hh * dh:(hh + 1) * dh], q, nt,
                            preferred_element_type=F32) * c2
        s = jnp.where(key <= qry, s, MASK_BIAS)
        s_ref[hh, seq:seq + bs, :] = s
        m_ref[hh] = _fold_rows(s, "max")

    for c in range(nb // cbk):
        @pl.when(c * cbk < qi)
        def _(c=c):
            for hh in heads:
                s = lax.dot_general(kaug_ref[hh, c * ck:(c + 1) * ck, :], qaug_ref[hh], nt,
                                    preferred_element_type=F32) * c2
                s_ref[hh, c * ck:(c + 1) * ck, :] = s
                m_ref[hh] = jnp.maximum(m_ref[hh], _fold_rows(s, "max"))

    for hh in heads:
        m = jnp.max(m_ref[hh], axis=0, keepdims=True)
        m_ref[hh] = jnp.broadcast_to(m, (SUBLANES, bs))
        p = jnp.exp2(s_ref[hh, seq:seq + bs, :] - m)
        l_ref[hh] = _fold_rows(p, "sum")
        acc_ref[hh] = jnp.dot(vt_ref[hh, qi], p.astype(BF16), preferred_element_type=F32)

    for c in range(nb // cbk):
        @pl.when(c * cbk < qi)
        def _(c=c):
            for hh in heads:
                p = jnp.exp2(s_ref[hh, c * ck:(c + 1) * ck, :] - m_ref[hh, 0:1, :])
                l_ref[hh] = l_ref[hh] + _fold_rows(p, "sum")
                pb = p.astype(BF16)
                acc = acc_ref[hh]
                for j in range(cbk):
                    acc = acc + jnp.dot(vt_ref[hh, c * cbk + j], pb[j * bs:(j + 1) * bs, :],
                                        preferred_element_type=F32)
                acc_ref[hh] = acc

    for hh in heads:
        l = jnp.sum(l_ref[hh], axis=0, keepdims=True)
        o_ref[:, hh * dh:(hh + 1) * dh] = (acc_ref[hh] / l).T.astype(o_ref.dtype)


def _moba_attention(qkv, batch):
    m = qkv.shape[0]
    seq = m // batch
    dh = qkv.shape[1] // (3 * ATTN_HEADS)
    bs = MOBA_BLOCK
    nq = seq // bs
    hb = ATTN_HEADS_PER_STEP
    ng = ATTN_HEADS // hb
    assert dh == LANES and nq % ATTN_CHUNK_BLOCKS == 0 and nq <= LANES
    block_bytes = 2 * _nbytes((bs, hb * dh), BF16) + 2 * _nbytes((seq, hb * dh), BF16)
    scratch_shapes = [
        ((hb, nq, dh), F32),
        ((hb, seq, dh + LANES), BF16),
        ((hb, nq, dh, bs), BF16),
        ((hb, bs, dh + LANES), BF16),
        ((hb, seq + bs, bs), F32),
        ((hb, SUBLANES, bs), F32),
        ((hb, SUBLANES, bs), F32),
        ((hb, dh, bs), F32),
    ]
    return pl.pallas_call(
        _moba_body,
        grid=(batch, ng, nq),
        in_specs=[pl.BlockSpec((bs, hb * dh), lambda b, h, i: (b * nq + i, h)),
                  pl.BlockSpec((seq, hb * dh), lambda b, h, i: (b, ng + h)),
                  pl.BlockSpec((seq, hb * dh), lambda b, h, i: (b, 2 * ng + h))],
        out_specs=pl.BlockSpec((bs, hb * dh), lambda b, h, i: (b * nq + i, h)),
        out_shape=jax.ShapeDtypeStruct((m, ATTN_HEADS * dh), BF16),
        scratch_shapes=[pltpu.VMEM(s, d) for s, d in scratch_shapes],
        compiler_params=pltpu.CompilerParams(
            dimension_semantics=("arbitrary", "arbitrary", "arbitrary"),
            vmem_limit_bytes=_vmem_limit(block_bytes, sum(_nbytes(s, d) for s, d in scratch_shapes))),
        name="moba_attention",
    )(qkv, qkv, qkv)


def _mamba2_mixer(h, gain, w_in, layer, conv_w, conv_b, dt_bias, a_log, d_skip, norm_g, w_out, batch):
    d_inner = w_out.shape[1]
    n_zx = 2 * d_inner + 2 * SSM_GROUPS * SSM_STATE
    zx = _fused_matmul([h], [gain], [(w_in, layer)], [0], [], _ep_plain, BF16, tm=1024, tn=1024, n_out=n_zx,
                       name="ssm_in_proj")
    dt, dtt = _dt_proj(h, gain, w_in[layer][:, n_zx:])
    y = _ssd_scan(zx, dt, dtt, conv_w, conv_b.reshape(1, -1), dt_bias, a_log, d_skip, norm_g, batch)
    return _fused_matmul([y], [None], [(w_out, layer)], [0], [h], _ep_residual, F32, tm=1024, tn=512,
                         single_buffer_weights=True, name="ssm_out_proj")


def _moba_mixer(h, gain, w_qkv, w_o, layer, batch):
    qkv = _fused_matmul([h], [gain], [(w_qkv, layer)], [0], [], _ep_plain, BF16, tm=1024, tn=1024,
                        name="attn_qkv")
    o = _moba_attention(qkv, batch)
    return _fused_matmul([o], [None], [(w_o, layer)], [0], [h], _ep_residual, F32, tm=1024, tn=1024,
                         single_buffer_weights=True, name="attn_out")


def _ffn_and_ple(h, layer, ffn_norm_g, ple_norm_g, w_gate, w_up, w_down, p_l, w_pgate, w_pproj):
    a = _fused_matmul([h], [ffn_norm_g[layer]], [(w_gate, layer), (w_up, layer)], [0, 0], [], _ep_swiglu, BF16,
                      tm=1024, tn=512, name="ffn_up")
    h = _fused_matmul([a], [None], [(w_down, layer)], [0], [h], _ep_residual, F32, tm=1024, tn=512,
                      single_buffer_weights=True, name="ffn_down")
    return _fused_matmul([h, p_l], [ple_norm_g[layer], None], [(w_pgate, layer), (w_pproj, layer)], [0, 1], [h],
                         _ep_ple, F32, tm=1024, tn=512, name="ple")


def kernel(x, p, mix_norm_g, ffn_norm_g, ple_norm_g, ssm_w_in, ssm_conv_w, ssm_conv_b, ssm_dt_bias, ssm_a_log, ssm_d, ssm_norm_g, ssm_w_out, attn_w_qkv, attn_w_o, ffn_w_gate, ffn_w_up, ffn_w_down, ple_w_proj, ple_w_gate, final_norm_g):
    batch, seq, d = x.shape
    m = batch * seq
    depth = p.shape[0]
    h = x.reshape(m, d)
    for i in range(depth):
        j = i // 2
        if i % 2 == 0:
            h = _mamba2_mixer(h, mix_norm_g[i], ssm_w_in, j, ssm_conv_w[j], ssm_conv_b[j], ssm_dt_bias[j],
                              ssm_a_log[j], ssm_d[j], ssm_norm_g[j], ssm_w_out, batch)
        else:
            h = _moba_mixer(h, mix_norm_g[i], attn_w_qkv, attn_w_o, j, batch)
        h = _ffn_and_ple(h, i, ffn_norm_g, ple_norm_g, ffn_w_gate, ffn_w_up, ffn_w_down,
                         p[i].reshape(m, -1), ple_w_gate, ple_w_proj)
    return _rmsnorm(h, final_norm_g, F32).reshape(batch, seq, d)
```

```python
import functools

import jax
import jax.numpy as jnp
from jax import lax
from jax.experimental import pallas as pl
from jax.experimental.pallas import tpu as pltpu

NORM_EPS = 1e-6

SSM_HEAD_DIM = 64
SSM_GROUPS = 8
SSM_STATE = 128
CONV_WIDTH = 4
SSD_CHUNK = 128

ATTN_HEADS = 16
MOBA_BLOCK = 256
MOBA_TOPK = 3
ATTN_CHUNK_BLOCKS = 4
ATTN_HEADS_PER_STEP = 4
MASK_BIAS = -1e30
LOG2_E = 1.4426950408889634

LANES = 128
SUBLANES = 8
VMEM_BYTES = 64 * 1024 * 1024
COMPILER_SCRATCH_BYTES = 12 * 1024 * 1024

F32 = jnp.float32
BF16 = jnp.bfloat16


def _nbytes(shape, dtype):
    n = 1
    for s in shape:
        n *= s
    return n * jnp.dtype(dtype).itemsize


def _vmem_limit(block_bytes, scratch_bytes):
    need = 2 * block_bytes + scratch_bytes + COMPILER_SCRATCH_BYTES
    return int(min(need, VMEM_BYTES - 4 * 1024 * 1024))


def _sigmoid(v):
    return 0.5 * jnp.tanh(0.5 * v) + 0.5


def _silu(v):
    hv = 0.5 * v
    return hv * jnp.tanh(hv) + hv


def _softplus(v):
    return jnp.maximum(v, 0.0) + jnp.log1p(jnp.exp(-jnp.abs(v)))


def _rmsnorm_rows(x, g):
    ms = jnp.mean(x * x, axis=-1, keepdims=True)
    return x * lax.rsqrt(ms + NORM_EPS) * g


def _rmsnorm_body(x_ref, g_ref, o_ref):
    o_ref[...] = _rmsnorm_rows(x_ref[...], g_ref[...]).astype(o_ref.dtype)


def _rmsnorm(x, g, out_dtype, tm=512):
    m, d = x.shape
    blocks = _nbytes((tm, d), x.dtype) + _nbytes((tm, d), out_dtype) + _nbytes((1, d), F32)
    return pl.pallas_call(
        _rmsnorm_body,
        grid=(m // tm,),
        in_specs=[pl.BlockSpec((tm, d), lambda i: (i, 0)),
                  pl.BlockSpec((1, d), lambda i: (0, 0))],
        out_specs=pl.BlockSpec((tm, d), lambda i: (i, 0)),
        out_shape=jax.ShapeDtypeStruct((m, d), out_dtype),
        compiler_params=pltpu.CompilerParams(
            dimension_semantics=("arbitrary",), vmem_limit_bytes=_vmem_limit(blocks, 0)),
        name="rmsnorm",
    )(x, g.reshape(1, d))


def _mm_body(*refs, normed, w_x, w_kinds, n_extra, n_rows, n_out, epilogue):
    n_x, n_g, n_w = len(normed), sum(normed), len(w_x)
    refs = list(refs)
    x_refs = [refs.pop(0) for _ in range(n_x)]
    g_refs = [refs.pop(0) for _ in range(n_g)]
    w_refs = [refs.pop(0) for _ in range(n_w)]
    e_refs = [refs.pop(0) for _ in range(n_extra)]
    r_refs = [refs.pop(0) for _ in range(n_rows)]
    o_refs = [refs.pop(0) for _ in range(n_out)]
    wb_refs = {k: refs.pop(0) for k, kind in enumerate(w_kinds) if kind != "bf16"}

    @pl.when(pl.program_id(1) == 0)
    def _():
        for k, wb_ref in wb_refs.items():
            wb_ref[...] = w_refs[k][...].astype(BF16)

    xraw = [x_ref[...] for x_ref in x_refs]
    xs = []
    for x, is_normed in zip(xraw, normed):
        if is_normed:
            x = _rmsnorm_rows(x, g_refs.pop(0)[...])
        xs.append(x.astype(BF16))
    accs = []
    for k, (xi, kind) in enumerate(zip(w_x, w_kinds)):
        w = w_refs[k][...] if kind == "bf16" else wb_refs[k][...]
        dims = (((1,), (1,)), ((), ())) if kind == "f32_t" else (((1,), (0,)), ((), ()))
        accs.append(lax.dot_general(xs[xi], w, dims, preferred_element_type=F32))
    outs = epilogue(accs, [e_ref[...] for e_ref in e_refs], [r_ref[...] for r_ref in r_refs], xraw)
    for o_ref, o in zip(o_refs, outs):
        o_ref[...] = o.astype(o_ref.dtype)


def _fused_matmul(xs, gains, ws, w_x, extras, rows, epilogue, out_dtypes, tm, tn, n_out=None,
                  single_buffer_weights=False, name="matmul"):
    m = xs[0].shape[0]
    if n_out is None:
        w0, _, kind0 = ws[0]
        n = w0.shape[1] if kind0 == "f32_t" else w0.shape[-1]
    else:
        n = n_out
    grid = (n // tn, m // tm)
    in_specs, block_bytes, scratch, scratch_bytes = [], 0, [], 0
    for x in xs:
        k = x.shape[1]
        in_specs.append(pl.BlockSpec((tm, k), lambda j, i: (i, 0)))
        block_bytes += _nbytes((tm, k), x.dtype)
    gain_rows = [g.reshape(1, -1) for g in gains if g is not None]
    for g in gain_rows:
        in_specs.append(pl.BlockSpec(g.shape, lambda j, i: (0, 0)))
    w_mode = dict(pipeline_mode=pl.Buffered(1)) if single_buffer_weights else {}
    buffers = 1 if single_buffer_weights else 2
    for w, layer, kind in ws:
        if kind == "bf16":
            k = w.shape[0]
            in_specs.append(pl.BlockSpec((k, tn), lambda j, i: (0, j), **w_mode))
            block_bytes += _nbytes((k, tn), BF16) * buffers // 2
            continue
        if kind == "f32_t":
            k = w.shape[2]
            shape = (tn, k)
            in_specs.append(pl.BlockSpec((None, tn, k), functools.partial(lambda j, i, l: (l, j, 0), l=layer),
                                         **w_mode))
        else:
            k = w.shape[1]
            shape = (k, tn)
            in_specs.append(pl.BlockSpec((None, k, tn), functools.partial(lambda j, i, l: (l, 0, j), l=layer),
                                         **w_mode))
        block_bytes += _nbytes(shape, F32) * buffers // 2
        scratch.append(pltpu.VMEM(shape, BF16))
        scratch_bytes += _nbytes(shape, BF16)
    for e in extras:
        in_specs.append(pl.BlockSpec((tm, tn), lambda j, i: (i, j)))
        block_bytes += _nbytes((tm, tn), e.dtype)
    row_params = [r.reshape(1, -1) for r in rows]
    for r in row_params:
        in_specs.append(pl.BlockSpec((1, tn), lambda j, i: (0, j)))
    for dt in out_dtypes:
        block_bytes += _nbytes((tm, tn), dt)
    body = functools.partial(_mm_body, normed=tuple(g is not None for g in gains), w_x=tuple(w_x),
                             w_kinds=tuple(kind for _, _, kind in ws), n_extra=len(extras),
                             n_rows=len(rows), n_out=len(out_dtypes), epilogue=epilogue)
    outs = pl.pallas_call(
        body,
        grid=grid,
        in_specs=in_specs,
        out_specs=[pl.BlockSpec((tm, tn), lambda j, i: (i, j)) for _ in out_dtypes],
        out_shape=[jax.ShapeDtypeStruct((m, n), dt) for dt in out_dtypes],
        scratch_shapes=scratch,
        compiler_params=pltpu.CompilerParams(
            dimension_semantics=("arbitrary", "arbitrary"),
            vmem_limit_bytes=_vmem_limit(block_bytes, scratch_bytes)),
        name=name,
    )(*xs, *gain_rows, *[w for w, _, _ in ws], *extras, *row_params)
    return outs[0] if len(outs) == 1 else outs


def _ep_plain(accs, extras, rows, xraw):
    return (accs[0],)


def _ep_residual(accs, extras, rows, xraw):
    return (extras[0] + accs[0],)


def _ep_residual_norm(accs, extras, rows, xraw):
    h = extras[0] + accs[0]
    return h, _rmsnorm_rows(h, rows[0])


def _ep_swiglu(accs, extras, rows, xraw):
    return (_silu(accs[0]) * accs[1],)


def _ple_update(accs, xraw):
    return xraw[0] + _sigmoid(accs[0]) * accs[1]


def _ep_ple_norm(accs, extras, rows, xraw):
    h = _ple_update(accs, xraw)
    return h, _rmsnorm_rows(h, rows[0])


def _ep_ple_final(accs, extras, rows, xraw):
    return (_rmsnorm_rows(_ple_update(accs, xraw), rows[0]),)


def _dt_body(u_ref, wt_ref, dt_ref, dtt_ref):
    u = u_ref[...]
    wt = wt_ref[...].astype(BF16)
    nt = (((1,), (1,)), ((), ()))
    dt_ref[...] = lax.dot_general(u, wt, nt, preferred_element_type=F32)
    dtt_ref[...] = lax.dot_general(wt, u, nt, preferred_element_type=F32)


def _dt_proj(u, w_dt_t, tm=1024):
    m, k = u.shape
    wt_pad = jnp.pad(w_dt_t, ((0, LANES - w_dt_t.shape[0]), (0, 0)))
    blocks = _nbytes((tm, k), u.dtype) + _nbytes((LANES, k), F32) + 2 * _nbytes((tm, LANES), F32)
    return pl.pallas_call(
        _dt_body,
        grid=(m // tm,),
        in_specs=[pl.BlockSpec((tm, k), lambda i: (i, 0)),
                  pl.BlockSpec((LANES, k), lambda i: (0, 0))],
        out_specs=[pl.BlockSpec((tm, LANES), lambda i: (i, 0)),
                   pl.BlockSpec((LANES, tm), lambda i: (0, i))],
        out_shape=[jax.ShapeDtypeStruct((m, LANES), F32), jax.ShapeDtypeStruct((LANES, m), F32)],
        compiler_params=pltpu.CompilerParams(
            dimension_semantics=("arbitrary",), vmem_limit_bytes=_vmem_limit(blocks, 0)),
        name="dt_proj",
    )(u, wt_pad)


def _split3(v):
    hi = v.astype(BF16)
    r1 = v - hi.astype(F32)
    mid = r1.astype(BF16)
    lo = (r1 - mid.astype(F32)).astype(BF16)
    return hi, mid, lo


def _conv_silu(src_ref, halo_ref, buf_ref, w_ref, b_ref, g):
    q = SSD_CHUNK
    cur = src_ref[...].astype(F32)
    buf_ref[0:SUBLANES, :] = halo_ref[g]
    buf_ref[SUBLANES:SUBLANES + q, :] = cur
    halo_ref[g] = cur[q - SUBLANES:q, :]
    acc = b_ref[...] + w_ref[CONV_WIDTH - 1:CONV_WIDTH, :] * cur
    for k in range(CONV_WIDTH - 1):
        off = SUBLANES - (CONV_WIDTH - 1) + k
        acc = acc + w_ref[k:k + 1, :] * buf_ref[off:off + q, :]
    return _silu(acc)


def _ssd_body(z_ref, x_ref, b_ref, c_ref, dt_ref, dtt_ref,
              cwx_ref, cwb_ref, cwc_ref, cbx_ref, cbb_ref, cbc_ref,
              dtb_r_ref, dtb_c_ref, alog_r_ref, alog_c_ref, d_r_ref, ng_ref, e64_ref, e128_ref,
              y_ref,
              state_ref, hx_ref, hb_ref, hc_ref, xbuf_ref, bbuf_ref, cbuf_ref, dts3_ref, cs3_ref, cst_ref):
    t = pl.program_id(1)
    g = pl.program_id(2)
    q = SSD_CHUNK
    hpg = x_ref.shape[1] // SSM_HEAD_DIM
    row = lax.broadcasted_iota(jnp.int32, (q, q), 0)
    col = lax.broadcasted_iota(jnp.int32, (q, q), 1)
    tril = row >= col

    @pl.when(g == 0)
    def _():
        dts = _softplus(dt_ref[...] + dtb_r_ref[...])
        lo_tri = tril.astype(BF16)
        cs = jnp.zeros((q, LANES), F32)
        for part in _split3(dts * (-jnp.exp(alog_r_ref[...]))):
            cs = cs + jnp.dot(lo_tri, part, preferred_element_type=F32)
        dts3_ref[...] = jnp.concatenate(_split3(dts), axis=1)
        cs3_ref[...] = jnp.concatenate(_split3(cs), axis=1)
        up_tri = (row <= col).astype(BF16)
        cst = jnp.zeros((LANES, q), F32)
        for part in _split3(_softplus(dtt_ref[...] + dtb_c_ref[...]) * (-jnp.exp(alog_c_ref[...]))):
            cst = cst + jnp.dot(part, up_tri, preferred_element_type=F32)
        cst_ref[...] = cst

    @pl.when(t == 0)
    def _():
        state_ref[g] = jnp.zeros(state_ref.shape[1:], F32)
        hx_ref[g] = jnp.zeros(hx_ref.shape[1:], F32)
        hb_ref[g] = jnp.zeros(hb_ref.shape[1:], F32)
        hc_ref[g] = jnp.zeros(hc_ref.shape[1:], F32)

    xs = _conv_silu(x_ref, hx_ref, xbuf_ref, cwx_ref, cbx_ref, g)
    bm = _conv_silu(b_ref, hb_ref, bbuf_ref, cwb_ref, cbb_ref, g).astype(BF16)
    cm = _conv_silu(c_ref, hc_ref, cbuf_ref, cwc_ref, cbc_ref, g).astype(BF16)

    e64 = e64_ref[g]
    e128 = e128_ref[g]
    cs3 = cs3_ref[...]
    dt_x = jnp.dot(dts3_ref[...], e64, preferred_element_type=F32)
    cs_x = jnp.dot(cs3, e64, preferred_element_type=F32)
    cs_c = jnp.dot(cs3, e128, preferred_element_type=F32)
    d3 = jnp.concatenate(_split3(jnp.broadcast_to(d_r_ref[...], (SUBLANES, LANES))), axis=1)
    d_x = jnp.dot(d3, e64, preferred_element_type=F32)[0:1, :]
    cs_last = cs_x[q - 1:q, :]

    xdt = xs * dt_x
    xdt_b = xdt.astype(BF16)
    cb = lax.dot_general(cm, bm, (((1,), (1,)), ((), ())), preferred_element_type=F32)
    lane = lax.broadcasted_iota(jnp.int32, (q, LANES), 1)
    y_pairs = []
    for j in range(hpg // 2):
        xp = xdt_b[:, j * LANES:(j + 1) * LANES]
        outs = []
        for hh in range(2):
            h = 2 * j + hh
            colb = cs_c[:, h * LANES:(h + 1) * LANES]
            rowb = cst_ref[pl.ds(g * hpg + h, 1), :]
            decay = jnp.exp(jnp.where(tril, colb - rowb, -jnp.inf))
            outs.append(jnp.dot((cb * decay).astype(BF16), xp, preferred_element_type=F32))
        y_pairs.append(jnp.where(lane < SSM_HEAD_DIM, outs[0], outs[1]))
    y = jnp.concatenate(y_pairs, axis=1)

    st = state_ref[g]
    y = y + jnp.dot(cm, st.astype(BF16), preferred_element_type=F32) * jnp.exp(cs_x)
    xsc = (xdt * jnp.exp(cs_last - cs_x)).astype(BF16)
    s_new = lax.dot_general(bm, xsc, (((0,), (0,)), ((), ())), preferred_element_type=F32)
    state_ref[g] = st * jnp.exp(cs_last) + s_new

    y = y + d_x * xs
    yg = y * _silu(z_ref[...].astype(F32))
    y_ref[...] = _rmsnorm_rows(yg, ng_ref[...]).astype(y_ref.dtype)


def _expansion_matrices(heads_per_group, width):
    k = lax.broadcasted_iota(jnp.int32, (SSM_GROUPS, 3 * LANES, heads_per_group * width), 1) % LANES
    c = lax.broadcasted_iota(jnp.int32, (SSM_GROUPS, 3 * LANES, heads_per_group * width), 2) // width
    gi = lax.broadcasted_iota(jnp.int32, (SSM_GROUPS, 3 * LANES, heads_per_group * width), 0)
    return (k == gi * heads_per_group + c).astype(BF16)


def _ssd_scan(zx, dt, dtt, conv_w, conv_b, dt_bias, a_log, d_skip, norm_g, batch):
    m = zx.shape[0]
    q = SSD_CHUNK
    gn = SSM_GROUPS * SSM_STATE
    d_inner = (zx.shape[1] - 2 * gn) // 2
    gw = d_inner // SSM_GROUPS
    hpg = gw // SSM_HEAD_DIM
    nt = m // batch // q
    xb0 = d_inner // gw
    bb0 = 2 * d_inner // SSM_STATE
    cb0 = bb0 + SSM_GROUPS
    cwb0 = d_inner // SSM_STATE
    cwc0 = cwb0 + SSM_GROUPS

    def pad_row(v):
        return jnp.pad(v, (0, LANES - v.shape[0])).reshape(1, LANES)

    def pad_col(v):
        return jnp.pad(v, (0, LANES - v.shape[0])).reshape(LANES, 1)

    e64 = _expansion_matrices(hpg, SSM_HEAD_DIM)
    e128 = _expansion_matrices(hpg, LANES)
    row_map = lambda b, t, g: (b * nt + t, 0)
    const2 = lambda b, t, g: (0, 0)
    const3 = lambda b, t, g: (0, 0, 0)
    in_specs = [
        pl.BlockSpec((q, gw), lambda b, t, g: (b * nt + t, g)),
        pl.BlockSpec((q, gw), lambda b, t, g: (b * nt + t, xb0 + g)),
        pl.BlockSpec((q, SSM_STATE), lambda b, t, g: (b * nt + t, bb0 + g)),
        pl.BlockSpec((q, SSM_STATE), lambda b, t, g: (b * nt + t, cb0 + g)),
        pl.BlockSpec((q, LANES), row_map),
        pl.BlockSpec((LANES, q), lambda b, t, g: (0, b * nt + t)),
        pl.BlockSpec((CONV_WIDTH, gw), lambda b, t, g: (0, g)),
        pl.BlockSpec((CONV_WIDTH, SSM_STATE), lambda b, t, g: (0, cwb0 + g)),
        pl.BlockSpec((CONV_WIDTH, SSM_STATE), lambda b, t, g: (0, cwc0 + g)),
        pl.BlockSpec((1, gw), lambda b, t, g: (0, g)),
        pl.BlockSpec((1, SSM_STATE), lambda b, t, g: (0, cwb0 + g)),
        pl.BlockSpec((1, SSM_STATE), lambda b, t, g: (0, cwc0 + g)),
        pl.BlockSpec((1, LANES), const2),
        pl.BlockSpec((LANES, 1), const2),
        pl.BlockSpec((1, LANES), const2),
        pl.BlockSpec((LANES, 1), const2),
        pl.BlockSpec((1, LANES), const2),
        pl.BlockSpec((1, gw), lambda b, t, g: (0, g)),
        pl.BlockSpec(e64.shape, const3),
        pl.BlockSpec(e128.shape, const3),
    ]
    scratch = [
        pltpu.VMEM((SSM_GROUPS, SSM_STATE, gw), F32),
        pltpu.VMEM((SSM_GROUPS, SUBLANES, gw), F32),
        pltpu.VMEM((SSM_GROUPS, SUBLANES, SSM_STATE), F32),
        pltpu.VMEM((SSM_GROUPS, SUBLANES, SSM_STATE), F32),
        pltpu.VMEM((SUBLANES + q, gw), F32),
        pltpu.VMEM((SUBLANES + q, SSM_STATE), F32),
        pltpu.VMEM((SUBLANES + q, SSM_STATE), F32),
        pltpu.VMEM((q, 3 * LANES), BF16),
        pltpu.VMEM((q, 3 * LANES), BF16),
        pltpu.VMEM((LANES, q), F32),
    ]
    block_bytes = (3 * _nbytes((q, gw), BF16) + 2 * _nbytes((q, SSM_STATE), BF16)
                   + 2 * _nbytes((q, LANES), F32) + _nbytes(e64.shape, BF16) + _nbytes(e128.shape, BF16)
                   + 16 * _nbytes((SUBLANES, gw), F32))
    scratch_bytes = (_nbytes((SSM_GROUPS, SSM_STATE + 3 * SUBLANES, gw), F32)
                     + _nbytes((SUBLANES + q, gw + 2 * SSM_STATE), F32) + 3 * _nbytes((q, LANES), F32))
    return pl.pallas_call(
        _ssd_body,
        grid=(batch, nt, SSM_GROUPS),
        in_specs=in_specs,
        out_specs=pl.BlockSpec((q, gw), lambda b, t, g: (b * nt + t, g)),
        out_shape=jax.ShapeDtypeStruct((m, d_inner), BF16),
        scratch_shapes=scratch,
        compiler_params=pltpu.CompilerParams(
            dimension_semantics=("arbitrary", "arbitrary", "arbitrary"),
            vmem_limit_bytes=_vmem_limit(block_bytes, scratch_bytes)),
        name="ssd_scan",
    )(zx, zx, zx, zx, dt, dtt, conv_w, conv_w, conv_w, conv_b, conv_b, conv_b,
      pad_row(dt_bias), pad_col(dt_bias), pad_row(a_log), pad_col(a_log), pad_row(d_skip),
      norm_g.reshape(1, d_inner), e64, e128)


def _fold_rows(v, op):
    r, c = v.shape
    v3 = v.reshape(r // SUBLANES, SUBLANES, c)
    return jnp.max(v3, axis=0) if op == "max" else jnp.sum(v3, axis=0)


def _moba_body(q_ref, k_ref, v_ref, o_ref,
               kmean_ref, kaug_ref, vt_ref, qaug_ref, s_ref, m_ref, l_ref, acc_ref):
    qi = pl.program_id(2)
    bs = MOBA_BLOCK
    dh = LANES
    seq = k_ref.shape[0]
    nb = seq // bs
    heads = range(q_ref.shape[1] // dh)
    cbk = ATTN_CHUNK_BLOCKS
    ck = cbk * bs
    c2 = dh ** -0.5 * LOG2_E
    nt = (((1,), (1,)), ((), ()))

    @pl.when(qi == 0)
    def _():
        lane = lax.broadcasted_iota(jnp.int32, (bs, LANES), 1)
        for hh in heads:
            for n in range(nb):
                kb = k_ref[n * bs:(n + 1) * bs, hh * dh:(hh + 1) * dh]
                kmean_ref[hh, n:n + 1, :] = jnp.mean(kb.astype(F32), axis=0, keepdims=True)
                kaug_ref[hh, n * bs:(n + 1) * bs, 0:dh] = kb
                kaug_ref[hh, n * bs:(n + 1) * bs, dh:dh + LANES] = jnp.where(lane == n, 1.0, 0.0).astype(BF16)
                vt_ref[hh, n] = v_ref[n * bs:(n + 1) * bs, hh * dh:(hh + 1) * dh].astype(F32).T.astype(BF16)

    own = pl.multiple_of(qi * bs, bs)
    blk = lax.broadcasted_iota(jnp.int32, (nb, bs), 0)
    key = lax.broadcasted_iota(jnp.int32, (bs, bs), 0)
    qry = lax.broadcasted_iota(jnp.int32, (bs, bs), 1)
    for hh in heads:
        q = q_ref[:, hh * dh:(hh + 1) * dh]
        gate = lax.dot_general(kmean_ref[hh], q.astype(F32), nt,
                               precision=lax.Precision.HIGHEST, preferred_element_type=F32)
        gate = jnp.where(blk < qi, gate, -jnp.inf)
        rank = jnp.zeros((nb, bs), F32)
        for mm in range(nb):
            gm = gate[mm:mm + 1, :]
            beats = jnp.where(gm > gate, 1.0, jnp.where(jnp.logical_and(gm == gate, blk > mm), 1.0, 0.0))
            rank = rank + beats
        chosen = jnp.logical_and(rank < MOBA_TOPK, gate > -jnp.inf)
        bias_t = jnp.where(chosen, 0.0, MASK_BIAS)
        bias_q = jnp.concatenate([bias_t, jnp.zeros((LANES - nb, bs), F32)], axis=0).T
        qaug_ref[hh, :, 0:dh] = q
        qaug_ref[hh, :, dh:dh + LANES] = bias_q.astype(BF16)
        s = lax.dot_general(k_ref[pl.ds(own, bs), hh * dh:(hh + 1) * dh], q, nt,
                            preferred_element_type=F32) * c2
        s = jnp.where(key <= qry, s, MASK_BIAS)
        s_ref[hh, seq:seq + bs, :] = s
        m_ref[hh] = _fold_rows(s, "max")

    for c in range(nb // cbk):
        @pl.when(c * cbk < qi)
        def _(c=c):
            for hh in heads:
                s = lax.dot_general(kaug_ref[hh, c * ck:(c + 1) * ck, :], qaug_ref[hh], nt,
                                    preferred_element_type=F32) * c2
                s_ref[hh, c * ck:(c + 1) * ck, :] = s
                m_ref[hh] = jnp.maximum(m_ref[hh], _fold_rows(s, "max"))

    for hh in heads:
        m = jnp.max(m_ref[hh], axis=0, keepdims=True)
        m_ref[hh] = jnp.broadcast_to(m, (SUBLANES, bs))
        p = jnp.exp2(s_ref[hh, seq:seq + bs, :] - m)
        l_ref[hh] = _fold_rows(p, "sum")
        acc_ref[hh] = jnp.dot(vt_ref[hh, qi], p.astype(BF16), preferred_element_type=F32)

    for c in range(nb // cbk):
        @pl.when(c * cbk < qi)
        def _(c=c):
            for hh in heads:
                p = jnp.exp2(s_ref[hh, c * ck:(c + 1) * ck, :] - m_ref[hh, 0:1, :])
                l_ref[hh] = l_ref[hh] + _fold_rows(p, "sum")
                pb = p.astype(BF16)
                acc = acc_ref[hh]
                for j in range(cbk):
                    acc = acc + jnp.dot(vt_ref[hh, c * cbk + j], pb[j * bs:(j + 1) * bs, :],
                                        preferred_element_type=F32)
                acc_ref[hh] = acc

    for hh in heads:
        l = jnp.sum(l_ref[hh], axis=0, keepdims=True)
        o_ref[:, hh * dh:(hh + 1) * dh] = (acc_ref[hh] / l).T.astype(o_ref.dtype)


def _moba_attention(qkv, batch):
    m = qkv.shape[0]
    seq = m // batch
    dh = qkv.shape[1] // (3 * ATTN_HEADS)
    bs = MOBA_BLOCK
    nq = seq // bs
    hb = ATTN_HEADS_PER_STEP
    ng = ATTN_HEADS // hb
    assert dh == LANES and nq % ATTN_CHUNK_BLOCKS == 0 and nq <= LANES
    block_bytes = 2 * _nbytes((bs, hb * dh), BF16) + 2 * _nbytes((seq, hb * dh), BF16)
    scratch_shapes = [
        ((hb, nq, dh), F32),
        ((hb, seq, dh + LANES), BF16),
        ((hb, nq, dh, bs), BF16),
        ((hb, bs, dh + LANES), BF16),
        ((hb, seq + bs, bs), F32),
        ((hb, SUBLANES, bs), F32),
        ((hb, SUBLANES, bs), F32),
        ((hb, dh, bs), F32),
    ]
    return pl.pallas_call(
        _moba_body,
        grid=(batch, ng, nq),
        in_specs=[pl.BlockSpec((bs, hb * dh), lambda b, h, i: (b * nq + i, h)),
                  pl.BlockSpec((seq, hb * dh), lambda b, h, i: (b, ng + h)),
                  pl.BlockSpec((seq, hb * dh), lambda b, h, i: (b, 2 * ng + h))],
        out_specs=pl.BlockSpec((bs, hb * dh), lambda b, h, i: (b * nq + i, h)),
        out_shape=jax.ShapeDtypeStruct((m, ATTN_HEADS * dh), BF16),
        scratch_shapes=[pltpu.VMEM(s, d) for s, d in scratch_shapes],
        compiler_params=pltpu.CompilerParams(
            dimension_semantics=("arbitrary", "arbitrary", "arbitrary"),
            vmem_limit_bytes=_vmem_limit(block_bytes, sum(_nbytes(s, d) for s, d in scratch_shapes))),
        name="moba_attention",
    )(qkv, qkv, qkv)


def _mamba2_mixer(h, u, w_in_t, layer, conv_w, conv_b, dt_bias, a_log, d_skip, norm_g, w_out, batch):
    d_inner = w_out.shape[1]
    n_zx = 2 * d_inner + 2 * SSM_GROUPS * SSM_STATE
    zx = _fused_matmul([u], [None], [(w_in_t, layer, "f32_t")], [0], [], [], _ep_plain, (BF16,),
                       tm=2048, tn=1024, n_out=n_zx, name="ssm_in_proj")
    dt, dtt = _dt_proj(u, w_in_t[layer][n_zx:, :])
    y = _ssd_scan(zx, dt, dtt, conv_w, conv_b.reshape(1, -1), dt_bias, a_log, d_skip, norm_g, batch)
    h = _fused_matmul([y], [None], [(w_out, layer, "f32")], [0], [h], [], _ep_residual, (F32,),
                      tm=1024, tn=512, single_buffer_weights=True, name="ssm_out_proj")
    return h, None


def _moba_mixer(h, u, w_qkv, w_o, layer, next_gain, batch):
    qkv = _fused_matmul([u], [None], [(w_qkv, layer, "f32")], [0], [], [], _ep_plain, (BF16,),
                        tm=2048, tn=1024, name="attn_qkv")
    o = _moba_attention(qkv, batch)
    d = h.shape[1]
    return _fused_matmul([o], [None], [(w_o[layer].astype(BF16), None, "bf16")], [0], [h], [next_gain],
                         _ep_residual_norm, (F32, BF16), tm=512, tn=d, single_buffer_weights=True,
                         name="attn_out")


def _ffn(h, u, layer, gain, w_gate, w_up, w_down):
    ws = [(w_gate, layer, "f32"), (w_up, layer, "f32")]
    if u is None:
        a = _fused_matmul([h], [gain], ws, [0, 0], [], [], _ep_swiglu, (BF16,), tm=1024, tn=512, name="ffn_up")
    else:
        a = _fused_matmul([u], [None], ws, [0, 0], [], [], _ep_swiglu, (BF16,), tm=1024, tn=512, name="ffn_up")
    return _fused_matmul([a], [None], [(w_down, layer, "f32")], [0], [h], [], _ep_residual, (F32,),
                         tm=1024, tn=512, single_buffer_weights=True, name="ffn_down")


def _ple(h, p_l, gain, w_pgate_l, w_pproj_l, out_gain, last):
    d = h.shape[1]
    ws = [(w_pgate_l.astype(BF16), None, "bf16"), (w_pproj_l.astype(BF16), None, "bf16")]
    if last:
        return _fused_matmul([h, p_l], [gain, None], ws, [0, 1], [], [out_gain], _ep_ple_final, (F32,),
                             tm=512, tn=d, single_buffer_weights=True, name="ple_final")
    return _fused_matmul([h, p_l], [gain, None], ws, [0, 1], [], [out_gain], _ep_ple_norm, (F32, BF16),
                         tm=512, tn=d, single_buffer_weights=True, name="ple")


def kernel(x, p, mix_norm_g, ffn_norm_g, ple_norm_g, ssm_w_in, ssm_conv_w, ssm_conv_b, ssm_dt_bias, ssm_a_log, ssm_d, ssm_norm_g, ssm_w_out, attn_w_qkv, attn_w_o, ffn_w_gate, ffn_w_up, ffn_w_down, ple_w_proj, ple_w_gate, final_norm_g):
    batch, seq, d = x.shape
    m = batch * seq
    depth = p.shape[0]
    ssm_w_in_t = jnp.swapaxes(ssm_w_in, 1, 2)
    h = x.reshape(m, d)
    u = _rmsnorm(h, mix_norm_g[0], BF16)
    for i in range(depth):
        j = i // 2
        if i % 2 == 0:
            h, u = _mamba2_mixer(h, u, ssm_w_in_t, j, ssm_conv_w[j], ssm_conv_b[j], ssm_dt_bias[j],
                                 ssm_a_log[j], ssm_d[j], ssm_norm_g[j], ssm_w_out, batch)
        else:
            h, u = _moba_mixer(h, u, attn_w_qkv, attn_w_o, j, ffn_norm_g[i], batch)
        h = _ffn(h, u, i, ffn_norm_g[i], ffn_w_gate, ffn_w_up, ffn_w_down)
        last = i == depth - 1
        out = _ple(h, p[i].reshape(m, -1), ple_norm_g[i], ple_w_gate[i], ple_w_proj[i],
                   final_norm_g if last else mix_norm_g[i + 1], last)
        if not last:
            h, u = out
    return out.reshape(batch, seq, d)
```

```python
import functools

import jax
import jax.numpy as jnp
from jax import lax
from jax.experimental import pallas as pl
from jax.experimental.pallas import tpu as pltpu

NORM_EPS = 1e-6

SSM_HEAD_DIM = 64
SSM_GROUPS = 8
SSM_STATE = 128
CONV_WIDTH = 4
SSD_CHUNK = 128
SSD_GROUPS_PER_STEP = 8

ATTN_HEADS = 16
MOBA_BLOCK = 256
MOBA_TOPK = 3
ATTN_CHUNK_BLOCKS = 4
ATTN_HEADS_PER_STEP = 4
MASK_BIAS = -1e30
LOG2_E = 1.4426950408889634

LANES = 128
SUBLANES = 8
VMEM_BYTES = 64 * 1024 * 1024
COMPILER_SCRATCH_BYTES = 12 * 1024 * 1024

F32 = jnp.float32
BF16 = jnp.bfloat16


def _nbytes(shape, dtype):
    n = 1
    for s in shape:
        n *= s
    return n * jnp.dtype(dtype).itemsize


def _vmem_limit(block_bytes, scratch_bytes):
    need = 2 * block_bytes + scratch_bytes + COMPILER_SCRATCH_BYTES
    return int(min(need, VMEM_BYTES - 4 * 1024 * 1024))


def _sigmoid(v):
    return 0.5 * jnp.tanh(0.5 * v) + 0.5


def _silu(v):
    hv = 0.5 * v
    return hv * jnp.tanh(hv) + hv


def _softplus(v):
    return jnp.maximum(v, 0.0) + jnp.log1p(jnp.exp(-jnp.abs(v)))


def _rmsnorm_rows(x, g):
    ms = jnp.mean(x * x, axis=-1, keepdims=True)
    return x * lax.rsqrt(ms + NORM_EPS) * g


def _mm_body(*refs, normed, w_x, w_kinds, n_extra, n_rows, n_out, epilogue):
    n_x, n_g, n_w = len(normed), sum(normed), len(w_x)
    refs = list(refs)
    x_refs = [refs.pop(0) for _ in range(n_x)]
    g_refs = [refs.pop(0) for _ in range(n_g)]
    w_refs = [refs.pop(0) for _ in range(n_w)]
    e_refs = [refs.pop(0) for _ in range(n_extra)]
    r_refs = [refs.pop(0) for _ in range(n_rows)]
    o_refs = [refs.pop(0) for _ in range(n_out)]
    wb_refs = {k: refs.pop(0) for k, kind in enumerate(w_kinds) if kind != "bf16"}

    @pl.when(pl.program_id(1) == 0)
    def _():
        for k, wb_ref in wb_refs.items():
            wb_ref[...] = w_refs[k][...].astype(BF16)

    xraw = [x_ref[...] for x_ref in x_refs]
    xs = []
    for x, is_normed in zip(xraw, normed):
        if is_normed:
            x = _rmsnorm_rows(x, g_refs.pop(0)[...])
        xs.append(x.astype(BF16))
    accs = []
    for k, (xi, kind) in enumerate(zip(w_x, w_kinds)):
        w = w_refs[k][...] if kind == "bf16" else wb_refs[k][...]
        dims = (((1,), (1,)), ((), ())) if kind == "f32_t" else (((1,), (0,)), ((), ()))
        accs.append(lax.dot_general(xs[xi], w, dims, preferred_element_type=F32))
    outs = epilogue(accs, [e_ref[...] for e_ref in e_refs], [r_ref[...] for r_ref in r_refs], xraw)
    for o_ref, o in zip(o_refs, outs):
        o_ref[...] = o.astype(o_ref.dtype)


def _fused_matmul(xs, gains, ws, w_x, extras, rows, epilogue, out_dtypes, tm, tn, n_out=None,
                  single_buffer_weights=False, name="matmul"):
    m = xs[0].shape[0]
    if n_out is None:
        w0, _, kind0 = ws[0]
        n = w0.shape[1] if kind0 == "f32_t" else w0.shape[-1]
    else:
        n = n_out
    grid = (n // tn, m // tm)
    in_specs, block_bytes, scratch, scratch_bytes = [], 0, [], 0
    for x in xs:
        k = x.shape[1]
        in_specs.append(pl.BlockSpec((tm, k), lambda j, i: (i, 0)))
        block_bytes += _nbytes((tm, k), x.dtype)
    gain_rows = [g.reshape(1, -1) for g in gains if g is not None]
    for g in gain_rows:
        in_specs.append(pl.BlockSpec(g.shape, lambda j, i: (0, 0)))
    w_mode = dict(pipeline_mode=pl.Buffered(1)) if single_buffer_weights else {}
    buffers = 1 if single_buffer_weights else 2
    for w, layer, kind in ws:
        if kind == "bf16":
            k = w.shape[0]
            in_specs.append(pl.BlockSpec((k, tn), lambda j, i: (0, j), **w_mode))
            block_bytes += _nbytes((k, tn), BF16) * buffers // 2
            continue
        if kind == "f32_t":
            k = w.shape[2]
            shape = (tn, k)
            in_specs.append(pl.BlockSpec((None, tn, k), functools.partial(lambda j, i, l: (l, j, 0), l=layer),
                                         **w_mode))
        else:
            k = w.shape[1]
            shape = (k, tn)
            in_specs.append(pl.BlockSpec((None, k, tn), functools.partial(lambda j, i, l: (l, 0, j), l=layer),
                                         **w_mode))
        block_bytes += _nbytes(shape, F32) * buffers // 2
        scratch.append(pltpu.VMEM(shape, BF16))
        scratch_bytes += _nbytes(shape, BF16)
    for e in extras:
        in_specs.append(pl.BlockSpec((tm, tn), lambda j, i: (i, j)))
        block_bytes += _nbytes((tm, tn), e.dtype)
    row_params = [r.reshape(1, -1) for r in rows]
    for r in row_params:
        in_specs.append(pl.BlockSpec((1, tn), lambda j, i: (0, j)))
    for dt in out_dtypes:
        block_bytes += _nbytes((tm, tn), dt)
    body = functools.partial(_mm_body, normed=tuple(g is not None for g in gains), w_x=tuple(w_x),
                             w_kinds=tuple(kind for _, _, kind in ws), n_extra=len(extras),
                             n_rows=len(rows), n_out=len(out_dtypes), epilogue=epilogue)
    outs = pl.pallas_call(
        body,
        grid=grid,
        in_specs=in_specs,
        out_specs=[pl.BlockSpec((tm, tn), lambda j, i: (i, j)) for _ in out_dtypes],
        out_shape=[jax.ShapeDtypeStruct((m, n), dt) for dt in out_dtypes],
        scratch_shapes=scratch,
        compiler_params=pltpu.CompilerParams(
            dimension_semantics=("arbitrary", "arbitrary"),
            vmem_limit_bytes=_vmem_limit(block_bytes, scratch_bytes)),
        name=name,
    )(*xs, *gain_rows, *[w for w, _, _ in ws], *extras, *row_params)
    return outs[0] if len(outs) == 1 else outs


def _ep_plain(accs, extras, rows, xraw):
    return (accs[0],)


def _ep_residual(accs, extras, rows, xraw):
    return (extras[0] + accs[0],)


def _ep_residual_norm(accs, extras, rows, xraw):
    h = extras[0] + accs[0]
    return h, _rmsnorm_rows(h, rows[0])


def _ep_swiglu(accs, extras, rows, xraw):
    return (_silu(accs[0]) * accs[1],)


def _ple_update(accs, xraw):
    return xraw[0] + _sigmoid(accs[0]) * accs[1]


def _ep_ple_norm(accs, extras, rows, xraw):
    h = _ple_update(accs, xraw)
    return h, _rmsnorm_rows(h, rows[0])


def _ep_ple_final(accs, extras, rows, xraw):
    return (_rmsnorm_rows(_ple_update(accs, xraw), rows[0]),)


def _dt_body(*refs, normed):
    nt = (((1,), (1,)), ((), ()))
    if normed:
        x_ref, g_ref, wt_ref, u_ref, dt_ref, dtt_ref = refs
        u = _rmsnorm_rows(x_ref[...], g_ref[...]).astype(BF16)
        u_ref[...] = u
    else:
        x_ref, wt_ref, dt_ref, dtt_ref = refs
        u = x_ref[...]
    wt = wt_ref[...].astype(BF16)
    dt_ref[...] = lax.dot_general(u, wt, nt, preferred_element_type=F32)
    dtt_ref[...] = lax.dot_general(wt, u, nt, preferred_element_type=F32)


def _dt_proj(x, gain, w_dt_t, tm=1024):
    m, k = x.shape
    normed = gain is not None
    wt_pad = jnp.pad(w_dt_t, ((0, LANES - w_dt_t.shape[0]), (0, 0)))
    row_spec = pl.BlockSpec((tm, k), lambda i: (i, 0))
    in_specs = [row_spec] + ([pl.BlockSpec((1, k), lambda i: (0, 0))] if normed else [])
    in_specs.append(pl.BlockSpec((LANES, k), lambda i: (0, 0)))
    out_specs = [pl.BlockSpec((tm, LANES), lambda i: (i, 0)), pl.BlockSpec((LANES, tm), lambda i: (0, i))]
    out_shape = [jax.ShapeDtypeStruct((m, LANES), F32), jax.ShapeDtypeStruct((LANES, m), F32)]
    if normed:
        out_specs.insert(0, row_spec)
        out_shape.insert(0, jax.ShapeDtypeStruct((m, k), BF16))
    blocks = (_nbytes((tm, k), x.dtype) + _nbytes((LANES, k), F32) + 2 * _nbytes((tm, LANES), F32)
              + (_nbytes((tm, k), BF16) if normed else 0))
    args = (x, gain.reshape(1, k), wt_pad) if normed else (x, wt_pad)
    return pl.pallas_call(
        functools.partial(_dt_body, normed=normed),
        grid=(m // tm,),
        in_specs=in_specs,
        out_specs=out_specs,
        out_shape=out_shape,
        compiler_params=pltpu.CompilerParams(
            dimension_semantics=("arbitrary",), vmem_limit_bytes=_vmem_limit(blocks, 0)),
        name="dt_proj",
    )(*args)


def _split3(v):
    hi = v.astype(BF16)
    r1 = v - hi.astype(F32)
    mid = r1.astype(BF16)
    lo = (r1 - mid.astype(F32)).astype(BF16)
    return hi, mid, lo


def _conv_silu(src_ref, halo_ref, buf_ref, w_ref, b_ref, g):
    q = SSD_CHUNK
    cur = src_ref[...].astype(F32)
    buf_ref[0:SUBLANES, :] = halo_ref[g]
    buf_ref[SUBLANES:SUBLANES + q, :] = cur
    halo_ref[g] = cur[q - SUBLANES:q, :]
    acc = b_ref[...] + w_ref[CONV_WIDTH - 1:CONV_WIDTH, :] * cur
    for k in range(CONV_WIDTH - 1):
        off = SUBLANES - (CONV_WIDTH - 1) + k
        acc = acc + w_ref[k:k + 1, :] * buf_ref[off:off + q, :]
    return _silu(acc)


def _ssd_body(z_ref, x_ref, b_ref, c_ref, dt_ref, dtt_ref,
              cwx_ref, cwb_ref, cwc_ref, cbx_ref, cbb_ref, cbc_ref,
              dtb_r_ref, dtb_c_ref, alog_r_ref, alog_c_ref, d_r_ref, ng_ref, e64_ref, e128_ref,
              y_ref,
              state_ref, hx_ref, hb_ref, hc_ref, xbuf_ref, bbuf_ref, cbuf_ref, dts3_ref, cs3_ref, cst_ref):
    t = pl.program_id(1)
    g = pl.program_id(2)
    q = SSD_CHUNK
    gps = b_ref.shape[1] // SSM_STATE
    gw = x_ref.shape[1] // gps
    hpg = gw // SSM_HEAD_DIM
    row = lax.broadcasted_iota(jnp.int32, (q, q), 0)
    col = lax.broadcasted_iota(jnp.int32, (q, q), 1)
    tril = row >= col

    @pl.when(g == 0)
    def _():
        dts = _softplus(dt_ref[...] + dtb_r_ref[...])
        lo_tri = tril.astype(BF16)
        cs = jnp.zeros((q, LANES), F32)
        for part in _split3(dts * (-jnp.exp(alog_r_ref[...]))):
            cs = cs + jnp.dot(lo_tri, part, preferred_element_type=F32)
        dts3_ref[...] = jnp.concatenate(_split3(dts), axis=1)
        cs3_ref[...] = jnp.concatenate(_split3(cs), axis=1)
        up_tri = (row <= col).astype(BF16)
        cst = jnp.zeros((LANES, q), F32)
        for part in _split3(_softplus(dtt_ref[...] + dtb_c_ref[...]) * (-jnp.exp(alog_c_ref[...]))):
            cst = cst + jnp.dot(part, up_tri, preferred_element_type=F32)
        cst_ref[...] = cst

    lane = lax.broadcasted_iota(jnp.int32, (q, LANES), 1)
    d3 = jnp.concatenate(_split3(jnp.broadcast_to(d_r_ref[...], (SUBLANES, LANES))), axis=1)
    for s in range(gps):
        gi = g * gps + s
        xcols = slice(s * gw, (s + 1) * gw)
        ncols = slice(s * SSM_STATE, (s + 1) * SSM_STATE)

        @pl.when(t == 0)
        def _(gi=gi):
            state_ref[gi] = jnp.zeros(state_ref.shape[1:], F32)
            hx_ref[gi] = jnp.zeros(hx_ref.shape[1:], F32)
            hb_ref[gi] = jnp.zeros(hb_ref.shape[1:], F32)
            hc_ref[gi] = jnp.zeros(hc_ref.shape[1:], F32)

        xs = _conv_silu(x_ref.at[:, xcols], hx_ref, xbuf_ref.at[s], cwx_ref.at[:, xcols],
                        cbx_ref.at[:, xcols], gi)
        bm = _conv_silu(b_ref.at[:, ncols], hb_ref, bbuf_ref.at[s], cwb_ref.at[:, ncols],
                        cbb_ref.at[:, ncols], gi).astype(BF16)
        cm = _conv_silu(c_ref.at[:, ncols], hc_ref, cbuf_ref.at[s], cwc_ref.at[:, ncols],
                        cbc_ref.at[:, ncols], gi).astype(BF16)

        e64 = e64_ref[gi]
        e128 = e128_ref[gi]
        cs3 = cs3_ref[...]
        dt_x = jnp.dot(dts3_ref[...], e64, preferred_element_type=F32)
        cs_x = jnp.dot(cs3, e64, preferred_element_type=F32)
        cs_c = jnp.dot(cs3, e128, preferred_element_type=F32)
        d_x = jnp.dot(d3, e64, preferred_element_type=F32)[0:1, :]
        cs_last = cs_x[q - 1:q, :]

        xdt = xs * dt_x
        xdt_b = xdt.astype(BF16)
        cb = lax.dot_general(cm, bm, (((1,), (1,)), ((), ())), preferred_element_type=F32)
        y_pairs = []
        for j in range(hpg // 2):
            xp = xdt_b[:, j * LANES:(j + 1) * LANES]
            outs = []
            for hh in range(2):
                h = 2 * j + hh
                colb = cs_c[:, h * LANES:(h + 1) * LANES]
                rowb = cst_ref[pl.ds(gi * hpg + h, 1), :]
                decay = jnp.exp(jnp.where(tril, colb - rowb, -jnp.inf))
                outs.append(jnp.dot((cb * decay).astype(BF16), xp, preferred_element_type=F32))
            y_pairs.append(jnp.where(lane < SSM_HEAD_DIM, outs[0], outs[1]))
        y = jnp.concatenate(y_pairs, axis=1)

        st = state_ref[gi]
        y = y + jnp.dot(cm, st.astype(BF16), preferred_element_type=F32) * jnp.exp(cs_x)
        xsc = (xdt * jnp.exp(cs_last - cs_x)).astype(BF16)
        s_new = lax.dot_general(bm, xsc, (((0,), (0,)), ((), ())), preferred_element_type=F32)
        state_ref[gi] = st * jnp.exp(cs_last) + s_new

        y = y + d_x * xs
        yg = y * _silu(z_ref[:, xcols].astype(F32))
        y_ref[:, xcols] = _rmsnorm_rows(yg, ng_ref[:, xcols]).astype(y_ref.dtype)


def _expansion_matrices(heads_per_group, width):
    k = lax.broadcasted_iota(jnp.int32, (SSM_GROUPS, 3 * LANES, heads_per_group * width), 1) % LANES
    c = lax.broadcasted_iota(jnp.int32, (SSM_GROUPS, 3 * LANES, heads_per_group * width), 2) // width
    gi = lax.broadcasted_iota(jnp.int32, (SSM_GROUPS, 3 * LANES, heads_per_group * width), 0)
    return (k == gi * heads_per_group + c).astype(BF16)


def _ssd_scan(zx, dt, dtt, conv_w, conv_b, dt_bias, a_log, d_skip, norm_g, batch):
    m = zx.shape[0]
    q = SSD_CHUNK
    gn = SSM_GROUPS * SSM_STATE
    d_inner = (zx.shape[1] - 2 * gn) // 2
    gw = d_inner // SSM_GROUPS
    hpg = gw // SSM_HEAD_DIM
    nt = m // batch // q
    gps = SSD_GROUPS_PER_STEP
    bw, nw = gps * gw, gps * SSM_STATE
    xb0 = d_inner // bw
    bb0 = 2 * d_inner // nw
    cb0 = bb0 + SSM_GROUPS // gps
    cwb0 = d_inner // nw
    cwc0 = cwb0 + SSM_GROUPS // gps

    def pad_row(v):
        return jnp.pad(v, (0, LANES - v.shape[0])).reshape(1, LANES)

    def pad_col(v):
        return jnp.pad(v, (0, LANES - v.shape[0])).reshape(LANES, 1)

    e64 = _expansion_matrices(hpg, SSM_HEAD_DIM)
    e128 = _expansion_matrices(hpg, LANES)
    row_map = lambda b, t, g: (b * nt + t, 0)
    const2 = lambda b, t, g: (0, 0)
    const3 = lambda b, t, g: (0, 0, 0)
    in_specs = [
        pl.BlockSpec((q, bw), lambda b, t, g: (b * nt + t, g)),
        pl.BlockSpec((q, bw), lambda b, t, g: (b * nt + t, xb0 + g)),
        pl.BlockSpec((q, nw), lambda b, t, g: (b * nt + t, bb0 + g)),
        pl.BlockSpec((q, nw), lambda b, t, g: (b * nt + t, cb0 + g)),
        pl.BlockSpec((q, LANES), row_map),
        pl.BlockSpec((LANES, q), lambda b, t, g: (0, b * nt + t)),
        pl.BlockSpec((CONV_WIDTH, bw), lambda b, t, g: (0, g)),
        pl.BlockSpec((CONV_WIDTH, nw), lambda b, t, g: (0, cwb0 + g)),
        pl.BlockSpec((CONV_WIDTH, nw), lambda b, t, g: (0, cwc0 + g)),
        pl.BlockSpec((1, bw), lambda b, t, g: (0, g)),
        pl.BlockSpec((1, nw), lambda b, t, g: (0, cwb0 + g)),
        pl.BlockSpec((1, nw), lambda b, t, g: (0, cwc0 + g)),
        pl.BlockSpec((1, LANES), const2),
        pl.BlockSpec((LANES, 1), const2),
        pl.BlockSpec((1, LANES), const2),
        pl.BlockSpec((LANES, 1), const2),
        pl.BlockSpec((1, LANES), const2),
        pl.BlockSpec((1, bw), lambda b, t, g: (0, g)),
        pl.BlockSpec(e64.shape, const3),
        pl.BlockSpec(e128.shape, const3),
    ]
    scratch_shapes = [
        ((SSM_GROUPS, SSM_STATE, gw), F32),
        ((SSM_GROUPS, SUBLANES, gw), F32),
        ((SSM_GROUPS, SUBLANES, SSM_STATE), F32),
        ((SSM_GROUPS, SUBLANES, SSM_STATE), F32),
        ((gps, SUBLANES + q, gw), F32),
        ((gps, SUBLANES + q, SSM_STATE), F32),
        ((gps, SUBLANES + q, SSM_STATE), F32),
        ((q, 3 * LANES), BF16),
        ((q, 3 * LANES), BF16),
        ((LANES, q), F32),
    ]
    scratch = [pltpu.VMEM(s, d) for s, d in scratch_shapes]
    block_bytes = (3 * _nbytes((q, bw), BF16) + 2 * _nbytes((q, nw), BF16)
                   + 2 * _nbytes((q, LANES), F32) + _nbytes(e64.shape, BF16) + _nbytes(e128.shape, BF16)
                   + 16 * _nbytes((SUBLANES, bw), F32))
    scratch_bytes = sum(_nbytes(s, d) for s, d in scratch_shapes)
    return pl.pallas_call(
        _ssd_body,
        grid=(batch, nt, SSM_GROUPS // gps),
        in_specs=in_specs,
        out_specs=pl.BlockSpec((q, bw), lambda b, t, g: (b * nt + t, g)),
        out_shape=jax.ShapeDtypeStruct((m, d_inner), BF16),
        scratch_shapes=scratch,
        compiler_params=pltpu.CompilerParams(
            dimension_semantics=("arbitrary", "arbitrary", "arbitrary"),
            vmem_limit_bytes=_vmem_limit(block_bytes, scratch_bytes)),
        name="ssd_scan",
    )(zx, zx, zx, zx, dt, dtt, conv_w, conv_w, conv_w, conv_b, conv_b, conv_b,
      pad_row(dt_bias), pad_col(dt_bias), pad_row(a_log), pad_col(a_log), pad_row(d_skip),
      norm_g.reshape(1, d_inner), e64, e128)


def _fold_rows(v, op):
    r, c = v.shape
    v3 = v.reshape(r // SUBLANES, SUBLANES, c)
    return jnp.max(v3, axis=0) if op == "max" else jnp.sum(v3, axis=0)


def _moba_body(q_ref, k_ref, v_ref, o_ref,
               kmean_ref, kaug_ref, vt_ref, qaug_ref, s_ref, m_ref, l_ref, acc_ref):
    qi = pl.program_id(2)
    bs = MOBA_BLOCK
    dh = LANES
    seq = k_ref.shape[0]
    nb = seq // bs
    heads = range(q_ref.shape[1] // dh)
    cbk = ATTN_CHUNK_BLOCKS
    ck = cbk * bs
    c2 = dh ** -0.5 * LOG2_E
    nt = (((1,), (1,)), ((), ()))

    @pl.when(qi == 0)
    def _():
        lane = lax.broadcasted_iota(jnp.int32, (bs, LANES), 1)
        for hh in heads:
            for n in range(nb):
                kb = k_ref[n * bs:(n + 1) * bs, hh * dh:(hh + 1) * dh]
                kmean_ref[hh, n:n + 1, :] = jnp.mean(kb.astype(F32), axis=0, keepdims=True)
                kaug_ref[hh, n * bs:(n + 1) * bs, 0:dh] = kb
                kaug_ref[hh, n * bs:(n + 1) * bs, dh:dh + LANES] = jnp.where(lane == n, 1.0, 0.0).astype(BF16)
                vt_ref[hh, n] = v_ref[n * bs:(n + 1) * bs, hh * dh:(hh + 1) * dh].astype(F32).T.astype(BF16)

    own = pl.multiple_of(qi * bs, bs)
    blk = lax.broadcasted_iota(jnp.int32, (nb, bs), 0)
    key = lax.broadcasted_iota(jnp.int32, (bs, bs), 0)
    qry = lax.broadcasted_iota(jnp.int32, (bs, bs), 1)
    for hh in heads:
        q = q_ref[:, hh * dh:(hh + 1) * dh]
        gate = lax.dot_general(kmean_ref[hh], q.astype(F32), nt,
                               precision=lax.Precision.HIGHEST, preferred_element_type=F32)
        gate = jnp.where(blk < qi, gate, -jnp.inf)
        rank = jnp.zeros((nb, bs), F32)
        for mm in range(nb):
            gm = gate[mm:mm + 1, :]
            beats = jnp.where(gm > gate, 1.0, jnp.where(jnp.logical_and(gm == gate, blk > mm), 1.0, 0.0))
            rank = rank + beats
        chosen = jnp.logical_and(rank < MOBA_TOPK, gate > -jnp.inf)
        bias_t = jnp.where(chosen, 0.0, MASK_BIAS)
        bias_q = jnp.concatenate([bias_t, jnp.zeros((LANES - nb, bs), F32)], axis=0).T
        qaug_ref[hh, :, 0:dh] = q
        qaug_ref[hh, :, dh:dh + LANES] = bias_q.astype(BF16)
        s = lax.dot_general(k_ref[pl.ds(own, bs), hh * dh:(hh + 1) * dh], q, nt,
                            preferred_element_type=F32) * c2
        s = jnp.where(key <= qry, s, MASK_BIAS)
        s_ref[hh, seq:seq + bs, :] = s
        m_ref[hh] = _fold_rows(s, "max")

    for c in range(nb // cbk):
        @pl.when(c * cbk < qi)
        def _(c=c):
            for hh in heads:
                s = lax.dot_general(kaug_ref[hh, c * ck:(c + 1) * ck, :], qaug_ref[hh], nt,
                                    preferred_element_type=F32) * c2
                s_ref[hh, c * ck:(c + 1) * ck, :] = s
                m_ref[hh] = jnp.maximum(m_ref[hh], _fold_rows(s, "max"))

    for hh in heads:
        m = jnp.max(m_ref[hh], axis=0, keepdims=True)
        m_ref[hh] = jnp.broadcast_to(m, (SUBLANES, bs))
        p = jnp.exp2(s_ref[hh, seq:seq + bs, :] - m)
        l_ref[hh] = _fold_rows(p, "sum")
        acc_ref[hh] = jnp.dot(vt_ref[hh, qi], p.astype(BF16), preferred_element_type=F32)

    for c in range(nb // cbk):
        @pl.when(c * cbk < qi)
        def _(c=c):
            for hh in heads:
                p = jnp.exp2(s_ref[hh, c * ck:(c + 1) * ck, :] - m_ref[hh, 0:1, :])
                l_ref[hh] = l_ref[hh] + _fold_rows(p, "sum")
                pb = p.astype(BF16)
                acc = acc_ref[hh]
                for j in range(cbk):
                    acc = acc + jnp.dot(vt_ref[hh, c * cbk + j], pb[j * bs:(j + 1) * bs, :],
                                        preferred_element_type=F32)
                acc_ref[hh] = acc


    for hh in heads:
        l = jnp.sum(l_ref[hh], axis=0, keepdims=True)
        o_ref[:, hh * dh:(hh + 1) * dh] = (acc_ref[hh] / l).T.astype(o_ref.dtype)


def _moba_attention(qkv, batch):
    m = qkv.shape[0]
    seq = m // batch
    dh = qkv.shape[1] // (3 * ATTN_HEADS)
    bs = MOBA_BLOCK
    nq = seq // bs
    hb = ATTN_HEADS_PER_STEP
    ng = ATTN_HEADS // hb
    assert dh == LANES and nq % ATTN_CHUNK_BLOCKS == 0 and nq <= LANES
    block_bytes = 2 * _nbytes((bs, hb * dh), BF16) + 2 * _nbytes((seq, hb * dh), BF16)
    scratch_shapes = [
        ((hb, nq, dh), F32),
        ((hb, seq, dh + LANES), BF16),
        ((hb, nq, dh, bs), BF16),
        ((hb, bs, dh + LANES), BF16),
        ((hb, seq + bs, bs), F32),
        ((hb, SUBLANES, bs), F32),
        ((hb, SUBLANES, bs), F32),
        ((hb, dh, bs), F32),
    ]
    return pl.pallas_call(
        _moba_body,
        grid=(batch, ng, nq),
        in_specs=[pl.BlockSpec((bs, hb * dh), lambda b, h, i: (b * nq + i, h)),
                  pl.BlockSpec((seq, hb * dh), lambda b, h, i: (b, ng + h)),
                  pl.BlockSpec((seq, hb * dh), lambda b, h, i: (b, 2 * ng + h))],
        out_specs=pl.BlockSpec((bs, hb * dh), lambda b, h, i: (b * nq + i, h)),
        out_shape=jax.ShapeDtypeStruct((m, ATTN_HEADS * dh), BF16),
        scratch_shapes=[pltpu.VMEM(s, d) for s, d in scratch_shapes],
        compiler_params=pltpu.CompilerParams(
            dimension_semantics=("arbitrary", "arbitrary", "arbitrary"),
            vmem_limit_bytes=_vmem_limit(block_bytes, sum(_nbytes(s, d) for s, d in scratch_shapes))),
        name="moba_attention",
    )(qkv, qkv, qkv)


def _mamba2_mixer(h, u, gain, w_in_t, layer, conv_w, conv_b, dt_bias, a_log, d_skip, norm_g, w_out,
                  next_gain, batch):
    d_inner = w_out.shape[1]
    n_zx = 2 * d_inner + 2 * SSM_GROUPS * SSM_STATE
    w_dt_t = w_in_t[layer][n_zx:, :]
    if u is None:
        u, dt, dtt = _dt_proj(h, gain, w_dt_t)
    else:
        dt, dtt = _dt_proj(u, None, w_dt_t)
    zx = _fused_matmul([u], [None], [(w_in_t, layer, "f32_t")], [0], [], [], _ep_plain, (BF16,),
                       tm=2048, tn=1024, n_out=n_zx, name="ssm_in_proj")
    y = _ssd_scan(zx, dt, dtt, conv_w, conv_b.reshape(1, -1), dt_bias, a_log, d_skip, norm_g, batch)
    return _fused_matmul([y], [None], [(w_out[layer].astype(BF16), None, "bf16")], [0], [h], [next_gain],
                         _ep_residual_norm, (F32, BF16), tm=512, tn=h.shape[1], single_buffer_weights=True,
                         name="ssm_out_proj")


def _moba_mixer(h, u, w_qkv, w_o, layer, next_gain, batch):
    qkv = _fused_matmul([u], [None], [(w_qkv, layer, "f32")], [0], [], [], _ep_plain, (BF16,),
                        tm=2048, tn=1024, name="attn_qkv")
    o = _moba_attention(qkv, batch)
    d = h.shape[1]
    return _fused_matmul([o], [None], [(w_o[layer].astype(BF16), None, "bf16")], [0], [h], [next_gain],
                         _ep_residual_norm, (F32, BF16), tm=512, tn=d, single_buffer_weights=True,
                         name="attn_out")


def _ffn(h, u, layer, w_gate, w_up, w_down):
    ws = [(w_gate, layer, "f32"), (w_up, layer, "f32")]
    a = _fused_matmul([u], [None], ws, [0, 0], [], [], _ep_swiglu, (BF16,), tm=1024, tn=512, name="ffn_up")
    return _fused_matmul([a], [None], [(w_down, layer, "f32")], [0], [h], [], _ep_residual, (F32,),
                         tm=1024, tn=512, single_buffer_weights=True, name="ffn_down")


def _ple(h, p_l, gain, w_pgate_l, w_pproj_l, out_gain, last):
    d = h.shape[1]
    ws = [(w_pgate_l.astype(BF16), None, "bf16"), (w_pproj_l.astype(BF16), None, "bf16")]
    if last:
        return _fused_matmul([h, p_l], [gain, None], ws, [0, 1], [], [out_gain], _ep_ple_final, (F32,),
                             tm=512, tn=d, single_buffer_weights=True, name="ple_final")
    return _fused_matmul([h, p_l], [gain, None], ws, [0, 1], [], [out_gain], _ep_ple_norm, (F32, BF16),
                         tm=512, tn=d, single_buffer_weights=True, name="ple")


def kernel(x, p, mix_norm_g, ffn_norm_g, ple_norm_g, ssm_w_in, ssm_conv_w, ssm_conv_b, ssm_dt_bias, ssm_a_log, ssm_d, ssm_norm_g, ssm_w_out, attn_w_qkv, attn_w_o, ffn_w_gate, ffn_w_up, ffn_w_down, ple_w_proj, ple_w_gate, final_norm_g):
    batch, seq, d = x.shape
    m = batch * seq
    depth = p.shape[0]
    ssm_w_in_t = jnp.swapaxes(ssm_w_in, 1, 2)
    h = x.reshape(m, d)
    u = None
    for i in range(depth):
        j = i // 2
        if i % 2 == 0:
            h, u = _mamba2_mixer(h, u, mix_norm_g[i], ssm_w_in_t, j, ssm_conv_w[j], ssm_conv_b[j], ssm_dt_bias[j],
                                 ssm_a_log[j], ssm_d[j], ssm_norm_g[j], ssm_w_out, ffn_norm_g[i], batch)
        else:
            h, u = _moba_mixer(h, u, attn_w_qkv, attn_w_o, j, ffn_norm_g[i], batch)
        h = _ffn(h, u, i, ffn_w_gate, ffn_w_up, ffn_w_down)
        last = i == depth - 1
        out = _ple(h, p[i].reshape(m, -1), ple_norm_g[i], ple_w_gate[i], ple_w_proj[i],
                   final_norm_g if last else mix_norm_g[i + 1], last)
        if not last:
            h, u = out
    return out.reshape(batch, seq, d)
```

```python
import functools

import jax
import jax.numpy as jnp
from jax import lax
from jax.experimental import pallas as pl
from jax.experimental.pallas import tpu as pltpu

NORM_EPS = 1e-6

SSM_HEAD_DIM = 64
SSM_GROUPS = 8
SSM_STATE = 128
CONV_WIDTH = 4
SSD_CHUNK = 128

ATTN_HEADS = 16
MOBA_BLOCK = 256
MOBA_TOPK = 3
ATTN_CHUNK_BLOCKS = 4
ATTN_HEADS_PER_STEP = 4
MASK_BIAS = -1e30
LOG2_E = 1.4426950408889634

LANES = 128
SUBLANES = 8
VMEM_BYTES = 64 * 1024 * 1024
COMPILER_SCRATCH_BYTES = 12 * 1024 * 1024

F32 = jnp.float32
BF16 = jnp.bfloat16


def _nbytes(shape, dtype):
    n = 1
    for s in shape:
        n *= s
    return n * jnp.dtype(dtype).itemsize


def _vmem_limit(block_bytes, scratch_bytes):
    need = 2 * block_bytes + scratch_bytes + COMPILER_SCRATCH_BYTES
    return int(min(need, VMEM_BYTES - 4 * 1024 * 1024))


def _sigmoid(v):
    return 0.5 * jnp.tanh(0.5 * v) + 0.5


def _silu(v):
    hv = 0.5 * v
    return hv * jnp.tanh(hv) + hv


def _softplus(v):
    return jnp.maximum(v, 0.0) + jnp.log1p(jnp.exp(-jnp.abs(v)))


def _rmsnorm_rows(x, g):
    ms = jnp.mean(x * x, axis=-1, keepdims=True)
    return x * lax.rsqrt(ms + NORM_EPS) * g


def _mm_body(*refs, normed, w_x, w_kinds, n_extra, n_rows, n_out, epilogue):
    n_x, n_g, n_w = len(normed), sum(normed), len(w_x)
    refs = list(refs)
    x_refs = [refs.pop(0) for _ in range(n_x)]
    g_refs = [refs.pop(0) for _ in range(n_g)]
    w_refs = [refs.pop(0) for _ in range(n_w)]
    e_refs = [refs.pop(0) for _ in range(n_extra)]
    r_refs = [refs.pop(0) for _ in range(n_rows)]
    o_refs = [refs.pop(0) for _ in range(n_out)]
    wb_refs = {k: refs.pop(0) for k, kind in enumerate(w_kinds) if kind != "bf16"}

    @pl.when(pl.program_id(1) == 0)
    def _():
        for k, wb_ref in wb_refs.items():
            wb_ref[...] = w_refs[k][...].astype(BF16)

    xraw = [x_ref[...] for x_ref in x_refs]
    xs = []
    for x, is_normed in zip(xraw, normed):
        if is_normed:
            x = _rmsnorm_rows(x, g_refs.pop(0)[...])
        xs.append(x.astype(BF16))
    accs = []
    for k, (xi, kind) in enumerate(zip(w_x, w_kinds)):
        w = w_refs[k][...] if kind == "bf16" else wb_refs[k][...]
        dims = (((1,), (1,)), ((), ())) if kind == "f32_t" else (((1,), (0,)), ((), ()))
        accs.append(lax.dot_general(xs[xi], w, dims, preferred_element_type=F32))
    outs = epilogue(accs, [e_ref[...] for e_ref in e_refs], [r_ref[...] for r_ref in r_refs], xraw)
    for o_ref, o in zip(o_refs, outs):
        o_ref[...] = o.astype(o_ref.dtype)


def _fused_matmul(xs, gains, ws, w_x, extras, rows, epilogue, out_dtypes, tm, tn, n_out=None,
                  single_buffer_weights=False, name="matmul"):
    m = xs[0].shape[0]
    if n_out is None:
        w0, _, kind0 = ws[0]
        n = w0.shape[1] if kind0 == "f32_t" else w0.shape[-1]
    else:
        n = n_out
    grid = (n // tn, m // tm)
    in_specs, block_bytes, scratch, scratch_bytes = [], 0, [], 0
    for x in xs:
        k = x.shape[1]
        in_specs.append(pl.BlockSpec((tm, k), lambda j, i: (i, 0)))
        block_bytes += _nbytes((tm, k), x.dtype)
    gain_rows = [g.reshape(1, -1) for g in gains if g is not None]
    for g in gain_rows:
        in_specs.append(pl.BlockSpec(g.shape, lambda j, i: (0, 0)))
    w_mode = dict(pipeline_mode=pl.Buffered(1)) if single_buffer_weights else {}
    buffers = 1 if single_buffer_weights else 2
    for w, layer, kind in ws:
        if kind == "bf16":
            k = w.shape[0]
            in_specs.append(pl.BlockSpec((k, tn), lambda j, i: (0, j), **w_mode))
            block_bytes += _nbytes((k, tn), BF16) * buffers // 2
            continue
        if kind == "f32_t":
            k = w.shape[2]
            shape = (tn, k)
            in_specs.append(pl.BlockSpec((None, tn, k), functools.partial(lambda j, i, l: (l, j, 0), l=layer),
                                         **w_mode))
        else:
            k = w.shape[1]
            shape = (k, tn)
            in_specs.append(pl.BlockSpec((None, k, tn), functools.partial(lambda j, i, l: (l, 0, j), l=layer),
                                         **w_mode))
        block_bytes += _nbytes(shape, F32) * buffers // 2
        scratch.append(pltpu.VMEM(shape, BF16))
        scratch_bytes += _nbytes(shape, BF16)
    for e in extras:
        in_specs.append(pl.BlockSpec((tm, tn), lambda j, i: (i, j)))
        block_bytes += _nbytes((tm, tn), e.dtype)
    row_params = [r.reshape(1, -1) for r in rows]
    for r in row_params:
        in_specs.append(pl.BlockSpec((1, tn), lambda j, i: (0, j)))
    for dt in out_dtypes:
        block_bytes += _nbytes((tm, tn), dt)
    body = functools.partial(_mm_body, normed=tuple(g is not None for g in gains), w_x=tuple(w_x),
                             w_kinds=tuple(kind for _, _, kind in ws), n_extra=len(extras),
                             n_rows=len(rows), n_out=len(out_dtypes), epilogue=epilogue)
    outs = pl.pallas_call(
        body,
        grid=grid,
        in_specs=in_specs,
        out_specs=[pl.BlockSpec((tm, tn), lambda j, i: (i, j)) for _ in out_dtypes],
        out_shape=[jax.ShapeDtypeStruct((m, n), dt) for dt in out_dtypes],
        scratch_shapes=scratch,
        compiler_params=pltpu.CompilerParams(
            dimension_semantics=("arbitrary", "arbitrary"),
            vmem_limit_bytes=_vmem_limit(block_bytes, scratch_bytes)),
        name=name,
    )(*xs, *gain_rows, *[w for w, _, _ in ws], *extras, *row_params)
    return outs[0] if len(outs) == 1 else outs


def _ep_plain(accs, extras, rows, xraw):
    return (accs[0],)


def _ep_residual(accs, extras, rows, xraw):
    return (extras[0] + accs[0],)


def _ep_residual_norm(accs, extras, rows, xraw):
    h = extras[0] + accs[0]
    return h, _rmsnorm_rows(h, rows[0])


def _ep_swiglu(accs, extras, rows, xraw):
    return (_silu(accs[0]) * accs[1],)


def _ple_update(accs, xraw):
    return xraw[0] + _sigmoid(accs[0]) * accs[1]


def _ep_ple_norm(accs, extras, rows, xraw):
    h = _ple_update(accs, xraw)
    return h, _rmsnorm_rows(h, rows[0])


def _ep_ple_final(accs, extras, rows, xraw):
    return (_rmsnorm_rows(_ple_update(accs, xraw), rows[0]),)


def _dt_body(*refs, normed):
    nt = (((1,), (1,)), ((), ()))
    if normed:
        x_ref, g_ref, wt_ref, u_ref, dt_ref, dtt_ref = refs
        u = _rmsnorm_rows(x_ref[...], g_ref[...]).astype(BF16)
        u_ref[...] = u
    else:
        x_ref, wt_ref, dt_ref, dtt_ref = refs
        u = x_ref[...]
    wt = wt_ref[...].astype(BF16)
    dt_ref[...] = lax.dot_general(u, wt, nt, preferred_element_type=F32)
    dtt_ref[...] = lax.dot_general(wt, u, nt, preferred_element_type=F32)


def _dt_proj(x, gain, w_dt_t, tm=1024):
    m, k = x.shape
    normed = gain is not None
    wt_pad = jnp.pad(w_dt_t, ((0, LANES - w_dt_t.shape[0]), (0, 0)))
    row_spec = pl.BlockSpec((tm, k), lambda i: (i, 0))
    in_specs = [row_spec] + ([pl.BlockSpec((1, k), lambda i: (0, 0))] if normed else [])
    in_specs.append(pl.BlockSpec((LANES, k), lambda i: (0, 0)))
    out_specs = [pl.BlockSpec((tm, LANES), lambda i: (i, 0)), pl.BlockSpec((LANES, tm), lambda i: (0, i))]
    out_shape = [jax.ShapeDtypeStruct((m, LANES), F32), jax.ShapeDtypeStruct((LANES, m), F32)]
    if normed:
        out_specs.insert(0, row_spec)
        out_shape.insert(0, jax.ShapeDtypeStruct((m, k), BF16))
    blocks = (_nbytes((tm, k), x.dtype) + _nbytes((LANES, k), F32) + 2 * _nbytes((tm, LANES), F32)
              + (_nbytes((tm, k), BF16) if normed else 0))
    args = (x, gain.reshape(1, k), wt_pad) if normed else (x, wt_pad)
    return pl.pallas_call(
        functools.partial(_dt_body, normed=normed),
        grid=(m // tm,),
        in_specs=in_specs,
        out_specs=out_specs,
        out_shape=out_shape,
        compiler_params=pltpu.CompilerParams(
            dimension_semantics=("arbitrary",), vmem_limit_bytes=_vmem_limit(blocks, 0)),
        name="dt_proj",
    )(*args)


def _split3(v):
    hi = v.astype(BF16)
    r1 = v - hi.astype(F32)
    mid = r1.astype(BF16)
    lo = (r1 - mid.astype(F32)).astype(BF16)
    return hi, mid, lo


def _conv_silu(src_ref, halo_ref, buf_ref, w_ref, b_ref, g):
    q = SSD_CHUNK
    cur = src_ref[...].astype(F32)
    buf_ref[0:SUBLANES, :] = halo_ref[g]
    buf_ref[SUBLANES:SUBLANES + q, :] = cur
    halo_ref[g] = cur[q - SUBLANES:q, :]
    acc = b_ref[...] + w_ref[CONV_WIDTH - 1:CONV_WIDTH, :] * cur
    for k in range(CONV_WIDTH - 1):
        off = SUBLANES - (CONV_WIDTH - 1) + k
        acc = acc + w_ref[k:k + 1, :] * buf_ref[off:off + q, :]
    return _silu(acc)


def _ssd_body(z_ref, x_ref, b_ref, c_ref, dt_ref, dtt_ref,
              cwx_ref, cwb_ref, cwc_ref, cbx_ref, cbb_ref, cbc_ref,
              dtb_r_ref, dtb_c_ref, alog_r_ref, alog_c_ref, d_r_ref, ng_ref, e64_ref,
              y_ref,
              state_ref, hx_ref, hb_ref, hc_ref, xbuf_ref, bbuf_ref, cbuf_ref, dts3_ref, cs_ref, cst_ref):
    t = pl.program_id(1)
    q = SSD_CHUNK
    gw = x_ref.shape[1] // SSM_GROUPS
    hpg = gw // SSM_HEAD_DIM
    row = lax.broadcasted_iota(jnp.int32, (q, q), 0)
    col = lax.broadcasted_iota(jnp.int32, (q, q), 1)
    tril = row >= col

    dts = _softplus(dt_ref[...] + dtb_r_ref[...])
    lo_tri = tril.astype(BF16)
    cs = jnp.zeros((q, LANES), F32)
    for part in _split3(dts * (-jnp.exp(alog_r_ref[...]))):
        cs = cs + jnp.dot(lo_tri, part, preferred_element_type=F32)
    dts3_ref[...] = jnp.concatenate(_split3(dts), axis=1)
    cs_ref[...] = cs
    up_tri = (row <= col).astype(BF16)
    cst = jnp.zeros((LANES, q), F32)
    for part in _split3(_softplus(dtt_ref[...] + dtb_c_ref[...]) * (-jnp.exp(alog_c_ref[...]))):
        cst = cst + jnp.dot(part, up_tri, preferred_element_type=F32)
    cst_ref[...] = cst
    cs3 = jnp.concatenate(_split3(cs), axis=1)

    lane = lax.broadcasted_iota(jnp.int32, (q, LANES), 1)
    d3 = jnp.concatenate(_split3(jnp.broadcast_to(d_r_ref[...], (SUBLANES, LANES))), axis=1)
    for gi in range(SSM_GROUPS):
        xcols = slice(gi * gw, (gi + 1) * gw)
        ncols = slice(gi * SSM_STATE, (gi + 1) * SSM_STATE)

        @pl.when(t == 0)
        def _(gi=gi):
            state_ref[gi] = jnp.zeros(state_ref.shape[1:], F32)
            hx_ref[gi] = jnp.zeros(hx_ref.shape[1:], F32)
            hb_ref[gi] = jnp.zeros(hb_ref.shape[1:], F32)
            hc_ref[gi] = jnp.zeros(hc_ref.shape[1:], F32)

        xs = _conv_silu(x_ref.at[:, xcols], hx_ref, xbuf_ref.at[gi], cwx_ref.at[:, xcols],
                        cbx_ref.at[:, xcols], gi)
        bm = _conv_silu(b_ref.at[:, ncols], hb_ref, bbuf_ref.at[gi], cwb_ref.at[:, ncols],
                        cbb_ref.at[:, ncols], gi).astype(BF16)
        cm = _conv_silu(c_ref.at[:, ncols], hc_ref, cbuf_ref.at[gi], cwc_ref.at[:, ncols],
                        cbc_ref.at[:, ncols], gi).astype(BF16)

        e64 = e64_ref[gi]
        dt_x = jnp.dot(dts3_ref[...], e64, preferred_element_type=F32)
        cs_x = jnp.dot(cs3, e64, preferred_element_type=F32)
        d_x = jnp.dot(d3, e64, preferred_element_type=F32)[0:1, :]
        cs_last = cs_x[q - 1:q, :]

        xdt = xs * dt_x
        xdt_b = xdt.astype(BF16)
        cb = lax.dot_general(cm, bm, (((1,), (1,)), ((), ())), preferred_element_type=F32)
        y_pairs = []
        for j in range(hpg // 2):
            xp = xdt_b[:, j * LANES:(j + 1) * LANES]
            outs = []
            for hh in range(2):
                h = 2 * j + hh
                head = gi * hpg + h
                colb = jnp.broadcast_to(cs_ref[:, head:head + 1], (q, LANES))
                rowb = cst_ref[head:head + 1, :]
                decay = jnp.exp(jnp.where(tril, colb - rowb, -jnp.inf))
                outs.append(jnp.dot((cb * decay).astype(BF16), xp, preferred_element_type=F32))
            y_pairs.append(jnp.where(lane < SSM_HEAD_DIM, outs[0], outs[1]))
        y = jnp.concatenate(y_pairs, axis=1)

        st = state_ref[gi]
        y = y + jnp.dot(cm, st.astype(BF16), preferred_element_type=F32) * jnp.exp(cs_x)
        xsc = (xdt * jnp.exp(cs_last - cs_x)).astype(BF16)
        s_new = lax.dot_general(bm, xsc, (((0,), (0,)), ((), ())), preferred_element_type=F32)
        state_ref[gi] = st * jnp.exp(cs_last) + s_new

        y = y + d_x * xs
        yg = y * _silu(z_ref[:, xcols].astype(F32))
        y_ref[:, xcols] = _rmsnorm_rows(yg, ng_ref[:, xcols]).astype(y_ref.dtype)


def _expansion_matrices(heads_per_group, width):
    k = lax.broadcasted_iota(jnp.int32, (SSM_GROUPS, 3 * LANES, heads_per_group * width), 1) % LANES
    c = lax.broadcasted_iota(jnp.int32, (SSM_GROUPS, 3 * LANES, heads_per_group * width), 2) // width
    gi = lax.broadcasted_iota(jnp.int32, (SSM_GROUPS, 3 * LANES, heads_per_group * width), 0)
    return (k == gi * heads_per_group + c).astype(BF16)


def _ssd_scan(zx, dt, dtt, conv_w, conv_b, dt_bias, a_log, d_skip, norm_g, batch):
    m = zx.shape[0]
    q = SSD_CHUNK
    gn = SSM_GROUPS * SSM_STATE
    d_inner = (zx.shape[1] - 2 * gn) // 2
    gw = d_inner // SSM_GROUPS
    hpg = gw // SSM_HEAD_DIM
    nt = m // batch // q
    x_blk, b_blk, c_blk = 1, 2 * d_inner // gn, 2 * d_inner // gn + 1
    cwb_blk, cwc_blk = d_inner // gn, d_inner // gn + 1

    def pad_row(v):
        return jnp.pad(v, (0, LANES - v.shape[0])).reshape(1, LANES)

    def pad_col(v):
        return jnp.pad(v, (0, LANES - v.shape[0])).reshape(LANES, 1)

    e64 = _expansion_matrices(hpg, SSM_HEAD_DIM)
    rows = lambda blk: (lambda b, t: (b * nt + t, blk))
    cols = lambda blk: (lambda b, t: (0, blk))
    in_specs = [
        pl.BlockSpec((q, d_inner), rows(0)),
        pl.BlockSpec((q, d_inner), rows(x_blk)),
        pl.BlockSpec((q, gn), rows(b_blk)),
        pl.BlockSpec((q, gn), rows(c_blk)),
        pl.BlockSpec((q, LANES), rows(0)),
        pl.BlockSpec((LANES, q), lambda b, t: (0, b * nt + t)),
        pl.BlockSpec((CONV_WIDTH, d_inner), cols(0)),
        pl.BlockSpec((CONV_WIDTH, gn), cols(cwb_blk)),
        pl.BlockSpec((CONV_WIDTH, gn), cols(cwc_blk)),
        pl.BlockSpec((1, d_inner), cols(0)),
        pl.BlockSpec((1, gn), cols(cwb_blk)),
        pl.BlockSpec((1, gn), cols(cwc_blk)),
        pl.BlockSpec((1, LANES), cols(0)),
        pl.BlockSpec((LANES, 1), cols(0)),
        pl.BlockSpec((1, LANES), cols(0)),
        pl.BlockSpec((LANES, 1), cols(0)),
        pl.BlockSpec((1, LANES), cols(0)),
        pl.BlockSpec((1, d_inner), cols(0)),
        pl.BlockSpec(e64.shape, lambda b, t: (0, 0, 0)),
    ]
    scratch_shapes = [
        ((SSM_GROUPS, SSM_STATE, gw), F32),
        ((SSM_GROUPS, SUBLANES, gw), F32),
        ((SSM_GROUPS, SUBLANES, SSM_STATE), F32),
        ((SSM_GROUPS, SUBLANES, SSM_STATE), F32),
        ((SSM_GROUPS, SUBLANES + q, gw), F32),
        ((SSM_GROUPS, SUBLANES + q, SSM_STATE), F32),
        ((SSM_GROUPS, SUBLANES + q, SSM_STATE), F32),
        ((q, 3 * LANES), BF16),
        ((q, LANES), F32),
        ((LANES, q), F32),
    ]
    block_bytes = (3 * _nbytes((q, d_inner), BF16) + 2 * _nbytes((q, gn), BF16) + 2 * _nbytes((q, LANES), F32)
                   + _nbytes(e64.shape, BF16) + (CONV_WIDTH + 2) * _nbytes((SUBLANES, d_inner + 2 * gn), F32))
    return pl.pallas_call(
        _ssd_body,
        grid=(batch, nt),
        in_specs=in_specs,
        out_specs=pl.BlockSpec((q, d_inner), rows(0)),
        out_shape=jax.ShapeDtypeStruct((m, d_inner), BF16),
        scratch_shapes=[pltpu.VMEM(s, d) for s, d in scratch_shapes],
        compiler_params=pltpu.CompilerParams(
            dimension_semantics=("arbitrary", "arbitrary"),
            vmem_limit_bytes=_vmem_limit(block_bytes, sum(_nbytes(s, d) for s, d in scratch_shapes))),
        name="ssd_scan",
    )(zx, zx, zx, zx, dt, dtt, conv_w, conv_w, conv_w, conv_b, conv_b, conv_b,
      pad_row(dt_bias), pad_col(dt_bias), pad_row(a_log), pad_col(a_log), pad_row(d_skip),
      norm_g.reshape(1, d_inner), e64)


def _fold_rows(v, op):
    r, c = v.shape
    v3 = v.reshape(r // SUBLANES, SUBLANES, c)
    return jnp.max(v3, axis=0) if op == "max" else jnp.sum(v3, axis=0)


def _moba_body(q_ref, k_ref, v_ref, o_ref,
               kmean_ref, kaug_ref, vt_ref, qaug_ref, s_ref, m_ref, l_ref, acc_ref):
    qi = pl.program_id(2)
    bs = MOBA_BLOCK
    dh = LANES
    seq = k_ref.shape[0]
    nb = seq // bs
    heads = range(q_ref.shape[1] // dh)
    cbk = ATTN_CHUNK_BLOCKS
    ck = cbk * bs
    c2 = dh ** -0.5 * LOG2_E
    nt = (((1,), (1,)), ((), ()))

    @pl.when(qi == 0)
    def _():
        lane = lax.broadcasted_iota(jnp.int32, (bs, LANES), 1)
        for hh in heads:
            for n in range(nb):
                kb = k_ref[n * bs:(n + 1) * bs, hh * dh:(hh + 1) * dh]
                kmean_ref[hh, n:n + 1, :] = jnp.mean(kb.astype(F32), axis=0, keepdims=True)
                kaug_ref[hh, n * bs:(n + 1) * bs, 0:dh] = kb
                kaug_ref[hh, n * bs:(n + 1) * bs, dh:dh + LANES] = jnp.where(lane == n, 1.0, 0.0).astype(BF16)
                vt_ref[hh, n] = v_ref[n * bs:(n + 1) * bs, hh * dh:(hh + 1) * dh].astype(F32).T.astype(BF16)

    own = pl.multiple_of(qi * bs, bs)
    blk = lax.broadcasted_iota(jnp.int32, (nb, bs), 0)
    key = lax.broadcasted_iota(jnp.int32, (bs, bs), 0)
    qry = lax.broadcasted_iota(jnp.int32, (bs, bs), 1)
    for hh in heads:
        q = q_ref[:, hh * dh:(hh + 1) * dh]
        gate = lax.dot_general(kmean_ref[hh], q.astype(F32), nt,
                               precision=lax.Precision.HIGHEST, preferred_element_type=F32)
        gate = jnp.where(blk < qi, gate, -jnp.inf)
        rank = jnp.zeros((nb, bs), F32)
        for mm in range(nb):
            gm = gate[mm:mm + 1, :]
            beats = jnp.where(gm > gate, 1.0, jnp.where(jnp.logical_and(gm == gate, blk > mm), 1.0, 0.0))
            rank = rank + beats
        chosen = jnp.logical_and(rank < MOBA_TOPK, gate > -jnp.inf)
        bias_t = jnp.where(chosen, 0.0, MASK_BIAS)
        bias_q = jnp.concatenate([bias_t, jnp.zeros((LANES - nb, bs), F32)], axis=0).T
        qaug_ref[hh, :, 0:dh] = q
        qaug_ref[hh, :, dh:dh + LANES] = bias_q.astype(BF16)
        s = lax.dot_general(k_ref[pl.ds(own, bs), hh * dh:(hh + 1) * dh], q, nt,
                            preferred_element_type=F32) * c2
        s = jnp.where(key <= qry, s, MASK_BIAS)
        s_ref[hh, seq:seq + bs, :] = s
        m_ref[hh] = _fold_rows(s, "max")

    for c in range(nb // cbk):
        @pl.when(c * cbk < qi)
        def _(c=c):
            for hh in heads:
                s = lax.dot_general(kaug_ref[hh, c * ck:(c + 1) * ck, :], qaug_ref[hh], nt,
                                    preferred_element_type=F32) * c2
                s_ref[hh, c * ck:(c + 1) * ck, :] = s
                m_ref[hh] = jnp.maximum(m_ref[hh], _fold_rows(s, "max"))

    for hh in heads:
        m = jnp.max(m_ref[hh], axis=0, keepdims=True)
        m_ref[hh] = jnp.broadcast_to(m, (SUBLANES, bs))
        p = jnp.exp2(s_ref[hh, seq:seq + bs, :] - m)
        l_ref[hh] = _fold_rows(p, "sum")
        acc_ref[hh] = jnp.dot(vt_ref[hh, qi], p.astype(BF16), preferred_element_type=F32)

    for c in range(nb // cbk):
        @pl.when(c * cbk < qi)
        def _(c=c):
            for hh in heads:
                p = jnp.exp2(s_ref[hh, c * ck:(c + 1) * ck, :] - m_ref[hh, 0:1, :])
                l_ref[hh] = l_ref[hh] + _fold_rows(p, "sum")
                pb = p.astype(BF16)
                acc = acc_ref[hh]
                for j in range(cbk):
                    acc = acc + jnp.dot(vt_ref[hh, c * cbk + j], pb[j * bs:(j + 1) * bs, :],
                                        preferred_element_type=F32)
                acc_ref[hh] = acc


    for hh in heads:
        l = jnp.sum(l_ref[hh], axis=0, keepdims=True)
        o_ref[:, hh * dh:(hh + 1) * dh] = (acc_ref[hh] / l).T.astype(o_ref.dtype)


def _moba_attention(qkv, batch):
    m = qkv.shape[0]
    seq = m // batch
    dh = qkv.shape[1] // (3 * ATTN_HEADS)
    bs = MOBA_BLOCK
    nq = seq // bs
    hb = ATTN_HEADS_PER_STEP
    ng = ATTN_HEADS // hb
    assert dh == LANES and nq % ATTN_CHUNK_BLOCKS == 0 and nq <= LANES
    block_bytes = 2 * _nbytes((bs, hb * dh), BF16) + 2 * _nbytes((seq, hb * dh), BF16)
    scratch_shapes = [
        ((hb, nq, dh), F32),
        ((hb, seq, dh + LANES), BF16),
        ((hb, nq, dh, bs), BF16),
        ((hb, bs, dh + LANES), BF16),
        ((hb, seq + bs, bs), F32),
        ((hb, SUBLANES, bs), F32),
        ((hb, SUBLANES, bs), F32),
        ((hb, dh, bs), F32),
    ]
    return pl.pallas_call(
        _moba_body,
        grid=(batch, ng, nq),
        in_specs=[pl.BlockSpec((bs, hb * dh), lambda b, h, i: (b * nq + i, h)),
                  pl.BlockSpec((seq, hb * dh), lambda b, h, i: (b, ng + h)),
                  pl.BlockSpec((seq, hb * dh), lambda b, h, i: (b, 2 * ng + h))],
        out_specs=pl.BlockSpec((bs, hb * dh), lambda b, h, i: (b * nq + i, h)),
        out_shape=jax.ShapeDtypeStruct((m, ATTN_HEADS * dh), BF16),
        scratch_shapes=[pltpu.VMEM(s, d) for s, d in scratch_shapes],
        compiler_params=pltpu.CompilerParams(
            dimension_semantics=("arbitrary", "arbitrary", "arbitrary"),
            vmem_limit_bytes=_vmem_limit(block_bytes, sum(_nbytes(s, d) for s, d in scratch_shapes))),
        name="moba_attention",
    )(qkv, qkv, qkv)


def _mamba2_mixer(h, u, gain, w_in_t, layer, conv_w, conv_b, dt_bias, a_log, d_skip, norm_g, w_out,
                  next_gain, batch):
    d_inner = w_out.shape[1]
    n_zx = 2 * d_inner + 2 * SSM_GROUPS * SSM_STATE
    w_dt_t = w_in_t[layer][n_zx:, :]
    if u is None:
        u, dt, dtt = _dt_proj(h, gain, w_dt_t)
    else:
        dt, dtt = _dt_proj(u, None, w_dt_t)
    zx = _fused_matmul([u], [None], [(w_in_t, layer, "f32_t")], [0], [], [], _ep_plain, (BF16,),
                       tm=2048, tn=1024, n_out=n_zx, name="ssm_in_proj")
    y = _ssd_scan(zx, dt, dtt, conv_w, conv_b.reshape(1, -1), dt_bias, a_log, d_skip, norm_g, batch)
    return _fused_matmul([y], [None], [(w_out[layer].astype(BF16), None, "bf16")], [0], [h], [next_gain],
                         _ep_residual_norm, (F32, BF16), tm=512, tn=h.shape[1], single_buffer_weights=True,
                         name="ssm_out_proj")


def _moba_mixer(h, u, w_qkv, w_o, layer, next_gain, batch):
    qkv = _fused_matmul([u], [None], [(w_qkv, layer, "f32")], [0], [], [], _ep_plain, (BF16,),
                        tm=2048, tn=1024, name="attn_qkv")
    o = _moba_attention(qkv, batch)
    d = h.shape[1]
    return _fused_matmul([o], [None], [(w_o[layer].astype(BF16), None, "bf16")], [0], [h], [next_gain],
                         _ep_residual_norm, (F32, BF16), tm=512, tn=d, single_buffer_weights=True,
                         name="attn_out")


def _ffn(h, u, layer, w_gate, w_up, w_down):
    ws = [(w_gate, layer, "f32"), (w_up, layer, "f32")]
    a = _fused_matmul([u], [None], ws, [0, 0], [], [], _ep_swiglu, (BF16,), tm=1024, tn=512, name="ffn_up")
    return _fused_matmul([a], [None], [(w_down, layer, "f32")], [0], [h], [], _ep_residual, (F32,),
                         tm=1024, tn=512, single_buffer_weights=True, name="ffn_down")


def _ple(h, p_l, gain, w_pgate_l, w_pproj_l, out_gain, last):
    d = h.shape[1]
    ws = [(w_pgate_l.astype(BF16), None, "bf16"), (w_pproj_l.astype(BF16), None, "bf16")]
    if last:
        return _fused_matmul([h, p_l], [gain, None], ws, [0, 1], [], [out_gain], _ep_ple_final, (F32,),
                             tm=512, tn=d, single_buffer_weights=True, name="ple_final")
    return _fused_matmul([h, p_l], [gain, None], ws, [0, 1], [], [out_gain], _ep_ple_norm, (F32, BF16),
                         tm=512, tn=d, single_buffer_weights=True, name="ple")


def kernel(x, p, mix_norm_g, ffn_norm_g, ple_norm_g, ssm_w_in, ssm_conv_w, ssm_conv_b, ssm_dt_bias, ssm_a_log, ssm_d, ssm_norm_g, ssm_w_out, attn_w_qkv, attn_w_o, ffn_w_gate, ffn_w_up, ffn_w_down, ple_w_proj, ple_w_gate, final_norm_g):
    batch, seq, d = x.shape
    m = batch * seq
    depth = p.shape[0]
    ssm_w_in_t = jnp.swapaxes(ssm_w_in, 1, 2)
    h = x.reshape(m, d)
    u = None
    for i in range(depth):
        j = i // 2
        if i % 2 == 0:
            h, u = _mamba2_mixer(h, u, mix_norm_g[i], ssm_w_in_t, j, ssm_conv_w[j], ssm_conv_b[j], ssm_dt_bias[j],
                                 ssm_a_log[j], ssm_d[j], ssm_norm_g[j], ssm_w_out, ffn_norm_g[i], batch)
        else:
            h, u = _moba_mixer(h, u, attn_w_qkv, attn_w_o, j, ffn_norm_g[i], batch)
        h = _ffn(h, u, i, ffn_w_gate, ffn_w_up, ffn_w_down)
        last = i == depth - 1
        out = _ple(h, p[i].reshape(m, -1), ple_norm_g[i], ple_w_gate[i], ple_w_proj[i],
                   final_norm_g if last else mix_norm_g[i + 1], last)
        if not last:
            h, u = out
    return out.reshape(batch, seq, d)
```

```python
import functools

import jax
import jax.numpy as jnp
from jax import lax
from jax.experimental import pallas as pl
from jax.experimental.pallas import tpu as pltpu

NORM_EPS = 1e-6

SSM_HEAD_DIM = 64
SSM_GROUPS = 8
SSM_STATE = 128
CONV_WIDTH = 4
SSD_CHUNK = 128

ATTN_HEADS = 16
MOBA_BLOCK = 256
MOBA_TOPK = 3
ATTN_CHUNK_BLOCKS = 4
ATTN_HEADS_PER_STEP = 4
MASK_BIAS = -1e30
LOG2_E = 1.4426950408889634

LANES = 128
SUBLANES = 8
VMEM_BYTES = 64 * 1024 * 1024
COMPILER_SCRATCH_BYTES = 12 * 1024 * 1024

F32 = jnp.float32
BF16 = jnp.bfloat16


def _nbytes(shape, dtype):
    n = 1
    for s in shape:
        n *= s
    return n * jnp.dtype(dtype).itemsize


def _vmem_limit(block_bytes, scratch_bytes):
    need = 2 * block_bytes + scratch_bytes + COMPILER_SCRATCH_BYTES
    return int(min(need, VMEM_BYTES - 4 * 1024 * 1024))


def _sigmoid(v):
    return 0.5 * jnp.tanh(0.5 * v) + 0.5


def _silu(v):
    hv = 0.5 * v
    return hv * jnp.tanh(hv) + hv


def _softplus(v):
    return jnp.maximum(v, 0.0) + jnp.log1p(jnp.exp(-jnp.abs(v)))


def _rmsnorm_rows(x, g):
    ms = jnp.mean(x * x, axis=-1, keepdims=True)
    return x * lax.rsqrt(ms + NORM_EPS) * g


def _mm_body(*refs, normed, w_x, w_kinds, n_extra, n_rows, n_out, epilogue):
    n_x, n_g, n_w = len(normed), sum(normed), len(w_x)
    refs = list(refs)
    x_refs = [refs.pop(0) for _ in range(n_x)]
    g_refs = [refs.pop(0) for _ in range(n_g)]
    w_refs = [refs.pop(0) for _ in range(n_w)]
    e_refs = [refs.pop(0) for _ in range(n_extra)]
    r_refs = [refs.pop(0) for _ in range(n_rows)]
    o_refs = [refs.pop(0) for _ in range(n_out)]
    wb_refs = {k: refs.pop(0) for k, kind in enumerate(w_kinds) if kind != "bf16"}

    @pl.when(pl.program_id(1) == 0)
    def _():
        for k, wb_ref in wb_refs.items():
            wb_ref[...] = w_refs[k][...].astype(BF16)

    xraw = [x_ref[...] for x_ref in x_refs]
    xs = []
    for x, is_normed in zip(xraw, normed):
        if is_normed:
            x = _rmsnorm_rows(x, g_refs.pop(0)[...])
        xs.append(x.astype(BF16))
    accs = []
    for k, (xi, kind) in enumerate(zip(w_x, w_kinds)):
        w = w_refs[k][...] if kind == "bf16" else wb_refs[k][...]
        dims = (((1,), (1,)), ((), ())) if kind == "f32_t" else (((1,), (0,)), ((), ()))
        accs.append(lax.dot_general(xs[xi], w, dims, preferred_element_type=F32))
    outs = epilogue(accs, [e_ref[...] for e_ref in e_refs], [r_ref[...] for r_ref in r_refs], xraw)
    for o_ref, o in zip(o_refs, outs):
        o_ref[...] = o.astype(o_ref.dtype)


def _fused_matmul(xs, gains, ws, w_x, extras, rows, epilogue, out_dtypes, tm, tn, n_out=None,
                  single_buffer_weights=False, x_layer=0, name="matmul"):
    m = xs[0].shape[0]
    if n_out is None:
        w0, _, kind0 = ws[0]
        n = w0.shape[1] if kind0 == "f32_t" else w0.shape[-1]
    else:
        n = n_out
    grid = (n // tn, m // tm)
    in_specs, block_bytes, scratch, scratch_bytes = [], 0, [], 0
    for x in xs:
        k = x.shape[-1]
        if x.ndim == 3:
            in_specs.append(pl.BlockSpec((None, tm, k), functools.partial(lambda j, i, l: (l, i, 0), l=x_layer)))
        else:
            in_specs.append(pl.BlockSpec((tm, k), lambda j, i: (i, 0)))
        block_bytes += _nbytes((tm, k), x.dtype)
    gain_rows = [g.reshape(1, -1) for g in gains if g is not None]
    for g in gain_rows:
        in_specs.append(pl.BlockSpec(g.shape, lambda j, i: (0, 0)))
    w_mode = dict(pipeline_mode=pl.Buffered(1)) if single_buffer_weights else {}
    buffers = 1 if single_buffer_weights else 2
    for w, layer, kind in ws:
        if kind == "bf16":
            k = w.shape[0]
            in_specs.append(pl.BlockSpec((k, tn), lambda j, i: (0, j), **w_mode))
            block_bytes += _nbytes((k, tn), BF16) * buffers // 2
            continue
        if kind == "f32_t":
            k = w.shape[2]
            shape = (tn, k)
            in_specs.append(pl.BlockSpec((None, tn, k), functools.partial(lambda j, i, l: (l, j, 0), l=layer),
                                         **w_mode))
        else:
            k = w.shape[1]
            shape = (k, tn)
            in_specs.append(pl.BlockSpec((None, k, tn), functools.partial(lambda j, i, l: (l, 0, j), l=layer),
                                         **w_mode))
        block_bytes += _nbytes(shape, F32) * buffers // 2
        scratch.append(pltpu.VMEM(shape, BF16))
        scratch_bytes += _nbytes(shape, BF16)
    for e in extras:
        in_specs.append(pl.BlockSpec((tm, tn), lambda j, i: (i, j)))
        block_bytes += _nbytes((tm, tn), e.dtype)
    row_params = [r.reshape(1, -1) for r in rows]
    for r in row_params:
        in_specs.append(pl.BlockSpec((1, tn), lambda j, i: (0, j)))
    for dt in out_dtypes:
        block_bytes += _nbytes((tm, tn), dt)
    body = functools.partial(_mm_body, normed=tuple(g is not None for g in gains), w_x=tuple(w_x),
                             w_kinds=tuple(kind for _, _, kind in ws), n_extra=len(extras),
                             n_rows=len(rows), n_out=len(out_dtypes), epilogue=epilogue)
    outs = pl.pallas_call(
        body,
        grid=grid,
        in_specs=in_specs,
        out_specs=[pl.BlockSpec((tm, tn), lambda j, i: (i, j)) for _ in out_dtypes],
        out_shape=[jax.ShapeDtypeStruct((m, n), dt) for dt in out_dtypes],
        scratch_shapes=scratch,
        compiler_params=pltpu.CompilerParams(
            dimension_semantics=("arbitrary", "arbitrary"),
            vmem_limit_bytes=_vmem_limit(block_bytes, scratch_bytes)),
        name=name,
    )(*xs, *gain_rows, *[w for w, _, _ in ws], *extras, *row_params)
    return outs[0] if len(outs) == 1 else outs


def _ep_plain(accs, extras, rows, xraw):
    return (accs[0],)


def _ep_residual(accs, extras, rows, xraw):
    return (extras[0] + accs[0],)


def _ep_residual_norm(accs, extras, rows, xraw):
    h = extras[0] + accs[0]
    return h, _rmsnorm_rows(h, rows[0])


def _ep_swiglu(accs, extras, rows, xraw):
    return (_silu(accs[0]) * accs[1],)


def _ple_update(accs, xraw):
    return xraw[0] + _sigmoid(accs[0]) * accs[1]


def _ep_ple_norm(accs, extras, rows, xraw):
    h = _ple_update(accs, xraw)
    return h, _rmsnorm_rows(h, rows[0])


def _ep_ple_final(accs, extras, rows, xraw):
    return (_rmsnorm_rows(_ple_update(accs, xraw), rows[0]),)


def _dt_body(*refs, normed):
    nt = (((1,), (1,)), ((), ()))
    if normed:
        x_ref, g_ref, wt_ref, u_ref, dt_ref, dtt_ref = refs
        u = _rmsnorm_rows(x_ref[...], g_ref[...]).astype(BF16)
        u_ref[...] = u
    else:
        x_ref, wt_ref, dt_ref, dtt_ref = refs
        u = x_ref[...]
    wt = wt_ref[...].astype(BF16)
    dt_ref[...] = lax.dot_general(u, wt, nt, preferred_element_type=F32)
    dtt_ref[...] = lax.dot_general(wt, u, nt, preferred_element_type=F32)


def _dt_proj(x, gain, w_dt_t, tm=1024):
    m, k = x.shape
    normed = gain is not None
    wt_pad = jnp.pad(w_dt_t, ((0, LANES - w_dt_t.shape[0]), (0, 0)))
    row_spec = pl.BlockSpec((tm, k), lambda i: (i, 0))
    in_specs = [row_spec] + ([pl.BlockSpec((1, k), lambda i: (0, 0))] if normed else [])
    in_specs.append(pl.BlockSpec((LANES, k), lambda i: (0, 0)))
    out_specs = [pl.BlockSpec((tm, LANES), lambda i: (i, 0)), pl.BlockSpec((LANES, tm), lambda i: (0, i))]
    out_shape = [jax.ShapeDtypeStruct((m, LANES), F32), jax.ShapeDtypeStruct((LANES, m), F32)]
    if normed:
        out_specs.insert(0, row_spec)
        out_shape.insert(0, jax.ShapeDtypeStruct((m, k), BF16))
    blocks = (_nbytes((tm, k), x.dtype) + _nbytes((LANES, k), F32) + 2 * _nbytes((tm, LANES), F32)
              + (_nbytes((tm, k), BF16) if normed else 0))
    args = (x, gain.reshape(1, k), wt_pad) if normed else (x, wt_pad)
    return pl.pallas_call(
        functools.partial(_dt_body, normed=normed),
        grid=(m // tm,),
        in_specs=in_specs,
        out_specs=out_specs,
        out_shape=out_shape,
        compiler_params=pltpu.CompilerParams(
            dimension_semantics=("arbitrary",), vmem_limit_bytes=_vmem_limit(blocks, 0)),
        name="dt_proj",
    )(*args)


def _split3(v):
    hi = v.astype(BF16)
    r1 = v - hi.astype(F32)
    mid = r1.astype(BF16)
    lo = (r1 - mid.astype(F32)).astype(BF16)
    return hi, mid, lo


def _conv_silu(src_ref, halo_ref, buf_ref, w_ref, b_ref, g):
    q = SSD_CHUNK
    cur = src_ref[...].astype(F32)
    buf_ref[0:SUBLANES, :] = halo_ref[g]
    buf_ref[SUBLANES:SUBLANES + q, :] = cur
    halo_ref[g] = cur[q - SUBLANES:q, :]
    acc = b_ref[...] + w_ref[CONV_WIDTH - 1:CONV_WIDTH, :] * cur
    for k in range(CONV_WIDTH - 1):
        off = SUBLANES - (CONV_WIDTH - 1) + k
        acc = acc + w_ref[k:k + 1, :] * buf_ref[off:off + q, :]
    return _silu(acc)


def _ssd_body(z_ref, x_ref, b_ref, c_ref, dt_ref, dtt_ref,
              cwx_ref, cwb_ref, cwc_ref, cbx_ref, cbb_ref, cbc_ref,
              dtb_r_ref, dtb_c_ref, alog_r_ref, alog_c_ref, d_r_ref, ng_ref, e64_ref,
              y_ref,
              state_ref, hx_ref, hb_ref, hc_ref, xbuf_ref, bbuf_ref, cbuf_ref, dts3_ref, cs_ref, cst_ref):
    t = pl.program_id(1)
    q = SSD_CHUNK
    gw = x_ref.shape[1] // SSM_GROUPS
    hpg = gw // SSM_HEAD_DIM
    row = lax.broadcasted_iota(jnp.int32, (q, q), 0)
    col = lax.broadcasted_iota(jnp.int32, (q, q), 1)
    tril = row >= col

    dts = _softplus(dt_ref[...] + dtb_r_ref[...])
    lo_tri = tril.astype(BF16)
    cs = jnp.zeros((q, LANES), F32)
    for part in _split3(dts * (-jnp.exp(alog_r_ref[...]))):
        cs = cs + jnp.dot(lo_tri, part, preferred_element_type=F32)
    dts3_ref[...] = jnp.concatenate(_split3(dts), axis=1)
    cs_ref[...] = cs
    up_tri = (row <= col).astype(BF16)
    cst = jnp.zeros((LANES, q), F32)
    for part in _split3(_softplus(dtt_ref[...] + dtb_c_ref[...]) * (-jnp.exp(alog_c_ref[...]))):
        cst = cst + jnp.dot(part, up_tri, preferred_element_type=F32)
    cst_ref[...] = cst
    cs3 = jnp.concatenate(_split3(cs), axis=1)

    lane = lax.broadcasted_iota(jnp.int32, (q, LANES), 1)
    d3 = jnp.concatenate(_split3(jnp.broadcast_to(d_r_ref[...], (SUBLANES, LANES))), axis=1)
    for gi in range(SSM_GROUPS):
        xcols = slice(gi * gw, (gi + 1) * gw)
        ncols = slice(gi * SSM_STATE, (gi + 1) * SSM_STATE)

        @pl.when(t == 0)
        def _(gi=gi):
            state_ref[gi] = jnp.zeros(state_ref.shape[1:], F32)
            hx_ref[gi] = jnp.zeros(hx_ref.shape[1:], F32)
            hb_ref[gi] = jnp.zeros(hb_ref.shape[1:], F32)
            hc_ref[gi] = jnp.zeros(hc_ref.shape[1:], F32)

        xs = _conv_silu(x_ref.at[:, xcols], hx_ref, xbuf_ref.at[gi], cwx_ref.at[:, xcols],
                        cbx_ref.at[:, xcols], gi)
        bm = _conv_silu(b_ref.at[:, ncols], hb_ref, bbuf_ref.at[gi], cwb_ref.at[:, ncols],
                        cbb_ref.at[:, ncols], gi).astype(BF16)
        cm = _conv_silu(c_ref.at[:, ncols], hc_ref, cbuf_ref.at[gi], cwc_ref.at[:, ncols],
                        cbc_ref.at[:, ncols], gi).astype(BF16)

        e64 = e64_ref[gi]
        dt_x = jnp.dot(dts3_ref[...], e64, preferred_element_type=F32)
        cs_x = jnp.dot(cs3, e64, preferred_element_type=F32)
        d_x = jnp.dot(d3, e64, preferred_element_type=F32)[0:1, :]
        cs_last = cs_x[q - 1:q, :]

        xdt = xs * dt_x
        xdt_b = xdt.astype(BF16)
        cb = lax.dot_general(cm, bm, (((1,), (1,)), ((), ())), preferred_element_type=F32)
        y_pairs = []
        for j in range(hpg // 2):
            xp = xdt_b[:, j * LANES:(j + 1) * LANES]
            outs = []
            for hh in range(2):
                h = 2 * j + hh
                head = gi * hpg + h
                colb = jnp.broadcast_to(cs_ref[:, head:head + 1], (q, LANES))
                rowb = cst_ref[head:head + 1, :]
                decay = jnp.exp(jnp.where(tril, colb - rowb, -jnp.inf))
                outs.append(jnp.dot((cb * decay).astype(BF16), xp, preferred_element_type=F32))
            y_pairs.append(jnp.where(lane < SSM_HEAD_DIM, outs[0], outs[1]))
        y = jnp.concatenate(y_pairs, axis=1)

        st = state_ref[gi]
        y = y + jnp.dot(cm, st.astype(BF16), preferred_element_type=F32) * jnp.exp(cs_x)
        xsc = (xdt * jnp.exp(cs_last - cs_x)).astype(BF16)
        s_new = lax.dot_general(bm, xsc, (((0,), (0,)), ((), ())), preferred_element_type=F32)
        state_ref[gi] = st * jnp.exp(cs_last) + s_new

        y = y + d_x * xs
        yg = y * _silu(z_ref[:, xcols].astype(F32))
        y_ref[:, xcols] = _rmsnorm_rows(yg, ng_ref[:, xcols]).astype(y_ref.dtype)


def _expansion_matrices(heads_per_group, width):
    k = lax.broadcasted_iota(jnp.int32, (SSM_GROUPS, 3 * LANES, heads_per_group * width), 1) % LANES
    c = lax.broadcasted_iota(jnp.int32, (SSM_GROUPS, 3 * LANES, heads_per_group * width), 2) // width
    gi = lax.broadcasted_iota(jnp.int32, (SSM_GROUPS, 3 * LANES, heads_per_group * width), 0)
    return (k == gi * heads_per_group + c).astype(BF16)


def _ssd_scan(zx, dt, dtt, conv_w, conv_b, dt_bias, a_log, d_skip, norm_g, batch):
    m = zx.shape[0]
    q = SSD_CHUNK
    gn = SSM_GROUPS * SSM_STATE
    d_inner = (zx.shape[1] - 2 * gn) // 2
    gw = d_inner // SSM_GROUPS
    hpg = gw // SSM_HEAD_DIM
    nt = m // batch // q
    x_blk, b_blk, c_blk = 1, 2 * d_inner // gn, 2 * d_inner // gn + 1
    cwb_blk, cwc_blk = d_inner // gn, d_inner // gn + 1

    def pad_row(v):
        return jnp.pad(v, (0, LANES - v.shape[0])).reshape(1, LANES)

    def pad_col(v):
        return jnp.pad(v, (0, LANES - v.shape[0])).reshape(LANES, 1)

    e64 = _expansion_matrices(hpg, SSM_HEAD_DIM)
    rows = lambda blk: (lambda b, t: (b * nt + t, blk))
    cols = lambda blk: (lambda b, t: (0, blk))
    in_specs = [
        pl.BlockSpec((q, d_inner), rows(0)),
        pl.BlockSpec((q, d_inner), rows(x_blk)),
        pl.BlockSpec((q, gn), rows(b_blk)),
        pl.BlockSpec((q, gn), rows(c_blk)),
        pl.BlockSpec((q, LANES), rows(0)),
        pl.BlockSpec((LANES, q), lambda b, t: (0, b * nt + t)),
        pl.BlockSpec((CONV_WIDTH, d_inner), cols(0)),
        pl.BlockSpec((CONV_WIDTH, gn), cols(cwb_blk)),
        pl.BlockSpec((CONV_WIDTH, gn), cols(cwc_blk)),
        pl.BlockSpec((1, d_inner), cols(0)),
        pl.BlockSpec((1, gn), cols(cwb_blk)),
        pl.BlockSpec((1, gn), cols(cwc_blk)),
        pl.BlockSpec((1, LANES), cols(0)),
        pl.BlockSpec((LANES, 1), cols(0)),
        pl.BlockSpec((1, LANES), cols(0)),
        pl.BlockSpec((LANES, 1), cols(0)),
        pl.BlockSpec((1, LANES), cols(0)),
        pl.BlockSpec((1, d_inner), cols(0)),
        pl.BlockSpec(e64.shape, lambda b, t: (0, 0, 0)),
    ]
    scratch_shapes = [
        ((SSM_GROUPS, SSM_STATE, gw), F32),
        ((SSM_GROUPS, SUBLANES, gw), F32),
        ((SSM_GROUPS, SUBLANES, SSM_STATE), F32),
        ((SSM_GROUPS, SUBLANES, SSM_STATE), F32),
        ((SSM_GROUPS, SUBLANES + q, gw), F32),
        ((SSM_GROUPS, SUBLANES + q, SSM_STATE), F32),
        ((SSM_GROUPS, SUBLANES + q, SSM_STATE), F32),
        ((q, 3 * LANES), BF16),
        ((q, LANES), F32),
        ((LANES, q), F32),
    ]
    block_bytes = (3 * _nbytes((q, d_inner), BF16) + 2 * _nbytes((q, gn), BF16) + 2 * _nbytes((q, LANES), F32)
                   + _nbytes(e64.shape, BF16) + (CONV_WIDTH + 2) * _nbytes((SUBLANES, d_inner + 2 * gn), F32))
    return pl.pallas_call(
        _ssd_body,
        grid=(batch, nt),
        in_specs=in_specs,
        out_specs=pl.BlockSpec((q, d_inner), rows(0)),
        out_shape=jax.ShapeDtypeStruct((m, d_inner), BF16),
        scratch_shapes=[pltpu.VMEM(s, d) for s, d in scratch_shapes],
        compiler_params=pltpu.CompilerParams(
            dimension_semantics=("arbitrary", "arbitrary"),
            vmem_limit_bytes=_vmem_limit(block_bytes, sum(_nbytes(s, d) for s, d in scratch_shapes))),
        name="ssd_scan",
    )(zx, zx, zx, zx, dt, dtt, conv_w, conv_w, conv_w, conv_b, conv_b, conv_b,
      pad_row(dt_bias), pad_col(dt_bias), pad_row(a_log), pad_col(a_log), pad_row(d_skip),
      norm_g.reshape(1, d_inner), e64)


def _fold_rows(v, op):
    r, c = v.shape
    v3 = v.reshape(r // SUBLANES, SUBLANES, c)
    return jnp.max(v3, axis=0) if op == "max" else jnp.sum(v3, axis=0)


def _moba_body(q_ref, k_ref, v_ref, o_ref,
               kmean_ref, kaug_ref, vt_ref, qaug_ref, s_ref, m_ref, l_ref, acc_ref):
    qi = pl.program_id(2)
    bs = MOBA_BLOCK
    dh = LANES
    seq = k_ref.shape[0]
    nb = seq // bs
    heads = range(q_ref.shape[1] // dh)
    cbk = ATTN_CHUNK_BLOCKS
    ck = cbk * bs
    c2 = dh ** -0.5 * LOG2_E
    nt = (((1,), (1,)), ((), ()))

    @pl.when(qi == 0)
    def _():
        lane = lax.broadcasted_iota(jnp.int32, (bs, LANES), 1)
        for hh in heads:
            for n in range(nb):
                kb = k_ref[n * bs:(n + 1) * bs, hh * dh:(hh + 1) * dh]
                kmean_ref[hh, n:n + 1, :] = jnp.mean(kb.astype(F32), axis=0, keepdims=True)
                kaug_ref[hh, n * bs:(n + 1) * bs, 0:dh] = kb
                kaug_ref[hh, n * bs:(n + 1) * bs, dh:dh + LANES] = jnp.where(lane == n, 1.0, 0.0).astype(BF16)
                vt_ref[hh, n] = v_ref[n * bs:(n + 1) * bs, hh * dh:(hh + 1) * dh].astype(F32).T.astype(BF16)

    blk = lax.broadcasted_iota(jnp.int32, (nb, bs), 0)
    for hh in heads:
        q = q_ref[:, hh * dh:(hh + 1) * dh]
        gate = lax.dot_general(kmean_ref[hh], q.astype(F32), nt,
                               precision=lax.Precision.HIGHEST, preferred_element_type=F32)
        gate = jnp.where(blk < qi, gate, -jnp.inf)
        rank = jnp.zeros((nb, bs), F32)
        for mm in range(nb):
            gm = gate[mm:mm + 1, :]
            beats = jnp.where(gm > gate, 1.0, jnp.where(jnp.logical_and(gm == gate, blk > mm), 1.0, 0.0))
            rank = rank + beats
        chosen = jnp.logical_or(jnp.logical_and(rank < MOBA_TOPK, gate > -jnp.inf), blk == qi)
        bias_t = jnp.where(chosen, 0.0, MASK_BIAS)
        bias_q = jnp.concatenate([bias_t, jnp.zeros((LANES - nb, bs), F32)], axis=0).T
        qaug_ref[hh, :, 0:dh] = q
        qaug_ref[hh, :, dh:dh + LANES] = bias_q.astype(BF16)

    key = lax.broadcasted_iota(jnp.int32, (ck, bs), 0)
    qry = lax.broadcasted_iota(jnp.int32, (ck, bs), 1)

    def score_chunk(c, first):
        causal = key + (c * ck - qi * bs) <= qry
        for hh in heads:
            s = lax.dot_general(kaug_ref[hh, c * ck:(c + 1) * ck, :], qaug_ref[hh], nt,
                                preferred_element_type=F32) * c2
            s = jnp.where(causal, s, MASK_BIAS)
            s_ref[hh, c * ck:(c + 1) * ck, :] = s
            fold = _fold_rows(s, "max")
            m_ref[hh] = fold if first else jnp.maximum(m_ref[hh], fold)

    score_chunk(0, True)
    for c in range(1, nb // cbk):
        pl.when(c * cbk <= qi)(functools.partial(score_chunk, c, False))

    for hh in heads:
        m = jnp.max(m_ref[hh], axis=0, keepdims=True)
        m_ref[hh] = jnp.broadcast_to(m, (SUBLANES, bs))

    def accumulate_chunk(c, first):
        for hh in heads:
            p = jnp.exp2(s_ref[hh, c * ck:(c + 1) * ck, :] - m_ref[hh, 0:1, :])
            fold = _fold_rows(p, "sum")
            l_ref[hh] = fold if first else l_ref[hh] + fold
            pb = p.astype(BF16)
            acc = None if first else acc_ref[hh]
            for j in range(cbk):
                pv = jnp.dot(vt_ref[hh, c * cbk + j], pb[j * bs:(j + 1) * bs, :], preferred_element_type=F32)
                acc = pv if acc is None else acc + pv
            acc_ref[hh] = acc

    accumulate_chunk(0, True)
    for c in range(1, nb // cbk):
        pl.when(c * cbk <= qi)(functools.partial(accumulate_chunk, c, False))


    for hh in heads:
        l = jnp.sum(l_ref[hh], axis=0, keepdims=True)
        o_ref[:, hh * dh:(hh + 1) * dh] = (acc_ref[hh] / l).T.astype(o_ref.dtype)


def _moba_attention(qkv, batch):
    m = qkv.shape[0]
    seq = m // batch
    dh = qkv.shape[1] // (3 * ATTN_HEADS)
    bs = MOBA_BLOCK
    nq = seq // bs
    hb = ATTN_HEADS_PER_STEP
    ng = ATTN_HEADS // hb
    assert dh == LANES and nq % ATTN_CHUNK_BLOCKS == 0 and nq <= LANES
    block_bytes = 2 * _nbytes((bs, hb * dh), BF16) + 2 * _nbytes((seq, hb * dh), BF16)
    scratch_shapes = [
        ((hb, nq, dh), F32),
        ((hb, seq, dh + LANES), BF16),
        ((hb, nq, dh, bs), BF16),
        ((hb, bs, dh + LANES), BF16),
        ((hb, seq, bs), F32),
        ((hb, SUBLANES, bs), F32),
        ((hb, SUBLANES, bs), F32),
        ((hb, dh, bs), F32),
    ]
    return pl.pallas_call(
        _moba_body,
        grid=(batch, ng, nq),
        in_specs=[pl.BlockSpec((bs, hb * dh), lambda b, h, i: (b * nq + i, h)),
                  pl.BlockSpec((seq, hb * dh), lambda b, h, i: (b, ng + h)),
                  pl.BlockSpec((seq, hb * dh), lambda b, h, i: (b, 2 * ng + h))],
        out_specs=pl.BlockSpec((bs, hb * dh), lambda b, h, i: (b * nq + i, h)),
        out_shape=jax.ShapeDtypeStruct((m, ATTN_HEADS * dh), BF16),
        scratch_shapes=[pltpu.VMEM(s, d) for s, d in scratch_shapes],
        compiler_params=pltpu.CompilerParams(
            dimension_semantics=("arbitrary", "arbitrary", "arbitrary"),
            vmem_limit_bytes=_vmem_limit(block_bytes, sum(_nbytes(s, d) for s, d in scratch_shapes))),
        name="moba_attention",
    )(qkv, qkv, qkv)


def _mamba2_mixer(h, u, gain, w_in_t, layer, conv_w, conv_b, dt_bias, a_log, d_skip, norm_g, w_out,
                  next_gain, batch):
    d_inner = w_out.shape[1]
    n_zx = 2 * d_inner + 2 * SSM_GROUPS * SSM_STATE
    w_dt_t = w_in_t[layer][n_zx:, :]
    if u is None:
        u, dt, dtt = _dt_proj(h, gain, w_dt_t)
    else:
        dt, dtt = _dt_proj(u, None, w_dt_t)
    zx = _fused_matmul([u], [None], [(w_in_t, layer, "f32_t")], [0], [], [], _ep_plain, (BF16,),
                       tm=2048, tn=1024, n_out=n_zx, name="ssm_in_proj")
    y = _ssd_scan(zx, dt, dtt, conv_w, conv_b.reshape(1, -1), dt_bias, a_log, d_skip, norm_g, batch)
    return _fused_matmul([y], [None], [(w_out[layer].astype(BF16), None, "bf16")], [0], [h], [next_gain],
                         _ep_residual_norm, (F32, BF16), tm=512, tn=h.shape[1], single_buffer_weights=True,
                         name="ssm_out_proj")


def _moba_mixer(h, u, w_qkv, w_o, layer, next_gain, batch):
    qkv = _fused_matmul([u], [None], [(w_qkv, layer, "f32")], [0], [], [], _ep_plain, (BF16,),
                        tm=2048, tn=1024, name="attn_qkv")
    o = _moba_attention(qkv, batch)
    d = h.shape[1]
    return _fused_matmul([o], [None], [(w_o[layer].astype(BF16), None, "bf16")], [0], [h], [next_gain],
                         _ep_residual_norm, (F32, BF16), tm=512, tn=d, single_buffer_weights=True,
                         name="attn_out")


def _ffn(h, u, layer, w_gate, w_up, w_down):
    ws = [(w_gate, layer, "f32"), (w_up, layer, "f32")]
    a = _fused_matmul([u], [None], ws, [0, 0], [], [], _ep_swiglu, (BF16,), tm=1024, tn=512, name="ffn_up")
    return _fused_matmul([a], [None], [(w_down, layer, "f32")], [0], [h], [], _ep_residual, (F32,),
                         tm=1024, tn=512, single_buffer_weights=True, name="ffn_down")


def _ple(h, p, layer, gain, w_pgate_l, w_pproj_l, out_gain, last):
    d = h.shape[1]
    ws = [(w_pgate_l.astype(BF16), None, "bf16"), (w_pproj_l.astype(BF16), None, "bf16")]
    if last:
        return _fused_matmul([h, p], [gain, None], ws, [0, 1], [], [out_gain], _ep_ple_final, (F32,),
                             tm=512, tn=d, single_buffer_weights=True, x_layer=layer, name="ple_final")
    return _fused_matmul([h, p], [gain, None], ws, [0, 1], [], [out_gain], _ep_ple_norm, (F32, BF16),
                         tm=512, tn=d, single_buffer_weights=True, x_layer=layer, name="ple")


def kernel(x, p, mix_norm_g, ffn_norm_g, ple_norm_g, ssm_w_in, ssm_conv_w, ssm_conv_b, ssm_dt_bias, ssm_a_log, ssm_d, ssm_norm_g, ssm_w_out, attn_w_qkv, attn_w_o, ffn_w_gate, ffn_w_up, ffn_w_down, ple_w_proj, ple_w_gate, final_norm_g):
    batch, seq, d = x.shape
    m = batch * seq
    depth = p.shape[0]
    ssm_w_in_t = jnp.swapaxes(ssm_w_in, 1, 2)
    h = x.reshape(m, d)
    u = None
    for i in range(depth):
        j = i // 2
        if i % 2 == 0:
            h, u = _mamba2_mixer(h, u, mix_norm_g[i], ssm_w_in_t, j, ssm_conv_w[j], ssm_conv_b[j], ssm_dt_bias[j],
                                 ssm_a_log[j], ssm_d[j], ssm_norm_g[j], ssm_w_out, ffn_norm_g[i], batch)
        else:
            h, u = _moba_mixer(h, u, attn_w_qkv, attn_w_o, j, ffn_norm_g[i], batch)
        h = _ffn(h, u, i, ffn_w_gate, ffn_w_up, ffn_w_down)
        last = i == depth - 1
        out = _ple(h, p.reshape(depth, m, -1), i, ple_norm_g[i], ple_w_gate[i], ple_w_proj[i],
                   final_norm_g if last else mix_norm_g[i + 1], last)
        if not last:
            h, u = out
    return out.reshape(batch, seq, d)
```

```python
import functools

import jax
import jax.numpy as jnp
from jax import lax
from jax.experimental import pallas as pl
from jax.experimental.pallas import tpu as pltpu

NORM_EPS = 1e-6

SSM_HEAD_DIM = 64
SSM_GROUPS = 8
SSM_STATE = 128
CONV_WIDTH = 4
SSD_CHUNK = 128

ATTN_HEADS = 16
MOBA_BLOCK = 256
MOBA_TOPK = 3
ATTN_CHUNK_BLOCKS = 4
ATTN_HEADS_PER_STEP = 4
MASK_BIAS = -1e30
LOG2_E = 1.4426950408889634

LANES = 128
SUBLANES = 8
VMEM_BYTES = 64 * 1024 * 1024
COMPILER_SCRATCH_BYTES = 12 * 1024 * 1024

F32 = jnp.float32
BF16 = jnp.bfloat16


def _nbytes(shape, dtype):
    n = 1
    for s in shape:
        n *= s
    return n * jnp.dtype(dtype).itemsize


def _vmem_limit(block_bytes, scratch_bytes):
    need = 2 * block_bytes + scratch_bytes + COMPILER_SCRATCH_BYTES
    return int(min(need, VMEM_BYTES - 4 * 1024 * 1024))


def _sigmoid(v):
    return 0.5 * jnp.tanh(0.5 * v) + 0.5


def _silu(v):
    hv = 0.5 * v
    return hv * jnp.tanh(hv) + hv


def _softplus(v):
    return jnp.maximum(v, 0.0) + jnp.log1p(jnp.exp(-jnp.abs(v)))


def _rmsnorm_rows(x, g):
    ms = jnp.mean(x * x, axis=-1, keepdims=True)
    return x * lax.rsqrt(ms + NORM_EPS) * g


def _mm_body(*refs, normed, w_x, w_kinds, n_extra, n_rows, n_out, epilogue):
    n_x, n_g, n_w = len(normed), sum(normed), len(w_x)
    refs = list(refs)
    x_refs = [refs.pop(0) for _ in range(n_x)]
    g_refs = [refs.pop(0) for _ in range(n_g)]
    w_refs = [refs.pop(0) for _ in range(n_w)]
    e_refs = [refs.pop(0) for _ in range(n_extra)]
    r_refs = [refs.pop(0) for _ in range(n_rows)]
    o_refs = [refs.pop(0) for _ in range(n_out)]
    wb_refs = {k: refs.pop(0) for k, kind in enumerate(w_kinds) if kind != "bf16"}

    @pl.when(pl.program_id(1) == 0)
    def _():
        for k, wb_ref in wb_refs.items():
            wb_ref[...] = w_refs[k][...].astype(BF16)

    xraw = [x_ref[...] for x_ref in x_refs]
    xs = []
    for x, is_normed in zip(xraw, normed):
        if is_normed:
            x = _rmsnorm_rows(x, g_refs.pop(0)[...])
        xs.append(x.astype(BF16))
    accs = []
    for k, (xi, kind) in enumerate(zip(w_x, w_kinds)):
        w = w_refs[k][...] if kind == "bf16" else wb_refs[k][...]
        dims = (((1,), (1,)), ((), ())) if kind == "f32_t" else (((1,), (0,)), ((), ()))
        accs.append(lax.dot_general(xs[xi], w, dims, preferred_element_type=F32))
    outs = epilogue(accs, [e_ref[...] for e_ref in e_refs], [r_ref[...] for r_ref in r_refs], xraw)
    for o_ref, o in zip(o_refs, outs):
        o_ref[...] = o.astype(o_ref.dtype)


def _fused_matmul(xs, gains, ws, w_x, extras, rows, epilogue, out_dtypes, tm, tn, n_out=None,
                  single_buffer_weights=False, x_layer=0, name="matmul"):
    m = xs[0].shape[0]
    if n_out is None:
        w0, _, kind0 = ws[0]
        n = w0.shape[1] if kind0 == "f32_t" else w0.shape[-1]
    else:
        n = n_out
    grid = (n // tn, m // tm)
    in_specs, block_bytes, scratch, scratch_bytes = [], 0, [], 0
    for x in xs:
        k = x.shape[-1]
        if x.ndim == 3:
            in_specs.append(pl.BlockSpec((None, tm, k), functools.partial(lambda j, i, l: (l, i, 0), l=x_layer)))
        else:
            in_specs.append(pl.BlockSpec((tm, k), lambda j, i: (i, 0)))
        block_bytes += _nbytes((tm, k), x.dtype)
    gain_rows = [g.reshape(1, -1) for g in gains if g is not None]
    for g in gain_rows:
        in_specs.append(pl.BlockSpec(g.shape, lambda j, i: (0, 0)))
    w_mode = dict(pipeline_mode=pl.Buffered(1)) if single_buffer_weights else {}
    buffers = 1 if single_buffer_weights else 2
    for w, layer, kind in ws:
        if kind == "bf16":
            k = w.shape[0]
            in_specs.append(pl.BlockSpec((k, tn), lambda j, i: (0, j), **w_mode))
            block_bytes += _nbytes((k, tn), BF16) * buffers // 2
            continue
        if kind == "f32_t":
            k = w.shape[2]
            shape = (tn, k)
            in_specs.append(pl.BlockSpec((None, tn, k), functools.partial(lambda j, i, l: (l, j, 0), l=layer),
                                         **w_mode))
        else:
            k = w.shape[1]
            shape = (k, tn)
            in_specs.append(pl.BlockSpec((None, k, tn), functools.partial(lambda j, i, l: (l, 0, j), l=layer),
                                         **w_mode))
        block_bytes += _nbytes(shape, F32) * buffers // 2
        scratch.append(pltpu.VMEM(shape, BF16))
        scratch_bytes += _nbytes(shape, BF16)
    for e in extras:
        in_specs.append(pl.BlockSpec((tm, tn), lambda j, i: (i, j)))
        block_bytes += _nbytes((tm, tn), e.dtype)
    row_params = [r.reshape(1, -1) for r in rows]
    for r in row_params:
        in_specs.append(pl.BlockSpec((1, tn), lambda j, i: (0, j)))
    for dt in out_dtypes:
        block_bytes += _nbytes((tm, tn), dt)
    body = functools.partial(_mm_body, normed=tuple(g is not None for g in gains), w_x=tuple(w_x),
                             w_kinds=tuple(kind for _, _, kind in ws), n_extra=len(extras),
                             n_rows=len(rows), n_out=len(out_dtypes), epilogue=epilogue)
    outs = pl.pallas_call(
        body,
        grid=grid,
        in_specs=in_specs,
        out_specs=[pl.BlockSpec((tm, tn), lambda j, i: (i, j)) for _ in out_dtypes],
        out_shape=[jax.ShapeDtypeStruct((m, n), dt) for dt in out_dtypes],
        scratch_shapes=scratch,
        compiler_params=pltpu.CompilerParams(
            dimension_semantics=("arbitrary", "arbitrary"),
            vmem_limit_bytes=_vmem_limit(block_bytes, scratch_bytes)),
        name=name,
    )(*xs, *gain_rows, *[w for w, _, _ in ws], *extras, *row_params)
    return outs[0] if len(outs) == 1 else outs


def _ep_plain(accs, extras, rows, xraw):
    return (accs[0],)


def _ep_residual(accs, extras, rows, xraw):
    return (extras[0] + accs[0],)


def _ep_residual_norm(accs, extras, rows, xraw):
    h = extras[0] + accs[0]
    return h, _rmsnorm_rows(h, rows[0])


def _ep_swiglu(accs, extras, rows, xraw):
    return (_silu(accs[0]) * accs[1],)


def _ple_update(accs, xraw):
    return xraw[0] + _sigmoid(accs[0]) * accs[1]


def _ep_ple_norm(accs, extras, rows, xraw):
    h = _ple_update(accs, xraw)
    return h, _rmsnorm_rows(h, rows[0])


def _ep_ple_final(accs, extras, rows, xraw):
    return (_rmsnorm_rows(_ple_update(accs, xraw), rows[0]),)


def _dt_body(*refs, normed):
    nt = (((1,), (1,)), ((), ()))
    if normed:
        x_ref, g_ref, wt_ref, u_ref, dt_ref, dtt_ref = refs
        u = _rmsnorm_rows(x_ref[...], g_ref[...]).astype(BF16)
        u_ref[...] = u
    else:
        x_ref, wt_ref, dt_ref, dtt_ref = refs
        u = x_ref[...]
    wt = wt_ref[...].astype(BF16)
    dt_ref[...] = lax.dot_general(u, wt, nt, preferred_element_type=F32)
    dtt_ref[...] = lax.dot_general(wt, u, nt, preferred_element_type=F32)


def _dt_proj(x, gain, w_dt_t, tm=1024):
    m, k = x.shape
    normed = gain is not None
    wt_pad = jnp.pad(w_dt_t, ((0, LANES - w_dt_t.shape[0]), (0, 0)))
    row_spec = pl.BlockSpec((tm, k), lambda i: (i, 0))
    in_specs = [row_spec] + ([pl.BlockSpec((1, k), lambda i: (0, 0))] if normed else [])
    in_specs.append(pl.BlockSpec((LANES, k), lambda i: (0, 0)))
    out_specs = [pl.BlockSpec((tm, LANES), lambda i: (i, 0)), pl.BlockSpec((LANES, tm), lambda i: (0, i))]
    out_shape = [jax.ShapeDtypeStruct((m, LANES), F32), jax.ShapeDtypeStruct((LANES, m), F32)]
    if normed:
        out_specs.insert(0, row_spec)
        out_shape.insert(0, jax.ShapeDtypeStruct((m, k), BF16))
    blocks = (_nbytes((tm, k), x.dtype) + _nbytes((LANES, k), F32) + 2 * _nbytes((tm, LANES), F32)
              + (_nbytes((tm, k), BF16) if normed else 0))
    args = (x, gain.reshape(1, k), wt_pad) if normed else (x, wt_pad)
    return pl.pallas_call(
        functools.partial(_dt_body, normed=normed),
        grid=(m // tm,),
        in_specs=in_specs,
        out_specs=out_specs,
        out_shape=out_shape,
        compiler_params=pltpu.CompilerParams(
            dimension_semantics=("arbitrary",), vmem_limit_bytes=_vmem_limit(blocks, 0)),
        name="dt_proj",
    )(*args)


def _split3(v):
    hi = v.astype(BF16)
    r1 = v - hi.astype(F32)
    mid = r1.astype(BF16)
    lo = (r1 - mid.astype(F32)).astype(BF16)
    return hi, mid, lo


def _conv_silu(src_ref, halo_ref, buf_ref, w_ref, b_ref, g):
    q = SSD_CHUNK
    cur = src_ref[...].astype(F32)
    buf_ref[0:SUBLANES, :] = halo_ref[g]
    buf_ref[SUBLANES:SUBLANES + q, :] = cur
    halo_ref[g] = cur[q - SUBLANES:q, :]
    acc = b_ref[...] + w_ref[CONV_WIDTH - 1:CONV_WIDTH, :] * cur
    for k in range(CONV_WIDTH - 1):
        off = SUBLANES - (CONV_WIDTH - 1) + k
        acc = acc + w_ref[k:k + 1, :] * buf_ref[off:off + q, :]
    return _silu(acc)


def _ssd_body(z_ref, x_ref, b_ref, c_ref, dt_ref, dtt_ref,
              cwx_ref, cwb_ref, cwc_ref, cbx_ref, cbb_ref, cbc_ref,
              dtb_r_ref, dtb_c_ref, alog_r_ref, alog_c_ref, d_r_ref, ng_ref, e64_ref,
              y_ref,
              state_ref, hx_ref, hb_ref, hc_ref, xbuf_ref, bbuf_ref, cbuf_ref, dts3_ref, cs_ref, cst_ref):
    t = pl.program_id(1)
    q = SSD_CHUNK
    gw = x_ref.shape[1] // SSM_GROUPS
    hpg = gw // SSM_HEAD_DIM
    row = lax.broadcasted_iota(jnp.int32, (q, q), 0)
    col = lax.broadcasted_iota(jnp.int32, (q, q), 1)
    tril = row >= col

    dts = _softplus(dt_ref[...] + dtb_r_ref[...])
    lo_tri = tril.astype(BF16)
    cs = jnp.zeros((q, LANES), F32)
    for part in _split3(dts * (-jnp.exp(alog_r_ref[...]))):
        cs = cs + jnp.dot(lo_tri, part, preferred_element_type=F32)
    dts3_ref[...] = jnp.concatenate(_split3(dts), axis=1)
    cs_ref[...] = cs
    up_tri = (row <= col).astype(BF16)
    cst = jnp.zeros((LANES, q), F32)
    for part in _split3(_softplus(dtt_ref[...] + dtb_c_ref[...]) * (-jnp.exp(alog_c_ref[...]))):
        cst = cst + jnp.dot(part, up_tri, preferred_element_type=F32)
    cst_ref[...] = cst
    cs3 = jnp.concatenate(_split3(cs), axis=1)

    lane = lax.broadcasted_iota(jnp.int32, (q, LANES), 1)
    d3 = jnp.concatenate(_split3(jnp.broadcast_to(d_r_ref[...], (SUBLANES, LANES))), axis=1)
    for gi in range(SSM_GROUPS):
        xcols = slice(gi * gw, (gi + 1) * gw)
        ncols = slice(gi * SSM_STATE, (gi + 1) * SSM_STATE)

        @pl.when(t == 0)
        def _(gi=gi):
            state_ref[gi] = jnp.zeros(state_ref.shape[1:], F32)
            hx_ref[gi] = jnp.zeros(hx_ref.shape[1:], F32)
            hb_ref[gi] = jnp.zeros(hb_ref.shape[1:], F32)
            hc_ref[gi] = jnp.zeros(hc_ref.shape[1:], F32)

        xs = _conv_silu(x_ref.at[:, xcols], hx_ref, xbuf_ref.at[gi], cwx_ref.at[:, xcols],
                        cbx_ref.at[:, xcols], gi)
        bm = _conv_silu(b_ref.at[:, ncols], hb_ref, bbuf_ref.at[gi], cwb_ref.at[:, ncols],
                        cbb_ref.at[:, ncols], gi).astype(BF16)
        cm = _conv_silu(c_ref.at[:, ncols], hc_ref, cbuf_ref.at[gi], cwc_ref.at[:, ncols],
                        cbc_ref.at[:, ncols], gi).astype(BF16)

        e64 = e64_ref[gi]
        dt_x = jnp.dot(dts3_ref[...], e64, preferred_element_type=F32)
        cs_x = jnp.dot(cs3, e64, preferred_element_type=F32)
        d_x = jnp.dot(d3, e64, preferred_element_type=F32)[0:1, :]
        cs_last = cs_x[q - 1:q, :]

        xdt = xs * dt_x
        xdt_b = xdt.astype(BF16)
        cb = lax.dot_general(cm, bm, (((1,), (1,)), ((), ())), preferred_element_type=F32)
        y_pairs = []
        for j in range(hpg // 2):
            xp = xdt_b[:, j * LANES:(j + 1) * LANES]
            outs = []
            for hh in range(2):
                h = 2 * j + hh
                head = gi * hpg + h
                colb = jnp.broadcast_to(cs_ref[:, head:head + 1], (q, LANES))
                rowb = cst_ref[head:head + 1, :]
                decay = jnp.exp(jnp.where(tril, colb - rowb, -jnp.inf))
                outs.append(jnp.dot((cb * decay).astype(BF16), xp, preferred_element_type=F32))
            y_pairs.append(jnp.where(lane < SSM_HEAD_DIM, outs[0], outs[1]))
        y = jnp.concatenate(y_pairs, axis=1)

        st = state_ref[gi]
        y = y + jnp.dot(cm, st.astype(BF16), preferred_element_type=F32) * jnp.exp(cs_x)
        xsc = (xdt * jnp.exp(cs_last - cs_x)).astype(BF16)
        s_new = lax.dot_general(bm, xsc, (((0,), (0,)), ((), ())), preferred_element_type=F32)
        state_ref[gi] = st * jnp.exp(cs_last) + s_new

        y = y + d_x * xs
        yg = y * _silu(z_ref[:, xcols].astype(F32))
        y_ref[:, xcols] = _rmsnorm_rows(yg, ng_ref[:, xcols]).astype(y_ref.dtype)


def _expansion_matrices(heads_per_group, width):
    k = lax.broadcasted_iota(jnp.int32, (SSM_GROUPS, 3 * LANES, heads_per_group * width), 1) % LANES
    c = lax.broadcasted_iota(jnp.int32, (SSM_GROUPS, 3 * LANES, heads_per_group * width), 2) // width
    gi = lax.broadcasted_iota(jnp.int32, (SSM_GROUPS, 3 * LANES, heads_per_group * width), 0)
    return (k == gi * heads_per_group + c).astype(BF16)


def _ssd_scan(zx, dt, dtt, conv_w, conv_b, dt_bias, a_log, d_skip, norm_g, batch):
    m = zx.shape[0]
    q = SSD_CHUNK
    gn = SSM_GROUPS * SSM_STATE
    d_inner = (zx.shape[1] - 2 * gn) // 2
    gw = d_inner // SSM_GROUPS
    hpg = gw // SSM_HEAD_DIM
    nt = m // batch // q
    x_blk, b_blk, c_blk = 1, 2 * d_inner // gn, 2 * d_inner // gn + 1
    cwb_blk, cwc_blk = d_inner // gn, d_inner // gn + 1

    def pad_row(v):
        return jnp.pad(v, (0, LANES - v.shape[0])).reshape(1, LANES)

    def pad_col(v):
        return jnp.pad(v, (0, LANES - v.shape[0])).reshape(LANES, 1)

    e64 = _expansion_matrices(hpg, SSM_HEAD_DIM)
    rows = lambda blk: (lambda b, t: (b * nt + t, blk))
    cols = lambda blk: (lambda b, t: (0, blk))
    in_specs = [
        pl.BlockSpec((q, d_inner), rows(0)),
        pl.BlockSpec((q, d_inner), rows(x_blk)),
        pl.BlockSpec((q, gn), rows(b_blk)),
        pl.BlockSpec((q, gn), rows(c_blk)),
        pl.BlockSpec((q, LANES), rows(0)),
        pl.BlockSpec((LANES, q), lambda b, t: (0, b * nt + t)),
        pl.BlockSpec((CONV_WIDTH, d_inner), cols(0)),
        pl.BlockSpec((CONV_WIDTH, gn), cols(cwb_blk)),
        pl.BlockSpec((CONV_WIDTH, gn), cols(cwc_blk)),
        pl.BlockSpec((1, d_inner), cols(0)),
        pl.BlockSpec((1, gn), cols(cwb_blk)),
        pl.BlockSpec((1, gn), cols(cwc_blk)),
        pl.BlockSpec((1, LANES), cols(0)),
        pl.BlockSpec((LANES, 1), cols(0)),
        pl.BlockSpec((1, LANES), cols(0)),
        pl.BlockSpec((LANES, 1), cols(0)),
        pl.BlockSpec((1, LANES), cols(0)),
        pl.BlockSpec((1, d_inner), cols(0)),
        pl.BlockSpec(e64.shape, lambda b, t: (0, 0, 0)),
    ]
    scratch_shapes = [
        ((SSM_GROUPS, SSM_STATE, gw), F32),
        ((SSM_GROUPS, SUBLANES, gw), F32),
        ((SSM_GROUPS, SUBLANES, SSM_STATE), F32),
        ((SSM_GROUPS, SUBLANES, SSM_STATE), F32),
        ((SSM_GROUPS, SUBLANES + q, gw), F32),
        ((SSM_GROUPS, SUBLANES + q, SSM_STATE), F32),
        ((SSM_GROUPS, SUBLANES + q, SSM_STATE), F32),
        ((q, 3 * LANES), BF16),
        ((q, LANES), F32),
        ((LANES, q), F32),
    ]
    block_bytes = (3 * _nbytes((q, d_inner), BF16) + 2 * _nbytes((q, gn), BF16) + 2 * _nbytes((q, LANES), F32)
                   + _nbytes(e64.shape, BF16) + (CONV_WIDTH + 2) * _nbytes((SUBLANES, d_inner + 2 * gn), F32))
    return pl.pallas_call(
        _ssd_body,
        grid=(batch, nt),
        in_specs=in_specs,
        out_specs=pl.BlockSpec((q, d_inner), rows(0)),
        out_shape=jax.ShapeDtypeStruct((m, d_inner), BF16),
        scratch_shapes=[pltpu.VMEM(s, d) for s, d in scratch_shapes],
        compiler_params=pltpu.CompilerParams(
            dimension_semantics=("arbitrary", "arbitrary"),
            vmem_limit_bytes=_vmem_limit(block_bytes, sum(_nbytes(s, d) for s, d in scratch_shapes))),
        name="ssd_scan",
    )(zx, zx, zx, zx, dt, dtt, conv_w, conv_w, conv_w, conv_b, conv_b, conv_b,
      pad_row(dt_bias), pad_col(dt_bias), pad_row(a_log), pad_col(a_log), pad_row(d_skip),
      norm_g.reshape(1, d_inner), e64)


def _fold_rows(v, op):
    r, c = v.shape
    v3 = v.reshape(r // SUBLANES, SUBLANES, c)
    return jnp.max(v3, axis=0) if op == "max" else jnp.sum(v3, axis=0)


def _moba_body(q_ref, k_ref, v_ref, o_ref,
               kmean_ref, kaug_ref, vt_ref, qaug_ref, s_ref, m_ref, l_ref, acc_ref):
    qi = pl.program_id(2)
    bs = MOBA_BLOCK
    dh = LANES
    seq = k_ref.shape[0]
    nb = seq // bs
    heads = range(q_ref.shape[1] // dh)
    cbk = ATTN_CHUNK_BLOCKS
    ck = cbk * bs
    c2 = dh ** -0.5 * LOG2_E
    nt = (((1,), (1,)), ((), ()))

    @pl.when(qi == 0)
    def _():
        lane = lax.broadcasted_iota(jnp.int32, (bs, LANES), 1)
        for hh in heads:
            for n in range(nb):
                kb = k_ref[n * bs:(n + 1) * bs, hh * dh:(hh + 1) * dh]
                kmean_ref[hh, n:n + 1, :] = jnp.mean(kb.astype(F32), axis=0, keepdims=True)
                kaug_ref[hh, n * bs:(n + 1) * bs, 0:dh] = kb
                kaug_ref[hh, n * bs:(n + 1) * bs, dh:dh + LANES] = jnp.where(lane == n, 1.0, 0.0).astype(BF16)
                vt_ref[hh, n] = v_ref[n * bs:(n + 1) * bs, hh * dh:(hh + 1) * dh].astype(F32).T.astype(BF16)

    own = pl.multiple_of(qi * bs, bs)
    blk = lax.broadcasted_iota(jnp.int32, (nb, bs), 0)
    key = lax.broadcasted_iota(jnp.int32, (bs, bs), 0)
    qry = lax.broadcasted_iota(jnp.int32, (bs, bs), 1)
    for hh in heads:
        q = q_ref[:, hh * dh:(hh + 1) * dh]
        gate = lax.dot_general(kmean_ref[hh], q.astype(F32), nt,
                               precision=lax.Precision.HIGHEST, preferred_element_type=F32)
        gate = jnp.where(blk < qi, gate, -jnp.inf)
        rank = jnp.zeros((nb, bs), F32)
        for mm in range(nb):
            gm = gate[mm:mm + 1, :]
            beats = jnp.where(gm > gate, 1.0, jnp.where(jnp.logical_and(gm == gate, blk > mm), 1.0, 0.0))
            rank = rank + beats
        chosen = jnp.logical_and(rank < MOBA_TOPK, gate > -jnp.inf)
        bias_t = jnp.where(chosen, 0.0, MASK_BIAS)
        bias_q = jnp.concatenate([bias_t, jnp.zeros((LANES - nb, bs), F32)], axis=0).T
        qaug_ref[hh, :, 0:dh] = q
        qaug_ref[hh, :, dh:dh + LANES] = bias_q.astype(BF16)
        s = lax.dot_general(k_ref[pl.ds(own, bs), hh * dh:(hh + 1) * dh], q, nt,
                            preferred_element_type=F32) * c2
        s = jnp.where(key <= qry, s, MASK_BIAS)
        s_ref[hh, seq:seq + bs, :] = s
        m_ref[hh] = _fold_rows(s, "max")

    for c in range(nb // cbk):
        @pl.when(c * cbk < qi)
        def _(c=c):
            for hh in heads:
                s = lax.dot_general(kaug_ref[hh, c * ck:(c + 1) * ck, :], qaug_ref[hh], nt,
                                    preferred_element_type=F32) * c2
                s_ref[hh, c * ck:(c + 1) * ck, :] = s
                m_ref[hh] = jnp.maximum(m_ref[hh], _fold_rows(s, "max"))

    for hh in heads:
        m = jnp.max(m_ref[hh], axis=0, keepdims=True)
        m_ref[hh] = jnp.broadcast_to(m, (SUBLANES, bs))
        p = jnp.exp2(s_ref[hh, seq:seq + bs, :] - m)
        l_ref[hh] = _fold_rows(p, "sum")
        acc_ref[hh] = jnp.dot(vt_ref[hh, qi], p.astype(BF16), preferred_element_type=F32)

    for c in range(nb // cbk):
        @pl.when(c * cbk < qi)
        def _(c=c):
            for hh in heads:
                p = jnp.exp2(s_ref[hh, c * ck:(c + 1) * ck, :] - m_ref[hh, 0:1, :])
                l_ref[hh] = l_ref[hh] + _fold_rows(p, "sum")
                pb = p.astype(BF16)
                acc = acc_ref[hh]
                for j in range(cbk):
                    acc = acc + jnp.dot(vt_ref[hh, c * cbk + j], pb[j * bs:(j + 1) * bs, :],
                                        preferred_element_type=F32)
                acc_ref[hh] = acc


    for hh in heads:
        l = jnp.sum(l_ref[hh], axis=0, keepdims=True)
        o_ref[:, hh * dh:(hh + 1) * dh] = (acc_ref[hh] / l).T.astype(o_ref.dtype)


def _moba_attention(qkv, batch):
    m = qkv.shape[0]
    seq = m // batch
    dh = qkv.shape[1] // (3 * ATTN_HEADS)
    bs = MOBA_BLOCK
    nq = seq // bs
    hb = ATTN_HEADS_PER_STEP
    ng = ATTN_HEADS // hb
    assert dh == LANES and nq % ATTN_CHUNK_BLOCKS == 0 and nq <= LANES
    block_bytes = 2 * _nbytes((bs, hb * dh), BF16) + 2 * _nbytes((seq, hb * dh), BF16)
    scratch_shapes = [
        ((hb, nq, dh), F32),
        ((hb, seq, dh + LANES), BF16),
        ((hb, nq, dh, bs), BF16),
        ((hb, bs, dh + LANES), BF16),
        ((hb, seq + bs, bs), F32),
        ((hb, SUBLANES, bs), F32),
        ((hb, SUBLANES, bs), F32),
        ((hb, dh, bs), F32),
    ]
    return pl.pallas_call(
        _moba_body,
        grid=(batch, ng, nq),
        in_specs=[pl.BlockSpec((bs, hb * dh), lambda b, h, i: (b * nq + i, h)),
                  pl.BlockSpec((seq, hb * dh), lambda b, h, i: (b, ng + h)),
                  pl.BlockSpec((seq, hb * dh), lambda b, h, i: (b, 2 * ng + h))],
        out_specs=pl.BlockSpec((bs, hb * dh), lambda b, h, i: (b * nq + i, h)),
        out_shape=jax.ShapeDtypeStruct((m, ATTN_HEADS * dh), BF16),
        scratch_shapes=[pltpu.VMEM(s, d) for s, d in scratch_shapes],
        compiler_params=pltpu.CompilerParams(
            dimension_semantics=("arbitrary", "arbitrary", "arbitrary"),
            vmem_limit_bytes=_vmem_limit(block_bytes, sum(_nbytes(s, d) for s, d in scratch_shapes))),
        name="moba_attention",
    )(qkv, qkv, qkv)


def _mamba2_mixer(h, u, gain, w_in_t, layer, conv_w, conv_b, dt_bias, a_log, d_skip, norm_g, w_out,
                  next_gain, batch):
    d_inner = w_out.shape[1]
    n_zx = 2 * d_inner + 2 * SSM_GROUPS * SSM_STATE
    w_dt_t = w_in_t[layer][n_zx:, :]
    if u is None:
        u, dt, dtt = _dt_proj(h, gain, w_dt_t)
    else:
        dt, dtt = _dt_proj(u, None, w_dt_t)
    zx = _fused_matmul([u], [None], [(w_in_t, layer, "f32_t")], [0], [], [], _ep_plain, (BF16,),
                       tm=2048, tn=1024, n_out=n_zx, name="ssm_in_proj")
    y = _ssd_scan(zx, dt, dtt, conv_w, conv_b.reshape(1, -1), dt_bias, a_log, d_skip, norm_g, batch)
    return _fused_matmul([y], [None], [(w_out[layer].astype(BF16), None, "bf16")], [0], [h], [next_gain],
                         _ep_residual_norm, (F32, BF16), tm=512, tn=h.shape[1], single_buffer_weights=True,
                         name="ssm_out_proj")


def _moba_mixer(h, u, w_qkv, w_o, layer, next_gain, batch):
    qkv = _fused_matmul([u], [None], [(w_qkv, layer, "f32")], [0], [], [], _ep_plain, (BF16,),
                        tm=2048, tn=1024, name="attn_qkv")
    o = _moba_attention(qkv, batch)
    d = h.shape[1]
    return _fused_matmul([o], [None], [(w_o[layer].astype(BF16), None, "bf16")], [0], [h], [next_gain],
                         _ep_residual_norm, (F32, BF16), tm=512, tn=d, single_buffer_weights=True,
                         name="attn_out")


def _ffn(h, u, layer, w_gate, w_up, w_down):
    ws = [(w_gate, layer, "f32"), (w_up, layer, "f32")]
    a = _fused_matmul([u], [None], ws, [0, 0], [], [], _ep_swiglu, (BF16,), tm=1024, tn=512, name="ffn_up")
    return _fused_matmul([a], [None], [(w_down, layer, "f32")], [0], [h], [], _ep_residual, (F32,),
                         tm=1024, tn=512, single_buffer_weights=True, name="ffn_down")


def _ple(h, p, layer, gain, w_pgate_l, w_pproj_l, out_gain, last):
    d = h.shape[1]
    ws = [(w_pgate_l.astype(BF16), None, "bf16"), (w_pproj_l.astype(BF16), None, "bf16")]
    if last:
        return _fused_matmul([h, p], [gain, None], ws, [0, 1], [], [out_gain], _ep_ple_final, (F32,),
                             tm=512, tn=d, single_buffer_weights=True, x_layer=layer, name="ple_final")
    return _fused_matmul([h, p], [gain, None], ws, [0, 1], [], [out_gain], _ep_ple_norm, (F32, BF16),
                         tm=512, tn=d, single_buffer_weights=True, x_layer=layer, name="ple")


def kernel(x, p, mix_norm_g, ffn_norm_g, ple_norm_g, ssm_w_in, ssm_conv_w, ssm_conv_b, ssm_dt_bias, ssm_a_log, ssm_d, ssm_norm_g, ssm_w_out, attn_w_qkv, attn_w_o, ffn_w_gate, ffn_w_up, ffn_w_down, ple_w_proj, ple_w_gate, final_norm_g):
    batch, seq, d = x.shape
    m = batch * seq
    depth = p.shape[0]
    ssm_w_in_t = jnp.swapaxes(ssm_w_in, 1, 2)
    h = x.reshape(m, d)
    u = None
    for i in range(depth):
        j = i // 2
        if i % 2 == 0:
            h, u = _mamba2_mixer(h, u, mix_norm_g[i], ssm_w_in_t, j, ssm_conv_w[j], ssm_conv_b[j], ssm_dt_bias[j],
                                 ssm_a_log[j], ssm_d[j], ssm_norm_g[j], ssm_w_out, ffn_norm_g[i], batch)
        else:
            h, u = _moba_mixer(h, u, attn_w_qkv, attn_w_o, j, ffn_norm_g[i], batch)
        h = _ffn(h, u, i, ffn_w_gate, ffn_w_up, ffn_w_down)
        last = i == depth - 1
        out = _ple(h, p.reshape(depth, m, -1), i, ple_norm_g[i], ple_w_gate[i], ple_w_proj[i],
                   final_norm_g if last else mix_norm_g[i + 1], last)
        if not last:
            h, u = out
    return out.reshape(batch, seq, d)
```

```python
import functools

import jax
import jax.numpy as jnp
from jax import lax
from jax.experimental import pallas as pl
from jax.experimental.pallas import tpu as pltpu

NORM_EPS = 1e-6

SSM_HEAD_DIM = 64
SSM_GROUPS = 8
SSM_STATE = 128
CONV_WIDTH = 4
SSD_CHUNK = 128

ATTN_HEADS = 16
MOBA_BLOCK = 256
MOBA_TOPK = 3
ATTN_CHUNK_BLOCKS = 4
ATTN_HEADS_PER_STEP = 4
MASK_BIAS = -1e30
LOG2_E = 1.4426950408889634

LANES = 128
SUBLANES = 8
VMEM_BYTES = 64 * 1024 * 1024
COMPILER_SCRATCH_BYTES = 12 * 1024 * 1024

F32 = jnp.float32
BF16 = jnp.bfloat16


def _nbytes(shape, dtype):
    n = 1
    for s in shape:
        n *= s
    return n * jnp.dtype(dtype).itemsize


def _vmem_limit(block_bytes, scratch_bytes):
    need = 2 * block_bytes + scratch_bytes + COMPILER_SCRATCH_BYTES
    return int(min(need, VMEM_BYTES - 4 * 1024 * 1024))


def _sigmoid(v):
    return 0.5 * jnp.tanh(0.5 * v) + 0.5


def _silu(v):
    hv = 0.5 * v
    return hv * jnp.tanh(hv) + hv


def _softplus(v):
    return jnp.maximum(v, 0.0) + jnp.log1p(jnp.exp(-jnp.abs(v)))


def _rmsnorm_rows(x, g):
    ms = jnp.mean(x * x, axis=-1, keepdims=True)
    return x * lax.rsqrt(ms + NORM_EPS) * g


def _mm_body(*refs, normed, w_x, w_kinds, n_extra, n_rows, n_out, epilogue):
    n_x, n_g, n_w = len(normed), sum(normed), len(w_x)
    refs = list(refs)
    x_refs = [refs.pop(0) for _ in range(n_x)]
    g_refs = [refs.pop(0) for _ in range(n_g)]
    w_refs = [refs.pop(0) for _ in range(n_w)]
    e_refs = [refs.pop(0) for _ in range(n_extra)]
    r_refs = [refs.pop(0) for _ in range(n_rows)]
    o_refs = [refs.pop(0) for _ in range(n_out)]
    wb_refs = {k: refs.pop(0) for k, kind in enumerate(w_kinds) if kind != "bf16"}

    @pl.when(pl.program_id(1) == 0)
    def _():
        for k, wb_ref in wb_refs.items():
            wb_ref[...] = w_refs[k][...].astype(BF16)

    xraw = [x_ref[...] for x_ref in x_refs]
    xs = []
    for x, is_normed in zip(xraw, normed):
        if is_normed:
            x = _rmsnorm_rows(x, g_refs.pop(0)[...])
        xs.append(x.astype(BF16))
    accs = []
    for k, (xi, kind) in enumerate(zip(w_x, w_kinds)):
        w = w_refs[k][...] if kind == "bf16" else wb_refs[k][...]
        dims = (((1,), (1,)), ((), ())) if kind == "f32_t" else (((1,), (0,)), ((), ()))
        accs.append(lax.dot_general(xs[xi], w, dims, preferred_element_type=F32))
    outs = epilogue(accs, [e_ref[...] for e_ref in e_refs], [r_ref[...] for r_ref in r_refs], xraw)
    for o_ref, o in zip(o_refs, outs):
        o_ref[...] = o.astype(o_ref.dtype)


def _fused_matmul(xs, gains, ws, w_x, extras, rows, epilogue, out_dtypes, tm, tn, n_out=None,
                  single_buffer_weights=False, x_layer=0, name="matmul"):
    m = xs[0].shape[0]
    if n_out is None:
        w0, _, kind0 = ws[0]
        n = w0.shape[1] if kind0 == "f32_t" else w0.shape[-1]
    else:
        n = n_out
    grid = (n // tn, m // tm)
    in_specs, block_bytes, scratch, scratch_bytes = [], 0, [], 0
    for x in xs:
        k = x.shape[-1]
        if x.ndim == 3:
            in_specs.append(pl.BlockSpec((None, tm, k), functools.partial(lambda j, i, l: (l, i, 0), l=x_layer)))
        else:
            in_specs.append(pl.BlockSpec((tm, k), lambda j, i: (i, 0)))
        block_bytes += _nbytes((tm, k), x.dtype)
    gain_rows = [g.reshape(1, -1) for g in gains if g is not None]
    for g in gain_rows:
        in_specs.append(pl.BlockSpec(g.shape, lambda j, i: (0, 0)))
    w_mode = dict(pipeline_mode=pl.Buffered(1)) if single_buffer_weights else {}
    buffers = 1 if single_buffer_weights else 2
    for w, layer, kind in ws:
        if kind == "bf16":
            k = w.shape[0]
            in_specs.append(pl.BlockSpec((k, tn), lambda j, i: (0, j), **w_mode))
            block_bytes += _nbytes((k, tn), BF16) * buffers // 2
            continue
        if kind == "f32_t":
            k = w.shape[2]
            shape = (tn, k)
            in_specs.append(pl.BlockSpec((None, tn, k), functools.partial(lambda j, i, l: (l, j, 0), l=layer),
                                         **w_mode))
        else:
            k = w.shape[1]
            shape = (k, tn)
            in_specs.append(pl.BlockSpec((None, k, tn), functools.partial(lambda j, i, l: (l, 0, j), l=layer),
                                         **w_mode))
        block_bytes += _nbytes(shape, F32) * buffers // 2
        scratch.append(pltpu.VMEM(shape, BF16))
        scratch_bytes += _nbytes(shape, BF16)
    for e in extras:
        in_specs.append(pl.BlockSpec((tm, tn), lambda j, i: (i, j)))
        block_bytes += _nbytes((tm, tn), e.dtype)
    row_params = [r.reshape(1, -1) for r in rows]
    for r in row_params:
        in_specs.append(pl.BlockSpec((1, tn), lambda j, i: (0, j)))
    for dt in out_dtypes:
        block_bytes += _nbytes((tm, tn), dt)
    body = functools.partial(_mm_body, normed=tuple(g is not None for g in gains), w_x=tuple(w_x),
                             w_kinds=tuple(kind for _, _, kind in ws), n_extra=len(extras),
                             n_rows=len(rows), n_out=len(out_dtypes), epilogue=epilogue)
    outs = pl.pallas_call(
        body,
        grid=grid,
        in_specs=in_specs,
        out_specs=[pl.BlockSpec((tm, tn), lambda j, i: (i, j)) for _ in out_dtypes],
        out_shape=[jax.ShapeDtypeStruct((m, n), dt) for dt in out_dtypes],
        scratch_shapes=scratch,
        compiler_params=pltpu.CompilerParams(
            dimension_semantics=("arbitrary", "arbitrary"),
            vmem_limit_bytes=_vmem_limit(block_bytes, scratch_bytes)),
        name=name,
    )(*xs, *gain_rows, *[w for w, _, _ in ws], *extras, *row_params)
    return outs[0] if len(outs) == 1 else outs


def _ep_plain(accs, extras, rows, xraw):
    return (accs[0],)


def _ep_residual(accs, extras, rows, xraw):
    return (extras[0] + accs[0],)


def _ep_residual_norm(accs, extras, rows, xraw):
    h = extras[0] + accs[0]
    return h, _rmsnorm_rows(h, rows[0])


def _ep_swiglu(accs, extras, rows, xraw):
    return (_silu(accs[0]) * accs[1],)


def _ple_update(accs, xraw):
    return xraw[0] + _sigmoid(accs[0]) * accs[1]


def _ep_ple_norm(accs, extras, rows, xraw):
    h = _ple_update(accs, xraw)
    return h, _rmsnorm_rows(h, rows[0])


def _ep_ple_final(accs, extras, rows, xraw):
    return (_rmsnorm_rows(_ple_update(accs, xraw), rows[0]),)


def _dt_body(*refs, normed):
    nt = (((1,), (1,)), ((), ()))
    if normed:
        x_ref, g_ref, wt_ref, u_ref, dt_ref, dtt_ref = refs
        u = _rmsnorm_rows(x_ref[...], g_ref[...]).astype(BF16)
        u_ref[...] = u
    else:
        x_ref, wt_ref, dt_ref, dtt_ref = refs
        u = x_ref[...]
    wt = wt_ref[...].astype(BF16)
    dt_ref[...] = lax.dot_general(u, wt, nt, preferred_element_type=F32)
    dtt_ref[...] = lax.dot_general(wt, u, nt, preferred_element_type=F32)


def _dt_proj(x, gain, w_dt_t, tm=1024):
    m, k = x.shape
    normed = gain is not None
    wt_pad = jnp.pad(w_dt_t, ((0, LANES - w_dt_t.shape[0]), (0, 0)))
    row_spec = pl.BlockSpec((tm, k), lambda i: (i, 0))
    in_specs = [row_spec] + ([pl.BlockSpec((1, k), lambda i: (0, 0))] if normed else [])
    in_specs.append(pl.BlockSpec((LANES, k), lambda i: (0, 0)))
    out_specs = [pl.BlockSpec((tm, LANES), lambda i: (i, 0)), pl.BlockSpec((LANES, tm), lambda i: (0, i))]
    out_shape = [jax.ShapeDtypeStruct((m, LANES), F32), jax.ShapeDtypeStruct((LANES, m), F32)]
    if normed:
        out_specs.insert(0, row_spec)
        out_shape.insert(0, jax.ShapeDtypeStruct((m, k), BF16))
    blocks = (_nbytes((tm, k), x.dtype) + _nbytes((LANES, k), F32) + 2 * _nbytes((tm, LANES), F32)
              + (_nbytes((tm, k), BF16) if normed else 0))
    args = (x, gain.reshape(1, k), wt_pad) if normed else (x, wt_pad)
    return pl.pallas_call(
        functools.partial(_dt_body, normed=normed),
        grid=(m // tm,),
        in_specs=in_specs,
        out_specs=out_specs,
        out_shape=out_shape,
        compiler_params=pltpu.CompilerParams(
            dimension_semantics=("arbitrary",), vmem_limit_bytes=_vmem_limit(blocks, 0)),
        name="dt_proj",
    )(*args)


def _split3(v):
    hi = v.astype(BF16)
    r1 = v - hi.astype(F32)
    mid = r1.astype(BF16)
    lo = (r1 - mid.astype(F32)).astype(BF16)
    return hi, mid, lo


def _conv_silu(src_ref, halo_ref, buf_ref, w_ref, b_ref, g):
    q = SSD_CHUNK
    cur = src_ref[...].astype(F32)
    buf_ref[0:SUBLANES, :] = halo_ref[g]
    buf_ref[SUBLANES:SUBLANES + q, :] = cur
    halo_ref[g] = cur[q - SUBLANES:q, :]
    acc = b_ref[...] + w_ref[CONV_WIDTH - 1:CONV_WIDTH, :] * cur
    for k in range(CONV_WIDTH - 1):
        off = SUBLANES - (CONV_WIDTH - 1) + k
        acc = acc + w_ref[k:k + 1, :] * buf_ref[off:off + q, :]
    return _silu(acc)


def _ssd_body(z_ref, x_ref, b_ref, c_ref, dt_ref, dtt_ref,
              cwx_ref, cwb_ref, cwc_ref, cbx_ref, cbb_ref, cbc_ref,
              dtb_r_ref, dtb_c_ref, alog_r_ref, alog_c_ref, d_r_ref, ng_ref, e64_ref,
              y_ref,
              state_ref, hx_ref, hb_ref, hc_ref, xbuf_ref, bbuf_ref, cbuf_ref, dts3_ref, cs_ref, cst_ref):
    t = pl.program_id(1)
    q = SSD_CHUNK
    gw = x_ref.shape[1] // SSM_GROUPS
    hpg = gw // SSM_HEAD_DIM
    row = lax.broadcasted_iota(jnp.int32, (q, q), 0)
    col = lax.broadcasted_iota(jnp.int32, (q, q), 1)
    tril = row >= col

    dts = _softplus(dt_ref[...] + dtb_r_ref[...])
    lo_tri = tril.astype(BF16)
    cs = jnp.zeros((q, LANES), F32)
    for part in _split3(dts * (-jnp.exp(alog_r_ref[...]))):
        cs = cs + jnp.dot(lo_tri, part, preferred_element_type=F32)
    dts3_ref[...] = jnp.concatenate(_split3(dts), axis=1)
    cs_ref[...] = cs
    up_tri = (row <= col).astype(BF16)
    cst = jnp.zeros((LANES, q), F32)
    for part in _split3(_softplus(dtt_ref[...] + dtb_c_ref[...]) * (-jnp.exp(alog_c_ref[...]))):
        cst = cst + jnp.dot(part, up_tri, preferred_element_type=F32)
    cst_ref[...] = cst
    cs3 = jnp.concatenate(_split3(cs), axis=1)

    lane = lax.broadcasted_iota(jnp.int32, (q, LANES), 1)
    d3 = jnp.concatenate(_split3(jnp.broadcast_to(d_r_ref[...], (SUBLANES, LANES))), axis=1)
    for gi in range(SSM_GROUPS):
        xcols = slice(gi * gw, (gi + 1) * gw)
        ncols = slice(gi * SSM_STATE, (gi + 1) * SSM_STATE)

        @pl.when(t == 0)
        def _(gi=gi):
            state_ref[gi] = jnp.zeros(state_ref.shape[1:], F32)
            hx_ref[gi] = jnp.zeros(hx_ref.shape[1:], F32)
            hb_ref[gi] = jnp.zeros(hb_ref.shape[1:], F32)
            hc_ref[gi] = jnp.zeros(hc_ref.shape[1:], F32)

        xs = _conv_silu(x_ref.at[:, xcols], hx_ref, xbuf_ref.at[gi], cwx_ref.at[:, xcols],
                        cbx_ref.at[:, xcols], gi)
        bm = _conv_silu(b_ref.at[:, ncols], hb_ref, bbuf_ref.at[gi], cwb_ref.at[:, ncols],
                        cbb_ref.at[:, ncols], gi).astype(BF16)
        cm = _conv_silu(c_ref.at[:, ncols], hc_ref, cbuf_ref.at[gi], cwc_ref.at[:, ncols],
                        cbc_ref.at[:, ncols], gi).astype(BF16)

        e64 = e64_ref[gi]
        dt_x = jnp.dot(dts3_ref[...], e64, preferred_element_type=F32)
        cs_x = jnp.dot(cs3, e64, preferred_element_type=F32)
        d_x = jnp.dot(d3, e64, preferred_element_type=F32)[0:1, :]
        cs_last = cs_x[q - 1:q, :]

        xdt = xs * dt_x
        xdt_b = xdt.astype(BF16)
        cb = lax.dot_general(cm, bm, (((1,), (1,)), ((), ())), preferred_element_type=F32)
        y_pairs = []
        for j in range(hpg // 2):
            xp = xdt_b[:, j * LANES:(j + 1) * LANES]
            outs = []
            for hh in range(2):
                h = 2 * j + hh
                head = gi * hpg + h
                colb = jnp.broadcast_to(cs_ref[:, head:head + 1], (q, LANES))
                rowb = cst_ref[head:head + 1, :]
                decay = jnp.exp(jnp.where(tril, colb - rowb, -jnp.inf))
                outs.append(jnp.dot((cb * decay).astype(BF16), xp, preferred_element_type=F32))
            y_pairs.append(jnp.where(lane < SSM_HEAD_DIM, outs[0], outs[1]))
        y = jnp.concatenate(y_pairs, axis=1)

        st = state_ref[gi]
        y = y + jnp.dot(cm, st.astype(BF16), preferred_element_type=F32) * jnp.exp(cs_x)
        xsc = (xdt * jnp.exp(cs_last - cs_x)).astype(BF16)
        s_new = lax.dot_general(bm, xsc, (((0,), (0,)), ((), ())), preferred_element_type=F32)
        state_ref[gi] = st * jnp.exp(cs_last) + s_new

        y = y + d_x * xs
        yg = y * _silu(z_ref[:, xcols].astype(F32))
        y_ref[:, xcols] = _rmsnorm_rows(yg, ng_ref[:, xcols]).astype(y_ref.dtype)


def _expansion_matrices(heads_per_group, width):
    k = lax.broadcasted_iota(jnp.int32, (SSM_GROUPS, 3 * LANES, heads_per_group * width), 1) % LANES
    c = lax.broadcasted_iota(jnp.int32, (SSM_GROUPS, 3 * LANES, heads_per_group * width), 2) // width
    gi = lax.broadcasted_iota(jnp.int32, (SSM_GROUPS, 3 * LANES, heads_per_group * width), 0)
    return (k == gi * heads_per_group + c).astype(BF16)


def _ssd_scan(zx, dt, dtt, conv_w, conv_b, dt_bias, a_log, d_skip, norm_g, batch):
    m = zx.shape[0]
    q = SSD_CHUNK
    gn = SSM_GROUPS * SSM_STATE
    d_inner = (zx.shape[1] - 2 * gn) // 2
    gw = d_inner // SSM_GROUPS
    hpg = gw // SSM_HEAD_DIM
    nt = m // batch // q
    x_blk, b_blk, c_blk = 1, 2 * d_inner // gn, 2 * d_inner // gn + 1
    cwb_blk, cwc_blk = d_inner // gn, d_inner // gn + 1

    def pad_row(v):
        return jnp.pad(v, (0, LANES - v.shape[0])).reshape(1, LANES)

    def pad_col(v):
        return jnp.pad(v, (0, LANES - v.shape[0])).reshape(LANES, 1)

    e64 = _expansion_matrices(hpg, SSM_HEAD_DIM)
    rows = lambda blk: (lambda b, t: (b * nt + t, blk))
    cols = lambda blk: (lambda b, t: (0, blk))
    in_specs = [
        pl.BlockSpec((q, d_inner), rows(0)),
        pl.BlockSpec((q, d_inner), rows(x_blk)),
        pl.BlockSpec((q, gn), rows(b_blk)),
        pl.BlockSpec((q, gn), rows(c_blk)),
        pl.BlockSpec((q, LANES), rows(0)),
        pl.BlockSpec((LANES, q), lambda b, t: (0, b * nt + t)),
        pl.BlockSpec((CONV_WIDTH, d_inner), cols(0)),
        pl.BlockSpec((CONV_WIDTH, gn), cols(cwb_blk)),
        pl.BlockSpec((CONV_WIDTH, gn), cols(cwc_blk)),
        pl.BlockSpec((1, d_inner), cols(0)),
        pl.BlockSpec((1, gn), cols(cwb_blk)),
        pl.BlockSpec((1, gn), cols(cwc_blk)),
        pl.BlockSpec((1, LANES), cols(0)),
        pl.BlockSpec((LANES, 1), cols(0)),
        pl.BlockSpec((1, LANES), cols(0)),
        pl.BlockSpec((LANES, 1), cols(0)),
        pl.BlockSpec((1, LANES), cols(0)),
        pl.BlockSpec((1, d_inner), cols(0)),
        pl.BlockSpec(e64.shape, lambda b, t: (0, 0, 0)),
    ]
    scratch_shapes = [
        ((SSM_GROUPS, SSM_STATE, gw), F32),
        ((SSM_GROUPS, SUBLANES, gw), F32),
        ((SSM_GROUPS, SUBLANES, SSM_STATE), F32),
        ((SSM_GROUPS, SUBLANES, SSM_STATE), F32),
        ((SSM_GROUPS, SUBLANES + q, gw), F32),
        ((SSM_GROUPS, SUBLANES + q, SSM_STATE), F32),
        ((SSM_GROUPS, SUBLANES + q, SSM_STATE), F32),
        ((q, 3 * LANES), BF16),
        ((q, LANES), F32),
        ((LANES, q), F32),
    ]
    block_bytes = (3 * _nbytes((q, d_inner), BF16) + 2 * _nbytes((q, gn), BF16) + 2 * _nbytes((q, LANES), F32)
                   + _nbytes(e64.shape, BF16) + (CONV_WIDTH + 2) * _nbytes((SUBLANES, d_inner + 2 * gn), F32))
    return pl.pallas_call(
        _ssd_body,
        grid=(batch, nt),
        in_specs=in_specs,
        out_specs=pl.BlockSpec((q, d_inner), rows(0)),
        out_shape=jax.ShapeDtypeStruct((m, d_inner), BF16),
        scratch_shapes=[pltpu.VMEM(s, d) for s, d in scratch_shapes],
        compiler_params=pltpu.CompilerParams(
            dimension_semantics=("arbitrary", "arbitrary"),
            vmem_limit_bytes=_vmem_limit(block_bytes, sum(_nbytes(s, d) for s, d in scratch_shapes))),
        name="ssd_scan",
    )(zx, zx, zx, zx, dt, dtt, conv_w, conv_w, conv_w, conv_b, conv_b, conv_b,
      pad_row(dt_bias), pad_col(dt_bias), pad_row(a_log), pad_col(a_log), pad_row(d_skip),
      norm_g.reshape(1, d_inner), e64)


def _fold_rows(v, op):
    r, c = v.shape
    v3 = v.reshape(r // SUBLANES, SUBLANES, c)
    return jnp.max(v3, axis=0) if op == "max" else jnp.sum(v3, axis=0)


def _moba_body(q_ref, k_ref, v_ref, o_ref,
               kmean_ref, kaug_ref, vt_ref, qaug_ref, s_ref, m_ref, l_ref, acc_ref):
    qi = pl.program_id(2)
    bs = MOBA_BLOCK
    dh = LANES
    seq = k_ref.shape[0]
    nb = seq // bs
    heads = range(q_ref.shape[1] // dh)
    cbk = ATTN_CHUNK_BLOCKS
    ck = cbk * bs
    c2 = dh ** -0.5 * LOG2_E
    nt = (((1,), (1,)), ((), ()))

    @pl.when(qi == 0)
    def _():
        lane = lax.broadcasted_iota(jnp.int32, (bs, LANES), 1)
        for hh in heads:
            for n in range(nb):
                kb = k_ref[n * bs:(n + 1) * bs, hh * dh:(hh + 1) * dh]
                kmean_ref[hh, n:n + 1, :] = jnp.mean(kb.astype(F32), axis=0, keepdims=True)
                kaug_ref[hh, n * bs:(n + 1) * bs, 0:dh] = kb
                kaug_ref[hh, n * bs:(n + 1) * bs, dh:dh + LANES] = jnp.where(lane == n, 1.0, 0.0).astype(BF16)
                vt_ref[hh, n] = v_ref[n * bs:(n + 1) * bs, hh * dh:(hh + 1) * dh].astype(F32).T.astype(BF16)

    own = pl.multiple_of(qi * bs, bs)
    blk = lax.broadcasted_iota(jnp.int32, (nb, bs), 0)
    key = lax.broadcasted_iota(jnp.int32, (bs, bs), 0)
    qry = lax.broadcasted_iota(jnp.int32, (bs, bs), 1)
    for hh in heads:
        q = q_ref[:, hh * dh:(hh + 1) * dh]
        gate = lax.dot_general(kmean_ref[hh], q.astype(F32), nt,
                               precision=lax.Precision.HIGHEST, preferred_element_type=F32)
        gate = jnp.where(blk < qi, gate, -jnp.inf)
        rank = jnp.zeros((nb, bs), F32)
        for mm in range(nb):
            gm = gate[mm:mm + 1, :]
            beats = jnp.where(gm > gate, 1.0, jnp.where(jnp.logical_and(gm == gate, blk > mm), 1.0, 0.0))
            rank = rank + beats
        chosen = jnp.logical_and(rank < MOBA_TOPK, gate > -jnp.inf)
        bias_t = jnp.where(chosen, 0.0, MASK_BIAS)
        bias_q = jnp.concatenate([bias_t, jnp.zeros((LANES - nb, bs), F32)], axis=0).T
        qaug_ref[hh, :, 0:dh] = q
        qaug_ref[hh, :, dh:dh + LANES] = bias_q.astype(BF16)
        s = lax.dot_general(k_ref[pl.ds(own, bs), hh * dh:(hh + 1) * dh], q, nt,
                            preferred_element_type=F32) * c2
        s = jnp.where(key <= qry, s, MASK_BIAS)
        s_ref[hh, seq:seq + bs, :] = s
        m_ref[hh] = _fold_rows(s, "max")

    for c in range(nb // cbk):
        @pl.when(c * cbk < qi)
        def _(c=c):
            for hh in heads:
                s = lax.dot_general(kaug_ref[hh, c * ck:(c + 1) * ck, :], qaug_ref[hh], nt,
                                    preferred_element_type=F32) * c2
                s_ref[hh, c * ck:(c + 1) * ck, :] = s
                m_ref[hh] = jnp.maximum(m_ref[hh], _fold_rows(s, "max"))

    for hh in heads:
        m = jnp.max(m_ref[hh], axis=0, keepdims=True)
        m_ref[hh] = jnp.broadcast_to(m, (SUBLANES, bs))
        p = jnp.exp2(s_ref[hh, seq:seq + bs, :] - m)
        l_ref[hh] = _fold_rows(p, "sum")
        acc_ref[hh] = jnp.dot(vt_ref[hh, qi], p.astype(BF16), preferred_element_type=F32)

    for c in range(nb // cbk):
        @pl.when(c * cbk < qi)
        def _(c=c):
            for hh in heads:
                p = jnp.exp2(s_ref[hh, c * ck:(c + 1) * ck, :] - m_ref[hh, 0:1, :])
                l_ref[hh] = l_ref[hh] + _fold_rows(p, "sum")
                pb = p.astype(BF16)
                acc = acc_ref[hh]
                for j in range(cbk):
                    acc = acc + jnp.dot(vt_ref[hh, c * cbk + j], pb[j * bs:(j + 1) * bs, :],
                                        preferred_element_type=F32)
                acc_ref[hh] = acc


    for hh in heads:
        l = jnp.sum(l_ref[hh], axis=0, keepdims=True)
        o_ref[:, hh * dh:(hh + 1) * dh] = (acc_ref[hh] / l).T.astype(o_ref.dtype)


def _moba_attention(qkv, batch):
    m = qkv.shape[0]
    seq = m // batch
    dh = qkv.shape[1] // (3 * ATTN_HEADS)
    bs = MOBA_BLOCK
    nq = seq // bs
    hb = ATTN_HEADS_PER_STEP
    ng = ATTN_HEADS // hb
    assert dh == LANES and nq % ATTN_CHUNK_BLOCKS == 0 and nq <= LANES
    block_bytes = 2 * _nbytes((bs, hb * dh), BF16) + 2 * _nbytes((seq, hb * dh), BF16)
    scratch_shapes = [
        ((hb, nq, dh), F32),
        ((hb, seq, dh + LANES), BF16),
        ((hb, nq, dh, bs), BF16),
        ((hb, bs, dh + LANES), BF16),
        ((hb, seq + bs, bs), F32),
        ((hb, SUBLANES, bs), F32),
        ((hb, SUBLANES, bs), F32),
        ((hb, dh, bs), F32),
    ]
    return pl.pallas_call(
        _moba_body,
        grid=(batch, ng, nq),
        in_specs=[pl.BlockSpec((bs, hb * dh), lambda b, h, i: (b * nq + i, h)),
                  pl.BlockSpec((seq, hb * dh), lambda b, h, i: (b, ng + h)),
                  pl.BlockSpec((seq, hb * dh), lambda b, h, i: (b, 2 * ng + h))],
        out_specs=pl.BlockSpec((bs, hb * dh), lambda b, h, i: (b * nq + i, h)),
        out_shape=jax.ShapeDtypeStruct((m, ATTN_HEADS * dh), BF16),
        scratch_shapes=[pltpu.VMEM(s, d) for s, d in scratch_shapes],
        compiler_params=pltpu.CompilerParams(
            dimension_semantics=("arbitrary", "arbitrary", "arbitrary"),
            vmem_limit_bytes=_vmem_limit(block_bytes, sum(_nbytes(s, d) for s, d in scratch_shapes))),
        name="moba_attention",
    )(qkv, qkv, qkv)


def _mamba2_mixer(h, u, gain, w_in_t, layer, conv_w, conv_b, dt_bias, a_log, d_skip, norm_g, w_out,
                  next_gain, batch):
    d_inner = w_out.shape[1]
    n_zx = 2 * d_inner + 2 * SSM_GROUPS * SSM_STATE
    w_dt_t = w_in_t[layer][n_zx:, :]
    if u is None:
        u, dt, dtt = _dt_proj(h, gain, w_dt_t)
    else:
        dt, dtt = _dt_proj(u, None, w_dt_t)
    zx = _fused_matmul([u], [None], [(w_in_t, layer, "f32_t")], [0], [], [], _ep_plain, (BF16,),
                       tm=2048, tn=1024, n_out=n_zx, name="ssm_in_proj")
    y = _ssd_scan(zx, dt, dtt, conv_w, conv_b.reshape(1, -1), dt_bias, a_log, d_skip, norm_g, batch)
    return _fused_matmul([y], [None], [(w_out[layer].astype(BF16), None, "bf16")], [0], [h], [next_gain],
                         _ep_residual_norm, (F32, BF16), tm=512, tn=h.shape[1], single_buffer_weights=True,
                         name="ssm_out_proj")


def _moba_mixer(h, u, w_qkv, w_o, layer, next_gain, batch):
    qkv = _fused_matmul([u], [None], [(w_qkv, layer, "f32")], [0], [], [], _ep_plain, (BF16,),
                        tm=2048, tn=1024, name="attn_qkv")
    o = _moba_attention(qkv, batch)
    d = h.shape[1]
    return _fused_matmul([o], [None], [(w_o[layer].astype(BF16), None, "bf16")], [0], [h], [next_gain],
                         _ep_residual_norm, (F32, BF16), tm=512, tn=d, single_buffer_weights=True,
                         name="attn_out")


def _ffn(h, u, layer, w_gate, w_up, w_down):
    ws = [(w_gate, layer, "f32"), (w_up, layer, "f32")]
    a = _fused_matmul([u], [None], ws, [0, 0], [], [], _ep_swiglu, (BF16,), tm=1024, tn=512, name="ffn_up")
    return _fused_matmul([a], [None], [(w_down, layer, "f32")], [0], [h], [], _ep_residual, (F32,),
                         tm=512, tn=1024, single_buffer_weights=True, name="ffn_down")


def _ple(h, p, layer, gain, w_pgate_l, w_pproj_l, out_gain, last):
    d = h.shape[1]
    ws = [(w_pgate_l.astype(BF16), None, "bf16"), (w_pproj_l.astype(BF16), None, "bf16")]
    if last:
        return _fused_matmul([h, p], [gain, None], ws, [0, 1], [], [out_gain], _ep_ple_final, (F32,),
                             tm=512, tn=d, single_buffer_weights=True, x_layer=layer, name="ple_final")
    return _fused_matmul([h, p], [gain, None], ws, [0, 1], [], [out_gain], _ep_ple_norm, (F32, BF16),
                         tm=512, tn=d, single_buffer_weights=True, x_layer=layer, name="ple")


def kernel(x, p, mix_norm_g, ffn_norm_g, ple_norm_g, ssm_w_in, ssm_conv_w, ssm_conv_b, ssm_dt_bias, ssm_a_log, ssm_d, ssm_norm_g, ssm_w_out, attn_w_qkv, attn_w_o, ffn_w_gate, ffn_w_up, ffn_w_down, ple_w_proj, ple_w_gate, final_norm_g):
    batch, seq, d = x.shape
    m = batch * seq
    depth = p.shape[0]
    ssm_w_in_t = jnp.swapaxes(ssm_w_in, 1, 2)
    h = x.reshape(m, d)
    u = None
    for i in range(depth):
        j = i // 2
        if i % 2 == 0:
            h, u = _mamba2_mixer(h, u, mix_norm_g[i], ssm_w_in_t, j, ssm_conv_w[j], ssm_conv_b[j], ssm_dt_bias[j],
                                 ssm_a_log[j], ssm_d[j], ssm_norm_g[j], ssm_w_out, ffn_norm_g[i], batch)
        else:
            h, u = _moba_mixer(h, u, attn_w_qkv, attn_w_o, j, ffn_norm_g[i], batch)
        h = _ffn(h, u, i, ffn_w_gate, ffn_w_up, ffn_w_down)
        last = i == depth - 1
        out = _ple(h, p.reshape(depth, m, -1), i, ple_norm_g[i], ple_w_gate[i], ple_w_proj[i],
                   final_norm_g if last else mix_norm_g[i + 1], last)
        if not last:
            h, u = out
    return out.reshape(batch, seq, d)
```

```python
import functools

import jax
import jax.numpy as jnp
from jax import lax
from jax.experimental import pallas as pl
from jax.experimental.pallas import tpu as pltpu

NORM_EPS = 1e-6

SSM_HEAD_DIM = 64
SSM_GROUPS = 8
SSM_STATE = 128
CONV_WIDTH = 4
SSD_CHUNK = 128

ATTN_HEADS = 16
MOBA_BLOCK = 256
MOBA_TOPK = 3
ATTN_CHUNK_BLOCKS = 4
ATTN_HEADS_PER_STEP = 4
MASK_BIAS = -1e30
LOG2_E = 1.4426950408889634

LANES = 128
SUBLANES = 8
VMEM_BYTES = 64 * 1024 * 1024
COMPILER_SCRATCH_BYTES = 12 * 1024 * 1024

F32 = jnp.float32
BF16 = jnp.bfloat16


def _nbytes(shape, dtype):
    n = 1
    for s in shape:
        n *= s
    return n * jnp.dtype(dtype).itemsize


def _vmem_limit(block_bytes, scratch_bytes):
    need = 2 * block_bytes + scratch_bytes + COMPILER_SCRATCH_BYTES
    return int(min(need, VMEM_BYTES - 4 * 1024 * 1024))


def _sigmoid(v):
    return 0.5 * jnp.tanh(0.5 * v) + 0.5


def _silu(v):
    hv = 0.5 * v
    return hv * jnp.tanh(hv) + hv


def _softplus(v):
    return jnp.maximum(v, 0.0) + jnp.log1p(jnp.exp(-jnp.abs(v)))


def _rmsnorm_rows(x, g):
    ms = jnp.mean(x * x, axis=-1, keepdims=True)
    return x * lax.rsqrt(ms + NORM_EPS) * g


def _cast_groups(w_x, w_kinds):
    groups = []
    for k, kind in enumerate(w_kinds):
        if kind == "bf16":
            continue
        mate = next((g for g in groups if kind == "f32" and w_kinds[g[0]] == "f32" and w_x[g[0]] == w_x[k]), None)
        if mate is None:
            groups.append([k])
        else:
            mate.append(k)
    return groups


def _mm_body(*refs, normed, w_x, w_kinds, n_extra, n_rows, n_out, epilogue):
    n_x, n_g, n_w = len(normed), sum(normed), len(w_x)
    groups = _cast_groups(w_x, w_kinds)
    refs = list(refs)
    x_refs = [refs.pop(0) for _ in range(n_x)]
    g_refs = [refs.pop(0) for _ in range(n_g)]
    w_refs = [refs.pop(0) for _ in range(n_w)]
    e_refs = [refs.pop(0) for _ in range(n_extra)]
    r_refs = [refs.pop(0) for _ in range(n_rows)]
    o_refs = [refs.pop(0) for _ in range(n_out)]
    wb_refs = [refs.pop(0) for _ in groups]
    tn = o_refs[0].shape[1]

    @pl.when(pl.program_id(1) == 0)
    def _():
        for grp, wb_ref in zip(groups, wb_refs):
            if w_kinds[grp[0]] == "f32_t":
                wb_ref[...] = w_refs[grp[0]][...].astype(BF16)
            else:
                for slot, k in enumerate(grp):
                    wb_ref[:, slot * tn:(slot + 1) * tn] = w_refs[k][...].astype(BF16)

    xraw = [x_ref[...] for x_ref in x_refs]
    xs = []
    for x, is_normed in zip(xraw, normed):
        if is_normed:
            x = _rmsnorm_rows(x, g_refs.pop(0)[...])
        xs.append(x.astype(BF16))
    accs = [None] * n_w
    for grp, wb_ref in zip(groups, wb_refs):
        x = xs[w_x[grp[0]]]
        if w_kinds[grp[0]] == "f32_t":
            accs[grp[0]] = lax.dot_general(x, wb_ref[...], (((1,), (1,)), ((), ())), preferred_element_type=F32)
        else:
            acc = jnp.dot(x, wb_ref[...], preferred_element_type=F32)
            for slot, k in enumerate(grp):
                accs[k] = acc[:, slot * tn:(slot + 1) * tn]
    for k, kind in enumerate(w_kinds):
        if kind == "bf16":
            accs[k] = jnp.dot(xs[w_x[k]], w_refs[k][...], preferred_element_type=F32)
    outs = epilogue(accs, [e_ref[...] for e_ref in e_refs], [r_ref[...] for r_ref in r_refs], xraw)
    for o_ref, o in zip(o_refs, outs):
        o_ref[...] = o.astype(o_ref.dtype)


def _fused_matmul(xs, gains, ws, w_x, extras, rows, epilogue, out_dtypes, tm, tn, n_out=None,
                  single_buffer_weights=False, x_layer=0, name="matmul"):
    m = xs[0].shape[0]
    if n_out is None:
        w0, _, kind0 = ws[0]
        n = w0.shape[1] if kind0 == "f32_t" else w0.shape[-1]
    else:
        n = n_out
    grid = (n // tn, m // tm)
    in_specs, block_bytes, scratch, scratch_bytes = [], 0, [], 0
    for x in xs:
        k = x.shape[-1]
        if x.ndim == 3:
            in_specs.append(pl.BlockSpec((None, tm, k), functools.partial(lambda j, i, l: (l, i, 0), l=x_layer)))
        else:
            in_specs.append(pl.BlockSpec((tm, k), lambda j, i: (i, 0)))
        block_bytes += _nbytes((tm, k), x.dtype)
    gain_rows = [g.reshape(1, -1) for g in gains if g is not None]
    for g in gain_rows:
        in_specs.append(pl.BlockSpec(g.shape, lambda j, i: (0, 0)))
    w_mode = dict(pipeline_mode=pl.Buffered(1)) if single_buffer_weights else {}
    buffers = 1 if single_buffer_weights else 2
    for w, layer, kind in ws:
        if kind == "bf16":
            k = w.shape[0]
            in_specs.append(pl.BlockSpec((k, tn), lambda j, i: (0, j), **w_mode))
            block_bytes += _nbytes((k, tn), BF16) * buffers // 2
            continue
        if kind == "f32_t":
            k = w.shape[2]
            shape = (tn, k)
            in_specs.append(pl.BlockSpec((None, tn, k), functools.partial(lambda j, i, l: (l, j, 0), l=layer),
                                         **w_mode))
        else:
            k = w.shape[1]
            shape = (k, tn)
            in_specs.append(pl.BlockSpec((None, k, tn), functools.partial(lambda j, i, l: (l, 0, j), l=layer),
                                         **w_mode))
        block_bytes += _nbytes(shape, F32) * buffers // 2
    for grp in _cast_groups(w_x, [kind for _, _, kind in ws]):
        w, _, kind = ws[grp[0]]
        shape = (tn, w.shape[2]) if kind == "f32_t" else (w.shape[1], len(grp) * tn)
        scratch.append(pltpu.VMEM(shape, BF16))
        scratch_bytes += _nbytes(shape, BF16)
    for e in extras:
        in_specs.append(pl.BlockSpec((tm, tn), lambda j, i: (i, j)))
        block_bytes += _nbytes((tm, tn), e.dtype)
    row_params = [r.reshape(1, -1) for r in rows]
    for r in row_params:
        in_specs.append(pl.BlockSpec((1, tn), lambda j, i: (0, j)))
    for dt in out_dtypes:
        block_bytes += _nbytes((tm, tn), dt)
    body = functools.partial(_mm_body, normed=tuple(g is not None for g in gains), w_x=tuple(w_x),
                             w_kinds=tuple(kind for _, _, kind in ws), n_extra=len(extras),
                             n_rows=len(rows), n_out=len(out_dtypes), epilogue=epilogue)
    outs = pl.pallas_call(
        body,
        grid=grid,
        in_specs=in_specs,
        out_specs=[pl.BlockSpec((tm, tn), lambda j, i: (i, j)) for _ in out_dtypes],
        out_shape=[jax.ShapeDtypeStruct((m, n), dt) for dt in out_dtypes],
        scratch_shapes=scratch,
        compiler_params=pltpu.CompilerParams(
            dimension_semantics=("arbitrary", "arbitrary"),
            vmem_limit_bytes=_vmem_limit(block_bytes, scratch_bytes)),
        name=name,
    )(*xs, *gain_rows, *[w for w, _, _ in ws], *extras, *row_params)
    return outs[0] if len(outs) == 1 else outs


def _ep_plain(accs, extras, rows, xraw):
    return (accs[0],)


def _ep_residual(accs, extras, rows, xraw):
    return (extras[0] + accs[0],)


def _ep_residual_norm(accs, extras, rows, xraw):
    h = extras[0] + accs[0]
    return h, _rmsnorm_rows(h, rows[0])


def _ep_swiglu(accs, extras, rows, xraw):
    return (_silu(accs[0]) * accs[1],)


def _ple_update(accs, xraw):
    return xraw[0] + _sigmoid(accs[0]) * accs[1]


def _ep_ple_norm(accs, extras, rows, xraw):
    h = _ple_update(accs, xraw)
    return h, _rmsnorm_rows(h, rows[0])


def _ep_ple_final(accs, extras, rows, xraw):
    return (_rmsnorm_rows(_ple_update(accs, xraw), rows[0]),)


def _dt_body(*refs, normed):
    nt = (((1,), (1,)), ((), ()))
    if normed:
        x_ref, g_ref, wt_ref, u_ref, dt_ref, dtt_ref = refs
        u = _rmsnorm_rows(x_ref[...], g_ref[...]).astype(BF16)
        u_ref[...] = u
    else:
        x_ref, wt_ref, dt_ref, dtt_ref = refs
        u = x_ref[...]
    wt = wt_ref[...].astype(BF16)
    dt_ref[...] = lax.dot_general(u, wt, nt, preferred_element_type=F32)
    dtt_ref[...] = lax.dot_general(wt, u, nt, preferred_element_type=F32)


def _dt_proj(x, gain, w_dt_t, tm=1024):
    m, k = x.shape
    normed = gain is not None
    wt_pad = jnp.pad(w_dt_t, ((0, LANES - w_dt_t.shape[0]), (0, 0)))
    row_spec = pl.BlockSpec((tm, k), lambda i: (i, 0))
    in_specs = [row_spec] + ([pl.BlockSpec((1, k), lambda i: (0, 0))] if normed else [])
    in_specs.append(pl.BlockSpec((LANES, k), lambda i: (0, 0)))
    out_specs = [pl.BlockSpec((tm, LANES), lambda i: (i, 0)), pl.BlockSpec((LANES, tm), lambda i: (0, i))]
    out_shape = [jax.ShapeDtypeStruct((m, LANES), F32), jax.ShapeDtypeStruct((LANES, m), F32)]
    if normed:
        out_specs.insert(0, row_spec)
        out_shape.insert(0, jax.ShapeDtypeStruct((m, k), BF16))
    blocks = (_nbytes((tm, k), x.dtype) + _nbytes((LANES, k), F32) + 2 * _nbytes((tm, LANES), F32)
              + (_nbytes((tm, k), BF16) if normed else 0))
    args = (x, gain.reshape(1, k), wt_pad) if normed else (x, wt_pad)
    return pl.pallas_call(
        functools.partial(_dt_body, normed=normed),
        grid=(m // tm,),
        in_specs=in_specs,
        out_specs=out_specs,
        out_shape=out_shape,
        compiler_params=pltpu.CompilerParams(
            dimension_semantics=("arbitrary",), vmem_limit_bytes=_vmem_limit(blocks, 0)),
        name="dt_proj",
    )(*args)


def _split3(v):
    hi = v.astype(BF16)
    r1 = v - hi.astype(F32)
    mid = r1.astype(BF16)
    lo = (r1 - mid.astype(F32)).astype(BF16)
    return hi, mid, lo


def _conv_silu(src_ref, halo_ref, buf_ref, w_ref, b_ref, g):
    q = SSD_CHUNK
    cur = src_ref[...].astype(F32)
    buf_ref[0:SUBLANES, :] = halo_ref[g]
    buf_ref[SUBLANES:SUBLANES + q, :] = cur
    halo_ref[g] = cur[q - SUBLANES:q, :]
    acc = b_ref[...] + w_ref[CONV_WIDTH - 1:CONV_WIDTH, :] * cur
    for k in range(CONV_WIDTH - 1):
        off = SUBLANES - (CONV_WIDTH - 1) + k
        acc = acc + w_ref[k:k + 1, :] * buf_ref[off:off + q, :]
    return _silu(acc)


def _ssd_body(z_ref, x_ref, b_ref, c_ref, dt_ref, dtt_ref,
              cwx_ref, cwb_ref, cwc_ref, cbx_ref, cbb_ref, cbc_ref,
              dtb_r_ref, dtb_c_ref, alog_r_ref, alog_c_ref, d_r_ref, ng_ref, e64_ref,
              y_ref,
              state_ref, hx_ref, hb_ref, hc_ref, xbuf_ref, bbuf_ref, cbuf_ref, dts3_ref, cs_ref, cst_ref):
    t = pl.program_id(1)
    q = SSD_CHUNK
    gw = x_ref.shape[1] // SSM_GROUPS
    hpg = gw // SSM_HEAD_DIM
    row = lax.broadcasted_iota(jnp.int32, (q, q), 0)
    col = lax.broadcasted_iota(jnp.int32, (q, q), 1)
    tril = row >= col

    dts = _softplus(dt_ref[...] + dtb_r_ref[...])
    lo_tri = tril.astype(BF16)
    cs = jnp.zeros((q, LANES), F32)
    for part in _split3(dts * (-jnp.exp(alog_r_ref[...]))):
        cs = cs + jnp.dot(lo_tri, part, preferred_element_type=F32)
    dts3_ref[...] = jnp.concatenate(_split3(dts), axis=1)
    cs_ref[...] = cs
    up_tri = (row <= col).astype(BF16)
    cst = jnp.zeros((LANES, q), F32)
    for part in _split3(_softplus(dtt_ref[...] + dtb_c_ref[...]) * (-jnp.exp(alog_c_ref[...]))):
        cst = cst + jnp.dot(part, up_tri, preferred_element_type=F32)
    cst_ref[...] = cst
    cs3 = jnp.concatenate(_split3(cs), axis=1)

    lane = lax.broadcasted_iota(jnp.int32, (q, LANES), 1)
    d3 = jnp.concatenate(_split3(jnp.broadcast_to(d_r_ref[...], (SUBLANES, LANES))), axis=1)
    for gi in range(SSM_GROUPS):
        xcols = slice(gi * gw, (gi + 1) * gw)
        ncols = slice(gi * SSM_STATE, (gi + 1) * SSM_STATE)

        @pl.when(t == 0)
        def _(gi=gi):
            state_ref[gi] = jnp.zeros(state_ref.shape[1:], F32)
            hx_ref[gi] = jnp.zeros(hx_ref.shape[1:], F32)
            hb_ref[gi] = jnp.zeros(hb_ref.shape[1:], F32)
            hc_ref[gi] = jnp.zeros(hc_ref.shape[1:], F32)

        xs = _conv_silu(x_ref.at[:, xcols], hx_ref, xbuf_ref.at[gi], cwx_ref.at[:, xcols],
                        cbx_ref.at[:, xcols], gi)
        bm = _conv_silu(b_ref.at[:, ncols], hb_ref, bbuf_ref.at[gi], cwb_ref.at[:, ncols],
                        cbb_ref.at[:, ncols], gi).astype(BF16)
        cm = _conv_silu(c_ref.at[:, ncols], hc_ref, cbuf_ref.at[gi], cwc_ref.at[:, ncols],
                        cbc_ref.at[:, ncols], gi).astype(BF16)

        e64 = e64_ref[gi]
        dt_x = jnp.dot(dts3_ref[...], e64, preferred_element_type=F32)
        cs_x = jnp.dot(cs3, e64, preferred_element_type=F32)
        d_x = jnp.dot(d3, e64, preferred_element_type=F32)[0:1, :]
        cs_last = cs_x[q - 1:q, :]

        xdt = xs * dt_x
        xdt_b = xdt.astype(BF16)
        cb = lax.dot_general(cm, bm, (((1,), (1,)), ((), ())), preferred_element_type=F32)
        y_pairs = []
        for j in range(hpg // 2):
            xp = xdt_b[:, j * LANES:(j + 1) * LANES]
            outs = []
            for hh in range(2):
                h = 2 * j + hh
                head = gi * hpg + h
                colb = jnp.broadcast_to(cs_ref[:, head:head + 1], (q, LANES))
                rowb = cst_ref[head:head + 1, :]
                decay = jnp.exp(jnp.where(tril, colb - rowb, -jnp.inf))
                outs.append(jnp.dot((cb * decay).astype(BF16), xp, preferred_element_type=F32))
            y_pairs.append(jnp.where(lane < SSM_HEAD_DIM, outs[0], outs[1]))
        y = jnp.concatenate(y_pairs, axis=1)

        st = state_ref[gi]
        y = y + jnp.dot(cm, st.astype(BF16), preferred_element_type=F32) * jnp.exp(cs_x)
        xsc = (xdt * jnp.exp(cs_last - cs_x)).astype(BF16)
        s_new = lax.dot_general(bm, xsc, (((0,), (0,)), ((), ())), preferred_element_type=F32)
        state_ref[gi] = st * jnp.exp(cs_last) + s_new

        y = y + d_x * xs
        yg = y * _silu(z_ref[:, xcols].astype(F32))
        y_ref[:, xcols] = _rmsnorm_rows(yg, ng_ref[:, xcols]).astype(y_ref.dtype)


def _expansion_matrices(heads_per_group, width):
    k = lax.broadcasted_iota(jnp.int32, (SSM_GROUPS, 3 * LANES, heads_per_group * width), 1) % LANES
    c = lax.broadcasted_iota(jnp.int32, (SSM_GROUPS, 3 * LANES, heads_per_group * width), 2) // width
    gi = lax.broadcasted_iota(jnp.int32, (SSM_GROUPS, 3 * LANES, heads_per_group * width), 0)
    return (k == gi * heads_per_group + c).astype(BF16)


def _ssd_scan(zx, dt, dtt, conv_w, conv_b, dt_bias, a_log, d_skip, norm_g, batch):
    m = zx.shape[0]
    q = SSD_CHUNK
    gn = SSM_GROUPS * SSM_STATE
    d_inner = (zx.shape[1] - 2 * gn) // 2
    gw = d_inner // SSM_GROUPS
    hpg = gw // SSM_HEAD_DIM
    nt = m // batch // q
    x_blk, b_blk, c_blk = 1, 2 * d_inner // gn, 2 * d_inner // gn + 1
    cwb_blk, cwc_blk = d_inner // gn, d_inner // gn + 1

    def pad_row(v):
        return jnp.pad(v, (0, LANES - v.shape[0])).reshape(1, LANES)

    def pad_col(v):
        return jnp.pad(v, (0, LANES - v.shape[0])).reshape(LANES, 1)

    e64 = _expansion_matrices(hpg, SSM_HEAD_DIM)
    rows = lambda blk: (lambda b, t: (b * nt + t, blk))
    cols = lambda blk: (lambda b, t: (0, blk))
    in_specs = [
        pl.BlockSpec((q, d_inner), rows(0)),
        pl.BlockSpec((q, d_inner), rows(x_blk)),
        pl.BlockSpec((q, gn), rows(b_blk)),
        pl.BlockSpec((q, gn), rows(c_blk)),
        pl.BlockSpec((q, LANES), rows(0)),
        pl.BlockSpec((LANES, q), lambda b, t: (0, b * nt + t)),
        pl.BlockSpec((CONV_WIDTH, d_inner), cols(0)),
        pl.BlockSpec((CONV_WIDTH, gn), cols(cwb_blk)),
        pl.BlockSpec((CONV_WIDTH, gn), cols(cwc_blk)),
        pl.BlockSpec((1, d_inner), cols(0)),
        pl.BlockSpec((1, gn), cols(cwb_blk)),
        pl.BlockSpec((1, gn), cols(cwc_blk)),
        pl.BlockSpec((1, LANES), cols(0)),
        pl.BlockSpec((LANES, 1), cols(0)),
        pl.BlockSpec((1, LANES), cols(0)),
        pl.BlockSpec((LANES, 1), cols(0)),
        pl.BlockSpec((1, LANES), cols(0)),
        pl.BlockSpec((1, d_inner), cols(0)),
        pl.BlockSpec(e64.shape, lambda b, t: (0, 0, 0)),
    ]
    scratch_shapes = [
        ((SSM_GROUPS, SSM_STATE, gw), F32),
        ((SSM_GROUPS, SUBLANES, gw), F32),
        ((SSM_GROUPS, SUBLANES, SSM_STATE), F32),
        ((SSM_GROUPS, SUBLANES, SSM_STATE), F32),
        ((SSM_GROUPS, SUBLANES + q, gw), F32),
        ((SSM_GROUPS, SUBLANES + q, SSM_STATE), F32),
        ((SSM_GROUPS, SUBLANES + q, SSM_STATE), F32),
        ((q, 3 * LANES), BF16),
        ((q, LANES), F32),
        ((LANES, q), F32),
    ]
    block_bytes = (3 * _nbytes((q, d_inner), BF16) + 2 * _nbytes((q, gn), BF16) + 2 * _nbytes((q, LANES), F32)
                   + _nbytes(e64.shape, BF16) + (CONV_WIDTH + 2) * _nbytes((SUBLANES, d_inner + 2 * gn), F32))
    return pl.pallas_call(
        _ssd_body,
        grid=(batch, nt),
        in_specs=in_specs,
        out_specs=pl.BlockSpec((q, d_inner), rows(0)),
        out_shape=jax.ShapeDtypeStruct((m, d_inner), BF16),
        scratch_shapes=[pltpu.VMEM(s, d) for s, d in scratch_shapes],
        compiler_params=pltpu.CompilerParams(
            dimension_semantics=("arbitrary", "arbitrary"),
            vmem_limit_bytes=_vmem_limit(block_bytes, sum(_nbytes(s, d) for s, d in scratch_shapes))),
        name="ssd_scan",
    )(zx, zx, zx, zx, dt, dtt, conv_w, conv_w, conv_w, conv_b, conv_b, conv_b,
      pad_row(dt_bias), pad_col(dt_bias), pad_row(a_log), pad_col(a_log), pad_row(d_skip),
      norm_g.reshape(1, d_inner), e64)


def _fold_rows(v, op):
    r, c = v.shape
    v3 = v.reshape(r // SUBLANES, SUBLANES, c)
    return jnp.max(v3, axis=0) if op == "max" else jnp.sum(v3, axis=0)


def _moba_body(q_ref, k_ref, v_ref, o_ref,
               kmean_ref, kaug_ref, vt_ref, qaug_ref, s_ref, m_ref, l_ref, acc_ref):
    qi = pl.program_id(2)
    bs = MOBA_BLOCK
    dh = LANES
    seq = k_ref.shape[0]
    nb = seq // bs
    heads = range(q_ref.shape[1] // dh)
    cbk = ATTN_CHUNK_BLOCKS
    ck = cbk * bs
    c2 = dh ** -0.5 * LOG2_E
    nt = (((1,), (1,)), ((), ()))

    @pl.when(qi == 0)
    def _():
        lane = lax.broadcasted_iota(jnp.int32, (bs, LANES), 1)
        for hh in heads:
            for n in range(nb):
                kb = k_ref[n * bs:(n + 1) * bs, hh * dh:(hh + 1) * dh]
                kmean_ref[hh, n:n + 1, :] = jnp.mean(kb.astype(F32), axis=0, keepdims=True)
                kaug_ref[hh, n * bs:(n + 1) * bs, 0:dh] = kb
                kaug_ref[hh, n * bs:(n + 1) * bs, dh:dh + LANES] = jnp.where(lane == n, 1.0, 0.0).astype(BF16)
                vt_ref[hh, n] = v_ref[n * bs:(n + 1) * bs, hh * dh:(hh + 1) * dh].astype(F32).T.astype(BF16)

    own = pl.multiple_of(qi * bs, bs)
    blk = lax.broadcasted_iota(jnp.int32, (nb, bs), 0)
    key = lax.broadcasted_iota(jnp.int32, (bs, bs), 0)
    qry = lax.broadcasted_iota(jnp.int32, (bs, bs), 1)
    for hh in heads:
        q = q_ref[:, hh * dh:(hh + 1) * dh]
        gate = lax.dot_general(kmean_ref[hh], q.astype(F32), nt,
                               precision=lax.Precision.HIGHEST, preferred_element_type=F32)
        gate = jnp.where(blk < qi, gate, -jnp.inf)
        rank = jnp.zeros((nb, bs), F32)
        for mm in range(nb):
            gm = gate[mm:mm + 1, :]
            beats = jnp.where(gm > gate, 1.0, jnp.where(jnp.logical_and(gm == gate, blk > mm), 1.0, 0.0))
            rank = rank + beats
        chosen = jnp.logical_and(rank < MOBA_TOPK, gate > -jnp.inf)
        bias_t = jnp.where(chosen, 0.0, MASK_BIAS)
        bias_q = jnp.concatenate([bias_t, jnp.zeros((LANES - nb, bs), F32)], axis=0).T
        qaug_ref[hh, :, 0:dh] = q
        qaug_ref[hh, :, dh:dh + LANES] = bias_q.astype(BF16)
        s = lax.dot_general(k_ref[pl.ds(own, bs), hh * dh:(hh + 1) * dh], q, nt,
                            preferred_element_type=F32) * c2
        s = jnp.where(key <= qry, s, MASK_BIAS)
        s_ref[hh, seq:seq + bs, :] = s
        m_ref[hh] = _fold_rows(s, "max")

    for c in range(nb // cbk):
        @pl.when(c * cbk < qi)
        def _(c=c):
            for hh in heads:
                s = lax.dot_general(kaug_ref[hh, c * ck:(c + 1) * ck, :], qaug_ref[hh], nt,
                                    preferred_element_type=F32) * c2
                s_ref[hh, c * ck:(c + 1) * ck, :] = s
                m_ref[hh] = jnp.maximum(m_ref[hh], _fold_rows(s, "max"))

    for hh in heads:
        m = jnp.max(m_ref[hh], axis=0, keepdims=True)
        m_ref[hh] = jnp.broadcast_to(m, (SUBLANES, bs))
        p = jnp.exp2(s_ref[hh, seq:seq + bs, :] - m)
        l_ref[hh] = _fold_rows(p, "sum")
        acc_ref[hh] = jnp.dot(vt_ref[hh, qi], p.astype(BF16), preferred_element_type=F32)

    for c in range(nb // cbk):
        @pl.when(c * cbk < qi)
        def _(c=c):
            for hh in heads:
                p = jnp.exp2(s_ref[hh, c * ck:(c + 1) * ck, :] - m_ref[hh, 0:1, :])
                l_ref[hh] = l_ref[hh] + _fold_rows(p, "sum")
                pb = p.astype(BF16)
                acc = acc_ref[hh]
                for j in range(cbk):
                    acc = acc + jnp.dot(vt_ref[hh, c * cbk + j], pb[j * bs:(j + 1) * bs, :],
                                        preferred_element_type=F32)
                acc_ref[hh] = acc


    for hh in heads:
        l = jnp.sum(l_ref[hh], axis=0, keepdims=True)
        o_ref[:, hh * dh:(hh + 1) * dh] = (acc_ref[hh] / l).T.astype(o_ref.dtype)


def _moba_attention(qkv, batch):
    m = qkv.shape[0]
    seq = m // batch
    dh = qkv.shape[1] // (3 * ATTN_HEADS)
    bs = MOBA_BLOCK
    nq = seq // bs
    hb = ATTN_HEADS_PER_STEP
    ng = ATTN_HEADS // hb
    assert dh == LANES and nq % ATTN_CHUNK_BLOCKS == 0 and nq <= LANES
    block_bytes = 2 * _nbytes((bs, hb * dh), BF16) + 2 * _nbytes((seq, hb * dh), BF16)
    scratch_shapes = [
        ((hb, nq, dh), F32),
        ((hb, seq, dh + LANES), BF16),
        ((hb, nq, dh, bs), BF16),
        ((hb, bs, dh + LANES), BF16),
        ((hb, seq + bs, bs), F32),
        ((hb, SUBLANES, bs), F32),
        ((hb, SUBLANES, bs), F32),
        ((hb, dh, bs), F32),
    ]
    return pl.pallas_call(
        _moba_body,
        grid=(batch, ng, nq),
        in_specs=[pl.BlockSpec((bs, hb * dh), lambda b, h, i: (b * nq + i, h)),
                  pl.BlockSpec((seq, hb * dh), lambda b, h, i: (b, ng + h)),
                  pl.BlockSpec((seq, hb * dh), lambda b, h, i: (b, 2 * ng + h))],
        out_specs=pl.BlockSpec((bs, hb * dh), lambda b, h, i: (b * nq + i, h)),
        out_shape=jax.ShapeDtypeStruct((m, ATTN_HEADS * dh), BF16),
        scratch_shapes=[pltpu.VMEM(s, d) for s, d in scratch_shapes],
        compiler_params=pltpu.CompilerParams(
            dimension_semantics=("arbitrary", "arbitrary", "arbitrary"),
            vmem_limit_bytes=_vmem_limit(block_bytes, sum(_nbytes(s, d) for s, d in scratch_shapes))),
        name="moba_attention",
    )(qkv, qkv, qkv)


def _mamba2_mixer(h, u, gain, w_in_t, layer, conv_w, conv_b, dt_bias, a_log, d_skip, norm_g, w_out,
                  next_gain, batch):
    d_inner = w_out.shape[1]
    n_zx = 2 * d_inner + 2 * SSM_GROUPS * SSM_STATE
    w_dt_t = w_in_t[layer][n_zx:, :]
    if u is None:
        u, dt, dtt = _dt_proj(h, gain, w_dt_t)
    else:
        dt, dtt = _dt_proj(u, None, w_dt_t)
    zx = _fused_matmul([u], [None], [(w_in_t, layer, "f32_t")], [0], [], [], _ep_plain, (BF16,),
                       tm=2048, tn=1024, n_out=n_zx, name="ssm_in_proj")
    y = _ssd_scan(zx, dt, dtt, conv_w, conv_b.reshape(1, -1), dt_bias, a_log, d_skip, norm_g, batch)
    return _fused_matmul([y], [None], [(w_out[layer].astype(BF16), None, "bf16")], [0], [h], [next_gain],
                         _ep_residual_norm, (F32, BF16), tm=512, tn=h.shape[1], single_buffer_weights=True,
                         name="ssm_out_proj")


def _moba_mixer(h, u, w_qkv, w_o, layer, next_gain, batch):
    qkv = _fused_matmul([u], [None], [(w_qkv, layer, "f32")], [0], [], [], _ep_plain, (BF16,),
                        tm=2048, tn=1024, name="attn_qkv")
    o = _moba_attention(qkv, batch)
    d = h.shape[1]
    return _fused_matmul([o], [None], [(w_o[layer].astype(BF16), None, "bf16")], [0], [h], [next_gain],
                         _ep_residual_norm, (F32, BF16), tm=512, tn=d, single_buffer_weights=True,
                         name="attn_out")


def _ffn(h, u, layer, w_gate, w_up, w_down):
    ws = [(w_gate, layer, "f32"), (w_up, layer, "f32")]
    a = _fused_matmul([u], [None], ws, [0, 0], [], [], _ep_swiglu, (BF16,), tm=1024, tn=512, name="ffn_up")
    return _fused_matmul([a], [None], [(w_down, layer, "f32")], [0], [h], [], _ep_residual, (F32,),
                         tm=512, tn=1024, single_buffer_weights=True, name="ffn_down")


def _ple(h, p, layer, gain, w_pgate_l, w_pproj_l, out_gain, last):
    d = h.shape[1]
    ws = [(w_pgate_l.astype(BF16), None, "bf16"), (w_pproj_l.astype(BF16), None, "bf16")]
    if last:
        return _fused_matmul([h, p], [gain, None], ws, [0, 1], [], [out_gain], _ep_ple_final, (F32,),
                             tm=512, tn=d, single_buffer_weights=True, x_layer=layer, name="ple_final")
    return _fused_matmul([h, p], [gain, None], ws, [0, 1], [], [out_gain], _ep_ple_norm, (F32, BF16),
                         tm=512, tn=d, single_buffer_weights=True, x_layer=layer, name="ple")


def kernel(x, p, mix_norm_g, ffn_norm_g, ple_norm_g, ssm_w_in, ssm_conv_w, ssm_conv_b, ssm_dt_bias, ssm_a_log, ssm_d, ssm_norm_g, ssm_w_out, attn_w_qkv, attn_w_o, ffn_w_gate, ffn_w_up, ffn_w_down, ple_w_proj, ple_w_gate, final_norm_g):
    batch, seq, d = x.shape
    m = batch * seq
    depth = p.shape[0]
    ssm_w_in_t = jnp.swapaxes(ssm_w_in, 1, 2)
    h = x.reshape(m, d)
    u = None
    for i in range(depth):
        j = i // 2
        if i % 2 == 0:
            h, u = _mamba2_mixer(h, u, mix_norm_g[i], ssm_w_in_t, j, ssm_conv_w[j], ssm_conv_b[j], ssm_dt_bias[j],
                                 ssm_a_log[j], ssm_d[j], ssm_norm_g[j], ssm_w_out, ffn_norm_g[i], batch)
        else:
            h, u = _moba_mixer(h, u, attn_w_qkv, attn_w_o, j, ffn_norm_g[i], batch)
        h = _ffn(h, u, i, ffn_w_gate, ffn_w_up, ffn_w_down)
        last = i == depth - 1
        out = _ple(h, p.reshape(depth, m, -1), i, ple_norm_g[i], ple_w_gate[i], ple_w_proj[i],
                   final_norm_g if last else mix_norm_g[i + 1], last)
        if not last:
            h, u = out
    return out.reshape(batch, seq, d)
```

```python
import functools

import jax
import jax.numpy as jnp
from jax import lax
from jax.experimental import pallas as pl
from jax.experimental.pallas import tpu as pltpu

NORM_EPS = 1e-6

SSM_HEAD_DIM = 64
SSM_GROUPS = 8
SSM_STATE = 128
CONV_WIDTH = 4
SSD_CHUNK = 128

ATTN_HEADS = 16
MOBA_BLOCK = 256
MOBA_TOPK = 3
ATTN_CHUNK_BLOCKS = 4
ATTN_HEADS_PER_STEP = 4
MASK_BIAS = -1e30
LOG2_E = 1.4426950408889634

LANES = 128
SUBLANES = 8
VMEM_BYTES = 64 * 1024 * 1024
VMEM_UNSCOPED_BYTES = 4 * 1024 * 1024
COMPILER_SCRATCH_BYTES = 12 * 1024 * 1024

F32 = jnp.float32
BF16 = jnp.bfloat16


def _nbytes(shape, dtype):
    n = 1
    for s in shape:
        n *= s
    return n * jnp.dtype(dtype).itemsize


def _vmem_limit(block_bytes, scratch_bytes):
    need = 2 * block_bytes + scratch_bytes + COMPILER_SCRATCH_BYTES
    return int(min(need, VMEM_BYTES - VMEM_UNSCOPED_BYTES))


def _sigmoid(v):
    return 0.5 * jnp.tanh(0.5 * v) + 0.5


def _silu(v):
    hv = 0.5 * v
    return hv * jnp.tanh(hv) + hv


def _softplus(v):
    return jnp.maximum(v, 0.0) + jnp.log1p(jnp.exp(-jnp.abs(v)))


def _rmsnorm_rows(x, g):
    ms = jnp.mean(x * x, axis=-1, keepdims=True)
    return x * lax.rsqrt(ms + NORM_EPS) * g


def _cast_groups(w_x, w_kinds):
    groups = []
    for k, kind in enumerate(w_kinds):
        if kind == "bf16":
            continue
        mate = next((g for g in groups if kind == "f32" and w_kinds[g[0]] == "f32" and w_x[g[0]] == w_x[k]), None)
        if mate is None:
            groups.append([k])
        else:
            mate.append(k)
    return groups


def _mm_body(*refs, normed, w_x, w_kinds, n_extra, n_rows, n_out, epilogue):
    n_x, n_g, n_w = len(normed), sum(normed), len(w_x)
    groups = _cast_groups(w_x, w_kinds)
    refs = list(refs)
    x_refs = [refs.pop(0) for _ in range(n_x)]
    g_refs = [refs.pop(0) for _ in range(n_g)]
    w_refs = [refs.pop(0) for _ in range(n_w)]
    e_refs = [refs.pop(0) for _ in range(n_extra)]
    r_refs = [refs.pop(0) for _ in range(n_rows)]
    o_refs = [refs.pop(0) for _ in range(n_out)]
    wb_refs = [refs.pop(0) for _ in groups]
    tn = o_refs[0].shape[1]

    @pl.when(pl.program_id(1) == 0)
    def _():
        for grp, wb_ref in zip(groups, wb_refs):
            if w_kinds[grp[0]] == "f32_t":
                wb_ref[...] = w_refs[grp[0]][...].astype(BF16)
            else:
                for slot, k in enumerate(grp):
                    wb_ref[:, slot * tn:(slot + 1) * tn] = w_refs[k][...].astype(BF16)

    xraw = [x_ref[...] for x_ref in x_refs]
    xs = []
    for x, is_normed in zip(xraw, normed):
        if is_normed:
            x = _rmsnorm_rows(x, g_refs.pop(0)[...])
        xs.append(x.astype(BF16))
    accs = [None] * n_w
    for grp, wb_ref in zip(groups, wb_refs):
        x = xs[w_x[grp[0]]]
        if w_kinds[grp[0]] == "f32_t":
            accs[grp[0]] = lax.dot_general(x, wb_ref[...], (((1,), (1,)), ((), ())), preferred_element_type=F32)
        else:
            acc = jnp.dot(x, wb_ref[...], preferred_element_type=F32)
            for slot, k in enumerate(grp):
                accs[k] = acc[:, slot * tn:(slot + 1) * tn]
    for k, kind in enumerate(w_kinds):
        if kind == "bf16":
            accs[k] = jnp.dot(xs[w_x[k]], w_refs[k][...], preferred_element_type=F32)
    outs = epilogue(accs, [e_ref[...] for e_ref in e_refs], [r_ref[...] for r_ref in r_refs], xraw)
    for o_ref, o in zip(o_refs, outs):
        o_ref[...] = o.astype(o_ref.dtype)


def _fused_matmul(xs, gains, ws, w_x, extras, rows, epilogue, out_dtypes, tm, tn, n_out=None,
                  single_buffer_weights=False, x_layer=0, name="matmul"):
    m = xs[0].shape[0]
    if n_out is None:
        w0, _, kind0 = ws[0]
        n = w0.shape[1] if kind0 == "f32_t" else w0.shape[-1]
    else:
        n = n_out
    grid = (n // tn, m // tm)
    in_specs, block_bytes, scratch, scratch_bytes = [], 0, [], 0
    for x in xs:
        k = x.shape[-1]
        if x.ndim == 3:
            in_specs.append(pl.BlockSpec((None, tm, k), functools.partial(lambda j, i, l: (l, i, 0), l=x_layer)))
        else:
            in_specs.append(pl.BlockSpec((tm, k), lambda j, i: (i, 0)))
        block_bytes += _nbytes((tm, k), x.dtype)
    gain_rows = [g.reshape(1, -1) for g in gains if g is not None]
    for g in gain_rows:
        in_specs.append(pl.BlockSpec(g.shape, lambda j, i: (0, 0)))
    w_mode = dict(pipeline_mode=pl.Buffered(1)) if single_buffer_weights else {}
    buffers = 1 if single_buffer_weights else 2
    for w, layer, kind in ws:
        if kind == "bf16":
            k = w.shape[0]
            in_specs.append(pl.BlockSpec((k, tn), lambda j, i: (0, j), **w_mode))
            block_bytes += _nbytes((k, tn), BF16) * buffers // 2
            continue
        if kind == "f32_t":
            k = w.shape[2]
            shape = (tn, k)
            in_specs.append(pl.BlockSpec((None, tn, k), functools.partial(lambda j, i, l: (l, j, 0), l=layer),
                                         **w_mode))
        else:
            k = w.shape[1]
            shape = (k, tn)
            in_specs.append(pl.BlockSpec((None, k, tn), functools.partial(lambda j, i, l: (l, 0, j), l=layer),
                                         **w_mode))
        block_bytes += _nbytes(shape, F32) * buffers // 2
    for grp in _cast_groups(w_x, [kind for _, _, kind in ws]):
        w, _, kind = ws[grp[0]]
        shape = (tn, w.shape[2]) if kind == "f32_t" else (w.shape[1], len(grp) * tn)
        scratch.append(pltpu.VMEM(shape, BF16))
        scratch_bytes += _nbytes(shape, BF16)
    for e in extras:
        in_specs.append(pl.BlockSpec((tm, tn), lambda j, i: (i, j)))
        block_bytes += _nbytes((tm, tn), e.dtype)
    row_params = [r.reshape(1, -1) for r in rows]
    for r in row_params:
        in_specs.append(pl.BlockSpec((1, tn), lambda j, i: (0, j)))
    for dt in out_dtypes:
        block_bytes += _nbytes((tm, tn), dt)
    body = functools.partial(_mm_body, normed=tuple(g is not None for g in gains), w_x=tuple(w_x),
                             w_kinds=tuple(kind for _, _, kind in ws), n_extra=len(extras),
                             n_rows=len(rows), n_out=len(out_dtypes), epilogue=epilogue)
    outs = pl.pallas_call(
        body,
        grid=grid,
        in_specs=in_specs,
        out_specs=[pl.BlockSpec((tm, tn), lambda j, i: (i, j)) for _ in out_dtypes],
        out_shape=[jax.ShapeDtypeStruct((m, n), dt) for dt in out_dtypes],
        scratch_shapes=scratch,
        compiler_params=pltpu.CompilerParams(
            dimension_semantics=("arbitrary", "arbitrary"),
            vmem_limit_bytes=_vmem_limit(block_bytes, scratch_bytes)),
        name=name,
    )(*xs, *gain_rows, *[w for w, _, _ in ws], *extras, *row_params)
    return outs[0] if len(outs) == 1 else outs


def _ep_plain(accs, extras, rows, xraw):
    return (accs[0],)


def _ep_residual(accs, extras, rows, xraw):
    return (extras[0] + accs[0],)


def _ep_residual_norm(accs, extras, rows, xraw):
    h = extras[0] + accs[0]
    return h, _rmsnorm_rows(h, rows[0])


def _ep_swiglu(accs, extras, rows, xraw):
    return (_silu(accs[0]) * accs[1],)


def _ple_update(accs, xraw):
    return xraw[0] + _sigmoid(accs[0]) * accs[1]


def _ep_ple_norm(accs, extras, rows, xraw):
    h = _ple_update(accs, xraw)
    return h, _rmsnorm_rows(h, rows[0])


def _ep_ple_final(accs, extras, rows, xraw):
    return (_rmsnorm_rows(_ple_update(accs, xraw), rows[0]),)


def _dt_body(*refs, normed):
    nt = (((1,), (1,)), ((), ()))
    if normed:
        x_ref, g_ref, wt_ref, u_ref, dt_ref, dtt_ref = refs
        u = _rmsnorm_rows(x_ref[...], g_ref[...]).astype(BF16)
        u_ref[...] = u
    else:
        x_ref, wt_ref, dt_ref, dtt_ref = refs
        u = x_ref[...]
    wt = wt_ref[...].astype(BF16)
    dt_ref[...] = lax.dot_general(u, wt, nt, preferred_element_type=F32)
    dtt_ref[...] = lax.dot_general(wt, u, nt, preferred_element_type=F32)


def _dt_proj(x, gain, w_dt_t, tm=1024):
    m, k = x.shape
    normed = gain is not None
    wt_pad = jnp.pad(w_dt_t, ((0, LANES - w_dt_t.shape[0]), (0, 0)))
    row_spec = pl.BlockSpec((tm, k), lambda i: (i, 0))
    in_specs = [row_spec] + ([pl.BlockSpec((1, k), lambda i: (0, 0))] if normed else [])
    in_specs.append(pl.BlockSpec((LANES, k), lambda i: (0, 0)))
    out_specs = [pl.BlockSpec((tm, LANES), lambda i: (i, 0)), pl.BlockSpec((LANES, tm), lambda i: (0, i))]
    out_shape = [jax.ShapeDtypeStruct((m, LANES), F32), jax.ShapeDtypeStruct((LANES, m), F32)]
    if normed:
        out_specs.insert(0, row_spec)
        out_shape.insert(0, jax.ShapeDtypeStruct((m, k), BF16))
    blocks = (_nbytes((tm, k), x.dtype) + _nbytes((LANES, k), F32) + 2 * _nbytes((tm, LANES), F32)
              + (_nbytes((tm, k), BF16) if normed else 0))
    args = (x, gain.reshape(1, k), wt_pad) if normed else (x, wt_pad)
    return pl.pallas_call(
        functools.partial(_dt_body, normed=normed),
        grid=(m // tm,),
        in_specs=in_specs,
        out_specs=out_specs,
        out_shape=out_shape,
        compiler_params=pltpu.CompilerParams(
            dimension_semantics=("arbitrary",), vmem_limit_bytes=_vmem_limit(blocks, 0)),
        name="dt_proj",
    )(*args)


def _split3(v):
    hi = v.astype(BF16)
    r1 = v - hi.astype(F32)
    mid = r1.astype(BF16)
    lo = (r1 - mid.astype(F32)).astype(BF16)
    return hi, mid, lo


def _conv_silu(src_ref, halo_ref, buf_ref, w_ref, b_ref, g):
    q = SSD_CHUNK
    cur = src_ref[...].astype(F32)
    buf_ref[0:SUBLANES, :] = halo_ref[g]
    buf_ref[SUBLANES:SUBLANES + q, :] = cur
    halo_ref[g] = cur[q - SUBLANES:q, :]
    acc = b_ref[...] + w_ref[CONV_WIDTH - 1:CONV_WIDTH, :] * cur
    for k in range(CONV_WIDTH - 1):
        off = SUBLANES - (CONV_WIDTH - 1) + k
        acc = acc + w_ref[k:k + 1, :] * buf_ref[off:off + q, :]
    return _silu(acc)


def _ssd_body(z_ref, x_ref, b_ref, c_ref, dt_ref, dtt_ref,
              cwx_ref, cwb_ref, cwc_ref, cbx_ref, cbb_ref, cbc_ref,
              dtb_r_ref, dtb_c_ref, alog_r_ref, alog_c_ref, d_r_ref, ng_ref, e64_ref,
              y_ref,
              state_ref, hx_ref, hb_ref, hc_ref, xbuf_ref, bbuf_ref, cbuf_ref, dts3_ref, cs_ref, cst_ref):
    t = pl.program_id(1)
    q = SSD_CHUNK
    gw = x_ref.shape[1] // SSM_GROUPS
    hpg = gw // SSM_HEAD_DIM
    row = lax.broadcasted_iota(jnp.int32, (q, q), 0)
    col = lax.broadcasted_iota(jnp.int32, (q, q), 1)
    tril = row >= col

    dts = _softplus(dt_ref[...] + dtb_r_ref[...])
    lo_tri = tril.astype(BF16)
    cs = jnp.zeros((q, LANES), F32)
    for part in _split3(dts * (-jnp.exp(alog_r_ref[...]))):
        cs = cs + jnp.dot(lo_tri, part, preferred_element_type=F32)
    dts3_ref[...] = jnp.concatenate(_split3(dts), axis=1)
    cs_ref[...] = cs
    up_tri = (row <= col).astype(BF16)
    cst = jnp.zeros((LANES, q), F32)
    for part in _split3(_softplus(dtt_ref[...] + dtb_c_ref[...]) * (-jnp.exp(alog_c_ref[...]))):
        cst = cst + jnp.dot(part, up_tri, preferred_element_type=F32)
    cst_ref[...] = cst
    cs3 = jnp.concatenate(_split3(cs), axis=1)

    lane = lax.broadcasted_iota(jnp.int32, (q, LANES), 1)
    d3 = jnp.concatenate(_split3(jnp.broadcast_to(d_r_ref[...], (SUBLANES, LANES))), axis=1)
    for gi in range(SSM_GROUPS):
        xcols = slice(gi * gw, (gi + 1) * gw)
        ncols = slice(gi * SSM_STATE, (gi + 1) * SSM_STATE)

        @pl.when(t == 0)
        def _(gi=gi):
            state_ref[gi] = jnp.zeros(state_ref.shape[1:], F32)
            hx_ref[gi] = jnp.zeros(hx_ref.shape[1:], F32)
            hb_ref[gi] = jnp.zeros(hb_ref.shape[1:], F32)
            hc_ref[gi] = jnp.zeros(hc_ref.shape[1:], F32)

        xs = _conv_silu(x_ref.at[:, xcols], hx_ref, xbuf_ref.at[gi], cwx_ref.at[:, xcols],
                        cbx_ref.at[:, xcols], gi)
        bm = _conv_silu(b_ref.at[:, ncols], hb_ref, bbuf_ref.at[gi], cwb_ref.at[:, ncols],
                        cbb_ref.at[:, ncols], gi).astype(BF16)
        cm = _conv_silu(c_ref.at[:, ncols], hc_ref, cbuf_ref.at[gi], cwc_ref.at[:, ncols],
                        cbc_ref.at[:, ncols], gi).astype(BF16)

        e64 = e64_ref[gi]
        dt_x = jnp.dot(dts3_ref[...], e64, preferred_element_type=F32)
        cs_x = jnp.dot(cs3, e64, preferred_element_type=F32)
        d_x = jnp.dot(d3, e64, preferred_element_type=F32)[0:1, :]
        cs_last = cs_x[q - 1:q, :]

        xdt = xs * dt_x
        xdt_b = xdt.astype(BF16)
        cb = lax.dot_general(cm, bm, (((1,), (1,)), ((), ())), preferred_element_type=F32)
        y_pairs = []
        for j in range(hpg // 2):
            xp = xdt_b[:, j * LANES:(j + 1) * LANES]
            outs = []
            for hh in range(2):
                h = 2 * j + hh
                head = gi * hpg + h
                colb = jnp.broadcast_to(cs_ref[:, head:head + 1], (q, LANES))
                rowb = cst_ref[head:head + 1, :]
                decay = jnp.exp(jnp.where(tril, colb - rowb, -jnp.inf))
                outs.append(jnp.dot((cb * decay).astype(BF16), xp, preferred_element_type=F32))
            y_pairs.append(jnp.where(lane < SSM_HEAD_DIM, outs[0], outs[1]))
        y = jnp.concatenate(y_pairs, axis=1)

        st = state_ref[gi]
        y = y + jnp.dot(cm, st.astype(BF16), preferred_element_type=F32) * jnp.exp(cs_x)
        xsc = (xdt * jnp.exp(cs_last - cs_x)).astype(BF16)
        s_new = lax.dot_general(bm, xsc, (((0,), (0,)), ((), ())), preferred_element_type=F32)
        state_ref[gi] = st * jnp.exp(cs_last) + s_new

        y = y + d_x * xs
        yg = y * _silu(z_ref[:, xcols].astype(F32))
        y_ref[:, xcols] = _rmsnorm_rows(yg, ng_ref[:, xcols]).astype(y_ref.dtype)


def _expansion_matrices(heads_per_group, width):
    k = lax.broadcasted_iota(jnp.int32, (SSM_GROUPS, 3 * LANES, heads_per_group * width), 1) % LANES
    c = lax.broadcasted_iota(jnp.int32, (SSM_GROUPS, 3 * LANES, heads_per_group * width), 2) // width
    gi = lax.broadcasted_iota(jnp.int32, (SSM_GROUPS, 3 * LANES, heads_per_group * width), 0)
    return (k == gi * heads_per_group + c).astype(BF16)


def _ssd_scan(zx, dt, dtt, conv_w, conv_b, dt_bias, a_log, d_skip, norm_g, batch):
    m = zx.shape[0]
    q = SSD_CHUNK
    gn = SSM_GROUPS * SSM_STATE
    d_inner = (zx.shape[1] - 2 * gn) // 2
    gw = d_inner // SSM_GROUPS
    hpg = gw // SSM_HEAD_DIM
    nt = m // batch // q
    x_blk, b_blk, c_blk = 1, 2 * d_inner // gn, 2 * d_inner // gn + 1
    cwb_blk, cwc_blk = d_inner // gn, d_inner // gn + 1

    def pad_row(v):
        return jnp.pad(v, (0, LANES - v.shape[0])).reshape(1, LANES)

    def pad_col(v):
        return jnp.pad(v, (0, LANES - v.shape[0])).reshape(LANES, 1)

    e64 = _expansion_matrices(hpg, SSM_HEAD_DIM)
    rows = lambda blk: (lambda b, t: (b * nt + t, blk))
    cols = lambda blk: (lambda b, t: (0, blk))
    in_specs = [
        pl.BlockSpec((q, d_inner), rows(0)),
        pl.BlockSpec((q, d_inner), rows(x_blk)),
        pl.BlockSpec((q, gn), rows(b_blk)),
        pl.BlockSpec((q, gn), rows(c_blk)),
        pl.BlockSpec((q, LANES), rows(0)),
        pl.BlockSpec((LANES, q), lambda b, t: (0, b * nt + t)),
        pl.BlockSpec((CONV_WIDTH, d_inner), cols(0)),
        pl.BlockSpec((CONV_WIDTH, gn), cols(cwb_blk)),
        pl.BlockSpec((CONV_WIDTH, gn), cols(cwc_blk)),
        pl.BlockSpec((1, d_inner), cols(0)),
        pl.BlockSpec((1, gn), cols(cwb_blk)),
        pl.BlockSpec((1, gn), cols(cwc_blk)),
        pl.BlockSpec((1, LANES), cols(0)),
        pl.BlockSpec((LANES, 1), cols(0)),
        pl.BlockSpec((1, LANES), cols(0)),
        pl.BlockSpec((LANES, 1), cols(0)),
        pl.BlockSpec((1, LANES), cols(0)),
        pl.BlockSpec((1, d_inner), cols(0)),
        pl.BlockSpec(e64.shape, lambda b, t: (0, 0, 0)),
    ]
    scratch_shapes = [
        ((SSM_GROUPS, SSM_STATE, gw), F32),
        ((SSM_GROUPS, SUBLANES, gw), F32),
        ((SSM_GROUPS, SUBLANES, SSM_STATE), F32),
        ((SSM_GROUPS, SUBLANES, SSM_STATE), F32),
        ((SSM_GROUPS, SUBLANES + q, gw), F32),
        ((SSM_GROUPS, SUBLANES + q, SSM_STATE), F32),
        ((SSM_GROUPS, SUBLANES + q, SSM_STATE), F32),
        ((q, 3 * LANES), BF16),
        ((q, LANES), F32),
        ((LANES, q), F32),
    ]
    block_bytes = (3 * _nbytes((q, d_inner), BF16) + 2 * _nbytes((q, gn), BF16) + 2 * _nbytes((q, LANES), F32)
                   + _nbytes(e64.shape, BF16) + (CONV_WIDTH + 2) * _nbytes((SUBLANES, d_inner + 2 * gn), F32))
    return pl.pallas_call(
        _ssd_body,
        grid=(batch, nt),
        in_specs=in_specs,
        out_specs=pl.BlockSpec((q, d_inner), rows(0)),
        out_shape=jax.ShapeDtypeStruct((m, d_inner), BF16),
        scratch_shapes=[pltpu.VMEM(s, d) for s, d in scratch_shapes],
        compiler_params=pltpu.CompilerParams(
            dimension_semantics=("arbitrary", "arbitrary"),
            vmem_limit_bytes=_vmem_limit(block_bytes, sum(_nbytes(s, d) for s, d in scratch_shapes))),
        name="ssd_scan",
    )(zx, zx, zx, zx, dt, dtt, conv_w, conv_w, conv_w, conv_b, conv_b, conv_b,
      pad_row(dt_bias), pad_col(dt_bias), pad_row(a_log), pad_col(a_log), pad_row(d_skip),
      norm_g.reshape(1, d_inner), e64)


def _fold_rows(v, op):
    r, c = v.shape
    v3 = v.reshape(r // SUBLANES, SUBLANES, c)
    return jnp.max(v3, axis=0) if op == "max" else jnp.sum(v3, axis=0)


def _moba_body(q_ref, k_ref, v_ref, o_ref,
               kmean_ref, kaug_ref, vt_ref, qaug_ref, s_ref, m_ref, l_ref, acc_ref):
    qi = pl.program_id(2)
    bs = MOBA_BLOCK
    dh = LANES
    seq = k_ref.shape[0]
    nb = seq // bs
    heads = range(q_ref.shape[1] // dh)
    cbk = ATTN_CHUNK_BLOCKS
    ck = cbk * bs
    c2 = dh ** -0.5 * LOG2_E
    nt = (((1,), (1,)), ((), ()))

    @pl.when(qi == 0)
    def _():
        lane = lax.broadcasted_iota(jnp.int32, (bs, LANES), 1)
        for hh in heads:
            for n in range(nb):
                kb = k_ref[n * bs:(n + 1) * bs, hh * dh:(hh + 1) * dh]
                kmean_ref[hh, n:n + 1, :] = jnp.mean(kb.astype(F32), axis=0, keepdims=True)
                kaug_ref[hh, n * bs:(n + 1) * bs, 0:dh] = kb
                kaug_ref[hh, n * bs:(n + 1) * bs, dh:dh + LANES] = jnp.where(lane == n, 1.0, 0.0).astype(BF16)
                vt_ref[hh, n] = v_ref[n * bs:(n + 1) * bs, hh * dh:(hh + 1) * dh].astype(F32).T.astype(BF16)

    own = pl.multiple_of(qi * bs, bs)
    blk = lax.broadcasted_iota(jnp.int32, (nb, bs), 0)
    key = lax.broadcasted_iota(jnp.int32, (bs, bs), 0)
    qry = lax.broadcasted_iota(jnp.int32, (bs, bs), 1)
    for hh in heads:
        q = q_ref[:, hh * dh:(hh + 1) * dh]
        gate = lax.dot_general(kmean_ref[hh], q.astype(F32), nt,
                               precision=lax.Precision.HIGHEST, preferred_element_type=F32)
        gate = jnp.where(blk < qi, gate, -jnp.inf)
        rank = jnp.zeros((nb, bs), F32)
        for mm in range(nb):
            gm = gate[mm:mm + 1, :]
            beats = jnp.where(gm > gate, 1.0, jnp.where(jnp.logical_and(gm == gate, blk > mm), 1.0, 0.0))
            rank = rank + beats
        chosen = jnp.logical_and(rank < MOBA_TOPK, gate > -jnp.inf)
        bias_t = jnp.where(chosen, 0.0, MASK_BIAS)
        bias_q = jnp.concatenate([bias_t, jnp.zeros((LANES - nb, bs), F32)], axis=0).T
        qaug_ref[hh, :, 0:dh] = q
        qaug_ref[hh, :, dh:dh + LANES] = bias_q.astype(BF16)
        s = lax.dot_general(k_ref[pl.ds(own, bs), hh * dh:(hh + 1) * dh], q, nt,
                            preferred_element_type=F32) * c2
        s = jnp.where(key <= qry, s, MASK_BIAS)
        s_ref[hh, seq:seq + bs, :] = s
        m_ref[hh] = _fold_rows(s, "max")

    for c in range(nb // cbk):
        @pl.when(c * cbk < qi)
        def _(c=c):
            for hh in heads:
                s = lax.dot_general(kaug_ref[hh, c * ck:(c + 1) * ck, :], qaug_ref[hh], nt,
                                    preferred_element_type=F32) * c2
                s_ref[hh, c * ck:(c + 1) * ck, :] = s
                m_ref[hh] = jnp.maximum(m_ref[hh], _fold_rows(s, "max"))

    for hh in heads:
        m = jnp.max(m_ref[hh], axis=0, keepdims=True)
        m_ref[hh] = jnp.broadcast_to(m, (SUBLANES, bs))
        p = jnp.exp2(s_ref[hh, seq:seq + bs, :] - m)
        l_ref[hh] = _fold_rows(p, "sum")
        acc_ref[hh] = jnp.dot(vt_ref[hh, qi], p.astype(BF16), preferred_element_type=F32)

    for c in range(nb // cbk):
        @pl.when(c * cbk < qi)
        def _(c=c):
            for hh in heads:
                p = jnp.exp2(s_ref[hh, c * ck:(c + 1) * ck, :] - m_ref[hh, 0:1, :])
                l_ref[hh] = l_ref[hh] + _fold_rows(p, "sum")
                pb = p.astype(BF16)
                acc = acc_ref[hh]
                for j in range(cbk):
                    acc = acc + jnp.dot(vt_ref[hh, c * cbk + j], pb[j * bs:(j + 1) * bs, :],
                                        preferred_element_type=F32)
                acc_ref[hh] = acc


    for hh in heads:
        l = jnp.sum(l_ref[hh], axis=0, keepdims=True)
        o_ref[:, hh * dh:(hh + 1) * dh] = (acc_ref[hh] / l).T.astype(o_ref.dtype)


def _moba_attention(qkv, batch):
    m = qkv.shape[0]
    seq = m // batch
    dh = qkv.shape[1] // (3 * ATTN_HEADS)
    bs = MOBA_BLOCK
    nq = seq // bs
    hb = ATTN_HEADS_PER_STEP
    ng = ATTN_HEADS // hb
    assert dh == LANES and nq % ATTN_CHUNK_BLOCKS == 0 and nq <= LANES
    block_bytes = 2 * _nbytes((bs, hb * dh), BF16) + 2 * _nbytes((seq, hb * dh), BF16)
    scratch_shapes = [
        ((hb, nq, dh), F32),
        ((hb, seq, dh + LANES), BF16),
        ((hb, nq, dh, bs), BF16),
        ((hb, bs, dh + LANES), BF16),
        ((hb, seq + bs, bs), F32),
        ((hb, SUBLANES, bs), F32),
        ((hb, SUBLANES, bs), F32),
        ((hb, dh, bs), F32),
    ]
    return pl.pallas_call(
        _moba_body,
        grid=(batch, ng, nq),
        in_specs=[pl.BlockSpec((bs, hb * dh), lambda b, h, i: (b * nq + i, h)),
                  pl.BlockSpec((seq, hb * dh), lambda b, h, i: (b, ng + h)),
                  pl.BlockSpec((seq, hb * dh), lambda b, h, i: (b, 2 * ng + h))],
        out_specs=pl.BlockSpec((bs, hb * dh), lambda b, h, i: (b * nq + i, h)),
        out_shape=jax.ShapeDtypeStruct((m, ATTN_HEADS * dh), BF16),
        scratch_shapes=[pltpu.VMEM(s, d) for s, d in scratch_shapes],
        compiler_params=pltpu.CompilerParams(
            dimension_semantics=("arbitrary", "arbitrary", "arbitrary"),
            vmem_limit_bytes=_vmem_limit(block_bytes, sum(_nbytes(s, d) for s, d in scratch_shapes))),
        name="moba_attention",
    )(qkv, qkv, qkv)


def _mamba2_mixer(h, u, gain, w_in_t, layer, conv_w, conv_b, dt_bias, a_log, d_skip, norm_g, w_out,
                  next_gain, batch):
    d_inner = w_out.shape[1]
    n_zx = 2 * d_inner + 2 * SSM_GROUPS * SSM_STATE
    w_dt_t = w_in_t[layer][n_zx:, :]
    if u is None:
        u, dt, dtt = _dt_proj(h, gain, w_dt_t)
    else:
        dt, dtt = _dt_proj(u, None, w_dt_t)
    zx = _fused_matmul([u], [None], [(w_in_t, layer, "f32_t")], [0], [], [], _ep_plain, (BF16,),
                       tm=2048, tn=1024, n_out=n_zx, name="ssm_in_proj")
    y = _ssd_scan(zx, dt, dtt, conv_w, conv_b.reshape(1, -1), dt_bias, a_log, d_skip, norm_g, batch)
    return _fused_matmul([y], [None], [(w_out[layer].astype(BF16), None, "bf16")], [0], [h], [next_gain],
                         _ep_residual_norm, (F32, BF16), tm=512, tn=h.shape[1], single_buffer_weights=True,
                         name="ssm_out_proj")


def _moba_mixer(h, u, w_qkv, w_o, layer, next_gain, batch):
    qkv = _fused_matmul([u], [None], [(w_qkv, layer, "f32")], [0], [], [], _ep_plain, (BF16,),
                        tm=2048, tn=1024, name="attn_qkv")
    o = _moba_attention(qkv, batch)
    d = h.shape[1]
    return _fused_matmul([o], [None], [(w_o[layer].astype(BF16), None, "bf16")], [0], [h], [next_gain],
                         _ep_residual_norm, (F32, BF16), tm=512, tn=d, single_buffer_weights=True,
                         name="attn_out")


def _ffn(h, u, layer, w_gate, w_up, w_down):
    ws = [(w_gate, layer, "f32"), (w_up, layer, "f32")]
    a = _fused_matmul([u], [None], ws, [0, 0], [], [], _ep_swiglu, (BF16,), tm=1024, tn=512, name="ffn_up")
    return _fused_matmul([a], [None], [(w_down, layer, "f32")], [0], [h], [], _ep_residual, (F32,),
                         tm=512, tn=1024, single_buffer_weights=True, name="ffn_down")


def _ple(h, p, layer, gain, w_pgate_l, w_pproj_l, out_gain, last):
    d = h.shape[1]
    ws = [(w_pgate_l.astype(BF16), None, "bf16"), (w_pproj_l.astype(BF16), None, "bf16")]
    if last:
        return _fused_matmul([h, p], [gain, None], ws, [0, 1], [], [out_gain], _ep_ple_final, (F32,),
                             tm=512, tn=d, single_buffer_weights=True, x_layer=layer, name="ple_final")
    return _fused_matmul([h, p], [gain, None], ws, [0, 1], [], [out_gain], _ep_ple_norm, (F32, BF16),
                         tm=512, tn=d, single_buffer_weights=True, x_layer=layer, name="ple")


def kernel(x, p, mix_norm_g, ffn_norm_g, ple_norm_g, ssm_w_in, ssm_conv_w, ssm_conv_b, ssm_dt_bias, ssm_a_log, ssm_d, ssm_norm_g, ssm_w_out, attn_w_qkv, attn_w_o, ffn_w_gate, ffn_w_up, ffn_w_down, ple_w_proj, ple_w_gate, final_norm_g):
    batch, seq, d = x.shape
    m = batch * seq
    depth = p.shape[0]
    ssm_w_in_t = jnp.swapaxes(ssm_w_in, 1, 2)
    h = x.reshape(m, d)
    u = None
    for i in range(depth):
        j = i // 2
        if i % 2 == 0:
            h, u = _mamba2_mixer(h, u, mix_norm_g[i], ssm_w_in_t, j, ssm_conv_w[j], ssm_conv_b[j], ssm_dt_bias[j],
                                 ssm_a_log[j], ssm_d[j], ssm_norm_g[j], ssm_w_out, ffn_norm_g[i], batch)
        else:
            h, u = _moba_mixer(h, u, attn_w_qkv, attn_w_o, j, ffn_norm_g[i], batch)
        h = _ffn(h, u, i, ffn_w_gate, ffn_w_up, ffn_w_down)
        last = i == depth - 1
        out = _ple(h, p.reshape(depth, m, -1), i, ple_norm_g[i], ple_w_gate[i], ple_w_proj[i],
                   final_norm_g if last else mix_norm_g[i + 1], last)
        if not last:
            h, u = out
    return out.reshape(batch, seq, d)
```

```python
import functools

import jax
import jax.numpy as jnp
from jax import lax
from jax.experimental import pallas as pl
from jax.experimental.pallas import tpu as pltpu

NORM_EPS = 1e-6

SSM_HEAD_DIM = 64
SSM_GROUPS = 8
SSM_STATE = 128
CONV_WIDTH = 4
SSD_CHUNK = 128

ATTN_HEADS = 16
MOBA_BLOCK = 256
MOBA_TOPK = 3
ATTN_CHUNK_BLOCKS = 4
ATTN_HEADS_PER_STEP = 4
MASK_BIAS = -1e30
LOG2_E = 1.4426950408889634

LANES = 128
SUBLANES = 8
VMEM_BYTES = 64 * 1024 * 1024
VMEM_UNSCOPED_BYTES = 4 * 1024 * 1024
COMPILER_SCRATCH_BYTES = 12 * 1024 * 1024

F32 = jnp.float32
BF16 = jnp.bfloat16


def _nbytes(shape, dtype):
    n = 1
    for s in shape:
        n *= s
    return n * jnp.dtype(dtype).itemsize


def _vmem_limit(block_bytes, scratch_bytes):
    need = 2 * block_bytes + scratch_bytes + COMPILER_SCRATCH_BYTES
    return int(min(need, VMEM_BYTES - VMEM_UNSCOPED_BYTES))


def _sigmoid(v):
    return 0.5 * jnp.tanh(0.5 * v) + 0.5


def _silu(v):
    hv = 0.5 * v
    return hv * jnp.tanh(hv) + hv


def _softplus(v):
    return jnp.maximum(v, 0.0) + jnp.log1p(jnp.exp(-jnp.abs(v)))


def _rmsnorm_rows(x, g):
    ms = jnp.mean(x * x, axis=-1, keepdims=True)
    return x * lax.rsqrt(ms + NORM_EPS) * g


def _cast_groups(w_x, w_kinds):
    groups = []
    for k, kind in enumerate(w_kinds):
        if kind == "bf16":
            continue
        mate = next((g for g in groups if kind == "f32" and w_kinds[g[0]] == "f32" and w_x[g[0]] == w_x[k]), None)
        if mate is None:
            groups.append([k])
        else:
            mate.append(k)
    return groups


def _mm_body(*refs, normed, w_x, w_kinds, n_extra, n_rows, n_out, epilogue):
    n_x, n_g, n_w = len(normed), sum(normed), len(w_x)
    groups = _cast_groups(w_x, w_kinds)
    refs = list(refs)
    x_refs = [refs.pop(0) for _ in range(n_x)]
    g_refs = [refs.pop(0) for _ in range(n_g)]
    w_refs = [refs.pop(0) for _ in range(n_w)]
    e_refs = [refs.pop(0) for _ in range(n_extra)]
    r_refs = [refs.pop(0) for _ in range(n_rows)]
    o_refs = [refs.pop(0) for _ in range(n_out)]
    wb_refs = [refs.pop(0) for _ in groups]
    tn = o_refs[0].shape[1]

    @pl.when(pl.program_id(1) == 0)
    def _():
        for grp, wb_ref in zip(groups, wb_refs):
            if w_kinds[grp[0]] == "f32_t":
                wb_ref[...] = w_refs[grp[0]][...].astype(BF16)
            else:
                for slot, k in enumerate(grp):
                    wb_ref[:, slot * tn:(slot + 1) * tn] = w_refs[k][...].astype(BF16)

    xraw = [x_ref[...] for x_ref in x_refs]
    xs = []
    for x, is_normed in zip(xraw, normed):
        if is_normed:
            x = _rmsnorm_rows(x, g_refs.pop(0)[...])
        xs.append(x.astype(BF16))
    accs = [None] * n_w
    for grp, wb_ref in zip(groups, wb_refs):
        x = xs[w_x[grp[0]]]
        if w_kinds[grp[0]] == "f32_t":
            accs[grp[0]] = lax.dot_general(x, wb_ref[...], (((1,), (1,)), ((), ())), preferred_element_type=F32)
        else:
            acc = jnp.dot(x, wb_ref[...], preferred_element_type=F32)
            for slot, k in enumerate(grp):
                accs[k] = acc[:, slot * tn:(slot + 1) * tn]
    for k, kind in enumerate(w_kinds):
        if kind == "bf16":
            accs[k] = jnp.dot(xs[w_x[k]], w_refs[k][...], preferred_element_type=F32)
    outs = epilogue(accs, [e_ref[...] for e_ref in e_refs], [r_ref[...] for r_ref in r_refs], xraw)
    for o_ref, o in zip(o_refs, outs):
        o_ref[...] = o.astype(o_ref.dtype)


def _fused_matmul(xs, gains, ws, w_x, extras, rows, epilogue, out_dtypes, tm, tn, n_out=None,
                  single_buffer_weights=False, x_layer=0, name="matmul"):
    m = xs[0].shape[0]
    if n_out is None:
        w0, _, kind0 = ws[0]
        n = w0.shape[1] if kind0 == "f32_t" else w0.shape[-1]
    else:
        n = n_out
    grid = (n // tn, m // tm)
    in_specs, block_bytes, scratch, scratch_bytes = [], 0, [], 0
    for x in xs:
        k = x.shape[-1]
        if x.ndim == 3:
            in_specs.append(pl.BlockSpec((None, tm, k), functools.partial(lambda j, i, l: (l, i, 0), l=x_layer)))
        else:
            in_specs.append(pl.BlockSpec((tm, k), lambda j, i: (i, 0)))
        block_bytes += _nbytes((tm, k), x.dtype)
    gain_rows = [g.reshape(1, -1) for g in gains if g is not None]
    for g in gain_rows:
        in_specs.append(pl.BlockSpec(g.shape, lambda j, i: (0, 0)))
    w_mode = dict(pipeline_mode=pl.Buffered(1)) if single_buffer_weights else {}
    buffers = 1 if single_buffer_weights else 2
    for w, layer, kind in ws:
        if kind == "bf16":
            k = w.shape[0]
            in_specs.append(pl.BlockSpec((k, tn), lambda j, i: (0, j), **w_mode))
            block_bytes += _nbytes((k, tn), BF16) * buffers // 2
            continue
        if kind == "f32_t":
            k = w.shape[2]
            shape = (tn, k)
            in_specs.append(pl.BlockSpec((None, tn, k), functools.partial(lambda j, i, l: (l, j, 0), l=layer),
                                         **w_mode))
        else:
            k = w.shape[1]
            shape = (k, tn)
            in_specs.append(pl.BlockSpec((None, k, tn), functools.partial(lambda j, i, l: (l, 0, j), l=layer),
                                         **w_mode))
        block_bytes += _nbytes(shape, F32) * buffers // 2
    for grp in _cast_groups(w_x, [kind for _, _, kind in ws]):
        w, _, kind = ws[grp[0]]
        shape = (tn, w.shape[2]) if kind == "f32_t" else (w.shape[1], len(grp) * tn)
        scratch.append(pltpu.VMEM(shape, BF16))
        scratch_bytes += _nbytes(shape, BF16)
    for e in extras:
        in_specs.append(pl.BlockSpec((tm, tn), lambda j, i: (i, j)))
        block_bytes += _nbytes((tm, tn), e.dtype)
    row_params = [r.reshape(1, -1) for r in rows]
    for r in row_params:
        in_specs.append(pl.BlockSpec((1, tn), lambda j, i: (0, j)))
    for dt in out_dtypes:
        block_bytes += _nbytes((tm, tn), dt)
    body = functools.partial(_mm_body, normed=tuple(g is not None for g in gains), w_x=tuple(w_x),
                             w_kinds=tuple(kind for _, _, kind in ws), n_extra=len(extras),
                             n_rows=len(rows), n_out=len(out_dtypes), epilogue=epilogue)
    outs = pl.pallas_call(
        body,
        grid=grid,
        in_specs=in_specs,
        out_specs=[pl.BlockSpec((tm, tn), lambda j, i: (i, j)) for _ in out_dtypes],
        out_shape=[jax.ShapeDtypeStruct((m, n), dt) for dt in out_dtypes],
        scratch_shapes=scratch,
        compiler_params=pltpu.CompilerParams(
            dimension_semantics=("arbitrary", "arbitrary"),
            vmem_limit_bytes=_vmem_limit(block_bytes, scratch_bytes)),
        name=name,
    )(*xs, *gain_rows, *[w for w, _, _ in ws], *extras, *row_params)
    return outs[0] if len(outs) == 1 else outs


def _ep_plain(accs, extras, rows, xraw):
    return (accs[0],)


def _ep_residual(accs, extras, rows, xraw):
    return (extras[0] + accs[0],)


def _ep_residual_norm(accs, extras, rows, xraw):
    h = extras[0] + accs[0]
    return h, _rmsnorm_rows(h, rows[0])


def _ep_swiglu(accs, extras, rows, xraw):
    return (_silu(accs[0]) * accs[1],)


def _ple_update(accs, xraw):
    return xraw[0] + _sigmoid(accs[0]) * accs[1]


def _ep_ple_norm(accs, extras, rows, xraw):
    h = _ple_update(accs, xraw)
    return h, _rmsnorm_rows(h, rows[0])


def _ep_ple_final(accs, extras, rows, xraw):
    return (_rmsnorm_rows(_ple_update(accs, xraw), rows[0]),)


def _dt_body(*refs, normed):
    nt = (((1,), (1,)), ((), ()))
    if normed:
        x_ref, g_ref, wt_ref, u_ref, dt_ref, dtt_ref = refs
        u = _rmsnorm_rows(x_ref[...], g_ref[...]).astype(BF16)
        u_ref[...] = u
    else:
        x_ref, wt_ref, dt_ref, dtt_ref = refs
        u = x_ref[...]
    wt = wt_ref[...].astype(BF16)
    dt_ref[...] = lax.dot_general(u, wt, nt, preferred_element_type=F32)
    dtt_ref[...] = lax.dot_general(wt, u, nt, preferred_element_type=F32)


def _dt_proj(x, gain, w_dt_t, tm=1024):
    m, k = x.shape
    normed = gain is not None
    wt_pad = jnp.pad(w_dt_t, ((0, LANES - w_dt_t.shape[0]), (0, 0)))
    row_spec = pl.BlockSpec((tm, k), lambda i: (i, 0))
    in_specs = [row_spec] + ([pl.BlockSpec((1, k), lambda i: (0, 0))] if normed else [])
    in_specs.append(pl.BlockSpec((LANES, k), lambda i: (0, 0)))
    out_specs = [pl.BlockSpec((tm, LANES), lambda i: (i, 0)), pl.BlockSpec((LANES, tm), lambda i: (0, i))]
    out_shape = [jax.ShapeDtypeStruct((m, LANES), F32), jax.ShapeDtypeStruct((LANES, m), F32)]
    if normed:
        out_specs.insert(0, row_spec)
        out_shape.insert(0, jax.ShapeDtypeStruct((m, k), BF16))
    blocks = (_nbytes((tm, k), x.dtype) + _nbytes((LANES, k), F32) + 2 * _nbytes((tm, LANES), F32)
              + (_nbytes((tm, k), BF16) if normed else 0))
    args = (x, gain.reshape(1, k), wt_pad) if normed else (x, wt_pad)
    return pl.pallas_call(
        functools.partial(_dt_body, normed=normed),
        grid=(m // tm,),
        in_specs=in_specs,
        out_specs=out_specs,
        out_shape=out_shape,
        compiler_params=pltpu.CompilerParams(
            dimension_semantics=("arbitrary",), vmem_limit_bytes=_vmem_limit(blocks, 0)),
        name="dt_proj",
    )(*args)


def _split3(v):
    hi = v.astype(BF16)
    r1 = v - hi.astype(F32)
    mid = r1.astype(BF16)
    lo = (r1 - mid.astype(F32)).astype(BF16)
    return hi, mid, lo


def _conv_silu(src_ref, halo_ref, buf_ref, w_ref, b_ref, g):
    q = SSD_CHUNK
    cur = src_ref[...].astype(F32)
    buf_ref[0:SUBLANES, :] = halo_ref[g]
    buf_ref[SUBLANES:SUBLANES + q, :] = cur
    halo_ref[g] = cur[q - SUBLANES:q, :]
    acc = b_ref[...] + w_ref[CONV_WIDTH - 1:CONV_WIDTH, :] * cur
    for k in range(CONV_WIDTH - 1):
        off = SUBLANES - (CONV_WIDTH - 1) + k
        acc = acc + w_ref[k:k + 1, :] * buf_ref[off:off + q, :]
    return _silu(acc)


def _ssd_body(z_ref, x_ref, b_ref, c_ref, dt_ref, dtt_ref,
              cwx_ref, cwb_ref, cwc_ref, cbx_ref, cbb_ref, cbc_ref,
              dtb_r_ref, dtb_c_ref, alog_r_ref, alog_c_ref, d_r_ref, ng_ref, e64_ref,
              y_ref,
              state_ref, hx_ref, hb_ref, hc_ref, xbuf_ref, bbuf_ref, cbuf_ref, dts3_ref, cs_ref, cst_ref):
    t = pl.program_id(1)
    q = SSD_CHUNK
    gw = x_ref.shape[1] // SSM_GROUPS
    hpg = gw // SSM_HEAD_DIM
    row = lax.broadcasted_iota(jnp.int32, (q, q), 0)
    col = lax.broadcasted_iota(jnp.int32, (q, q), 1)
    tril = row >= col

    @pl.when(t == 0)
    def _():
        state_ref[...] = jnp.zeros(state_ref.shape, F32)
        hx_ref[...] = jnp.zeros(hx_ref.shape, F32)
        hb_ref[...] = jnp.zeros(hb_ref.shape, F32)
        hc_ref[...] = jnp.zeros(hc_ref.shape, F32)

    dts = _softplus(dt_ref[...] + dtb_r_ref[...])
    lo_tri = tril.astype(BF16)
    cs = jnp.zeros((q, LANES), F32)
    for part in _split3(dts * (-jnp.exp(alog_r_ref[...]))):
        cs = cs + jnp.dot(lo_tri, part, preferred_element_type=F32)
    dts3_ref[...] = jnp.concatenate(_split3(dts), axis=1)
    cs_ref[...] = cs
    up_tri = (row <= col).astype(BF16)
    cst = jnp.zeros((LANES, q), F32)
    for part in _split3(_softplus(dtt_ref[...] + dtb_c_ref[...]) * (-jnp.exp(alog_c_ref[...]))):
        cst = cst + jnp.dot(part, up_tri, preferred_element_type=F32)
    cst_ref[...] = cst
    cs3 = jnp.concatenate(_split3(cs), axis=1)

    lane = lax.broadcasted_iota(jnp.int32, (q, LANES), 1)
    d3 = jnp.concatenate(_split3(jnp.broadcast_to(d_r_ref[...], (SUBLANES, LANES))), axis=1)
    for gi in range(SSM_GROUPS):
        xcols = slice(gi * gw, (gi + 1) * gw)
        ncols = slice(gi * SSM_STATE, (gi + 1) * SSM_STATE)

        xs = _conv_silu(x_ref.at[:, xcols], hx_ref, xbuf_ref.at[gi], cwx_ref.at[:, xcols],
                        cbx_ref.at[:, xcols], gi)
        bm = _conv_silu(b_ref.at[:, ncols], hb_ref, bbuf_ref.at[gi], cwb_ref.at[:, ncols],
                        cbb_ref.at[:, ncols], gi).astype(BF16)
        cm = _conv_silu(c_ref.at[:, ncols], hc_ref, cbuf_ref.at[gi], cwc_ref.at[:, ncols],
                        cbc_ref.at[:, ncols], gi).astype(BF16)

        e64 = e64_ref[gi]
        dt_x = jnp.dot(dts3_ref[...], e64, preferred_element_type=F32)
        cs_x = jnp.dot(cs3, e64, preferred_element_type=F32)
        d_x = jnp.dot(d3, e64, preferred_element_type=F32)[0:1, :]
        cs_last = cs_x[q - 1:q, :]

        xdt = xs * dt_x
        xdt_b = xdt.astype(BF16)
        cb = lax.dot_general(cm, bm, (((1,), (1,)), ((), ())), preferred_element_type=F32)
        y_pairs = []
        for j in range(hpg // 2):
            xp = xdt_b[:, j * LANES:(j + 1) * LANES]
            outs = []
            for hh in range(2):
                h = 2 * j + hh
                head = gi * hpg + h
                colb = jnp.broadcast_to(cs_ref[:, head:head + 1], (q, LANES))
                rowb = cst_ref[head:head + 1, :]
                decay = jnp.exp(jnp.where(tril, colb - rowb, -jnp.inf))
                outs.append(jnp.dot((cb * decay).astype(BF16), xp, preferred_element_type=F32))
            y_pairs.append(jnp.where(lane < SSM_HEAD_DIM, outs[0], outs[1]))
        y = jnp.concatenate(y_pairs, axis=1)

        st = state_ref[gi]
        y = y + jnp.dot(cm, st.astype(BF16), preferred_element_type=F32) * jnp.exp(cs_x)
        xsc = (xdt * jnp.exp(cs_last - cs_x)).astype(BF16)
        s_new = lax.dot_general(bm, xsc, (((0,), (0,)), ((), ())), preferred_element_type=F32)
        state_ref[gi] = st * jnp.exp(cs_last) + s_new

        y = y + d_x * xs
        yg = y * _silu(z_ref[:, xcols].astype(F32))
        y_ref[:, xcols] = _rmsnorm_rows(yg, ng_ref[:, xcols]).astype(y_ref.dtype)


def _expansion_matrices(heads_per_group, width):
    k = lax.broadcasted_iota(jnp.int32, (SSM_GROUPS, 3 * LANES, heads_per_group * width), 1) % LANES
    c = lax.broadcasted_iota(jnp.int32, (SSM_GROUPS, 3 * LANES, heads_per_group * width), 2) // width
    gi = lax.broadcasted_iota(jnp.int32, (SSM_GROUPS, 3 * LANES, heads_per_group * width), 0)
    return (k == gi * heads_per_group + c).astype(BF16)


def _ssd_scan(zx, dt, dtt, conv_w, conv_b, dt_bias, a_log, d_skip, norm_g, batch):
    m = zx.shape[0]
    q = SSD_CHUNK
    gn = SSM_GROUPS * SSM_STATE
    d_inner = (zx.shape[1] - 2 * gn) // 2
    gw = d_inner // SSM_GROUPS
    hpg = gw // SSM_HEAD_DIM
    nt = m // batch // q
    x_blk, b_blk, c_blk = 1, 2 * d_inner // gn, 2 * d_inner // gn + 1
    cwb_blk, cwc_blk = d_inner // gn, d_inner // gn + 1

    def pad_row(v):
        return jnp.pad(v, (0, LANES - v.shape[0])).reshape(1, LANES)

    def pad_col(v):
        return jnp.pad(v, (0, LANES - v.shape[0])).reshape(LANES, 1)

    e64 = _expansion_matrices(hpg, SSM_HEAD_DIM)
    rows = lambda blk: (lambda b, t: (b * nt + t, blk))
    cols = lambda blk: (lambda b, t: (0, blk))
    in_specs = [
        pl.BlockSpec((q, d_inner), rows(0)),
        pl.BlockSpec((q, d_inner), rows(x_blk)),
        pl.BlockSpec((q, gn), rows(b_blk)),
        pl.BlockSpec((q, gn), rows(c_blk)),
        pl.BlockSpec((q, LANES), rows(0)),
        pl.BlockSpec((LANES, q), lambda b, t: (0, b * nt + t)),
        pl.BlockSpec((CONV_WIDTH, d_inner), cols(0)),
        pl.BlockSpec((CONV_WIDTH, gn), cols(cwb_blk)),
        pl.BlockSpec((CONV_WIDTH, gn), cols(cwc_blk)),
        pl.BlockSpec((1, d_inner), cols(0)),
        pl.BlockSpec((1, gn), cols(cwb_blk)),
        pl.BlockSpec((1, gn), cols(cwc_blk)),
        pl.BlockSpec((1, LANES), cols(0)),
        pl.BlockSpec((LANES, 1), cols(0)),
        pl.BlockSpec((1, LANES), cols(0)),
        pl.BlockSpec((LANES, 1), cols(0)),
        pl.BlockSpec((1, LANES), cols(0)),
        pl.BlockSpec((1, d_inner), cols(0)),
        pl.BlockSpec(e64.shape, lambda b, t: (0, 0, 0)),
    ]
    scratch_shapes = [
        ((SSM_GROUPS, SSM_STATE, gw), F32),
        ((SSM_GROUPS, SUBLANES, gw), F32),
        ((SSM_GROUPS, SUBLANES, SSM_STATE), F32),
        ((SSM_GROUPS, SUBLANES, SSM_STATE), F32),
        ((SSM_GROUPS, SUBLANES + q, gw), F32),
        ((SSM_GROUPS, SUBLANES + q, SSM_STATE), F32),
        ((SSM_GROUPS, SUBLANES + q, SSM_STATE), F32),
        ((q, 3 * LANES), BF16),
        ((q, LANES), F32),
        ((LANES, q), F32),
    ]
    block_bytes = (3 * _nbytes((q, d_inner), BF16) + 2 * _nbytes((q, gn), BF16) + 2 * _nbytes((q, LANES), F32)
                   + _nbytes(e64.shape, BF16) + (CONV_WIDTH + 2) * _nbytes((SUBLANES, d_inner + 2 * gn), F32))
    return pl.pallas_call(
        _ssd_body,
        grid=(batch, nt),
        in_specs=in_specs,
        out_specs=pl.BlockSpec((q, d_inner), rows(0)),
        out_shape=jax.ShapeDtypeStruct((m, d_inner), BF16),
        scratch_shapes=[pltpu.VMEM(s, d) for s, d in scratch_shapes],
        compiler_params=pltpu.CompilerParams(
            dimension_semantics=("arbitrary", "arbitrary"),
            vmem_limit_bytes=_vmem_limit(block_bytes, sum(_nbytes(s, d) for s, d in scratch_shapes))),
        name="ssd_scan",
    )(zx, zx, zx, zx, dt, dtt, conv_w, conv_w, conv_w, conv_b, conv_b, conv_b,
      pad_row(dt_bias), pad_col(dt_bias), pad_row(a_log), pad_col(a_log), pad_row(d_skip),
      norm_g.reshape(1, d_inner), e64)


def _fold_rows(v, op):
    r, c = v.shape
    v3 = v.reshape(r // SUBLANES, SUBLANES, c)
    return jnp.max(v3, axis=0) if op == "max" else jnp.sum(v3, axis=0)


def _moba_body(q_ref, k_ref, v_ref, o_ref,
               kmean_ref, kaug_ref, vt_ref, qaug_ref, s_ref, m_ref, l_ref, acc_ref):
    qi = pl.program_id(2)
    bs = MOBA_BLOCK
    dh = LANES
    seq = k_ref.shape[0]
    nb = seq // bs
    heads = range(q_ref.shape[1] // dh)
    cbk = ATTN_CHUNK_BLOCKS
    ck = cbk * bs
    c2 = dh ** -0.5 * LOG2_E
    nt = (((1,), (1,)), ((), ()))

    @pl.when(qi == 0)
    def _():
        lane = lax.broadcasted_iota(jnp.int32, (bs, LANES), 1)
        for hh in heads:
            for n in range(nb):
                kb = k_ref[n * bs:(n + 1) * bs, hh * dh:(hh + 1) * dh]
                kmean_ref[hh, n:n + 1, :] = jnp.mean(kb.astype(F32), axis=0, keepdims=True)
                kaug_ref[hh, n * bs:(n + 1) * bs, 0:dh] = kb
                kaug_ref[hh, n * bs:(n + 1) * bs, dh:dh + LANES] = jnp.where(lane == n, 1.0, 0.0).astype(BF16)
                vt_ref[hh, n] = v_ref[n * bs:(n + 1) * bs, hh * dh:(hh + 1) * dh].astype(F32).T.astype(BF16)

    own = pl.multiple_of(qi * bs, bs)
    blk = lax.broadcasted_iota(jnp.int32, (nb, bs), 0)
    key = lax.broadcasted_iota(jnp.int32, (bs, bs), 0)
    qry = lax.broadcasted_iota(jnp.int32, (bs, bs), 1)
    for hh in heads:
        q = q_ref[:, hh * dh:(hh + 1) * dh]
        gate = lax.dot_general(kmean_ref[hh], q.astype(F32), nt,
                               precision=lax.Precision.HIGHEST, preferred_element_type=F32)
        gate = jnp.where(blk < qi, gate, -jnp.inf)
        rank = jnp.zeros((nb, bs), F32)
        for mm in range(nb):
            gm = gate[mm:mm + 1, :]
            beats = jnp.where(gm > gate, 1.0, jnp.where(jnp.logical_and(gm == gate, blk > mm), 1.0, 0.0))
            rank = rank + beats
        chosen = jnp.logical_and(rank < MOBA_TOPK, gate > -jnp.inf)
        bias_t = jnp.where(chosen, 0.0, MASK_BIAS)
        bias_q = jnp.concatenate([bias_t, jnp.zeros((LANES - nb, bs), F32)], axis=0).T
        qaug_ref[hh, :, 0:dh] = q
        qaug_ref[hh, :, dh:dh + LANES] = bias_q.astype(BF16)
        s = lax.dot_general(k_ref[pl.ds(own, bs), hh * dh:(hh + 1) * dh], q, nt,
                            preferred_element_type=F32) * c2
        s = jnp.where(key <= qry, s, MASK_BIAS)
        s_ref[hh, seq:seq + bs, :] = s
        m_ref[hh] = _fold_rows(s, "max")

    for c in range(nb // cbk):
        @pl.when(c * cbk < qi)
        def _(c=c):
            for hh in heads:
                s = lax.dot_general(kaug_ref[hh, c * ck:(c + 1) * ck, :], qaug_ref[hh], nt,
                                    preferred_element_type=F32) * c2
                s_ref[hh, c * ck:(c + 1) * ck, :] = s
                m_ref[hh] = jnp.maximum(m_ref[hh], _fold_rows(s, "max"))

    for hh in heads:
        m = jnp.max(m_ref[hh], axis=0, keepdims=True)
        m_ref[hh] = jnp.broadcast_to(m, (SUBLANES, bs))
        p = jnp.exp2(s_ref[hh, seq:seq + bs, :] - m)
        l_ref[hh] = _fold_rows(p, "sum")
        acc_ref[hh] = jnp.dot(vt_ref[hh, qi], p.astype(BF16), preferred_element_type=F32)

    for c in range(nb // cbk):
        @pl.when(c * cbk < qi)
        def _(c=c):
            for hh in heads:
                p = jnp.exp2(s_ref[hh, c * ck:(c + 1) * ck, :] - m_ref[hh, 0:1, :])
                l_ref[hh] = l_ref[hh] + _fold_rows(p, "sum")
                pb = p.astype(BF16)
                acc = acc_ref[hh]
                for j in range(cbk):
                    acc = acc + jnp.dot(vt_ref[hh, c * cbk + j], pb[j * bs:(j + 1) * bs, :],
                                        preferred_element_type=F32)
                acc_ref[hh] = acc


    for hh in heads:
        l = jnp.sum(l_ref[hh], axis=0, keepdims=True)
        o_ref[:, hh * dh:(hh + 1) * dh] = (acc_ref[hh] / l).T.astype(o_ref.dtype)


def _moba_attention(qkv, batch):
    m = qkv.shape[0]
    seq = m // batch
    dh = qkv.shape[1] // (3 * ATTN_HEADS)
    bs = MOBA_BLOCK
    nq = seq // bs
    hb = ATTN_HEADS_PER_STEP
    ng = ATTN_HEADS // hb
    assert dh == LANES and nq % ATTN_CHUNK_BLOCKS == 0 and nq <= LANES
    block_bytes = 2 * _nbytes((bs, hb * dh), BF16) + 2 * _nbytes((seq, hb * dh), BF16)
    scratch_shapes = [
        ((hb, nq, dh), F32),
        ((hb, seq, dh + LANES), BF16),
        ((hb, nq, dh, bs), BF16),
        ((hb, bs, dh + LANES), BF16),
        ((hb, seq + bs, bs), F32),
        ((hb, SUBLANES, bs), F32),
        ((hb, SUBLANES, bs), F32),
        ((hb, dh, bs), F32),
    ]
    return pl.pallas_call(
        _moba_body,
        grid=(batch, ng, nq),
        in_specs=[pl.BlockSpec((bs, hb * dh), lambda b, h, i: (b * nq + i, h)),
                  pl.BlockSpec((seq, hb * dh), lambda b, h, i: (b, ng + h)),
                  pl.BlockSpec((seq, hb * dh), lambda b, h, i: (b, 2 * ng + h))],
        out_specs=pl.BlockSpec((bs, hb * dh), lambda b, h, i: (b * nq + i, h)),
        out_shape=jax.ShapeDtypeStruct((m, ATTN_HEADS * dh), BF16),
        scratch_shapes=[pltpu.VMEM(s, d) for s, d in scratch_shapes],
        compiler_params=pltpu.CompilerParams(
            dimension_semantics=("arbitrary", "arbitrary", "arbitrary"),
            vmem_limit_bytes=_vmem_limit(block_bytes, sum(_nbytes(s, d) for s, d in scratch_shapes))),
        name="moba_attention",
    )(qkv, qkv, qkv)


def _mamba2_mixer(h, u, gain, w_in_t, layer, conv_w, conv_b, dt_bias, a_log, d_skip, norm_g, w_out,
                  next_gain, batch):
    d_inner = w_out.shape[1]
    n_zx = 2 * d_inner + 2 * SSM_GROUPS * SSM_STATE
    w_dt_t = w_in_t[layer][n_zx:, :]
    if u is None:
        u, dt, dtt = _dt_proj(h, gain, w_dt_t)
    else:
        dt, dtt = _dt_proj(u, None, w_dt_t)
    zx = _fused_matmul([u], [None], [(w_in_t, layer, "f32_t")], [0], [], [], _ep_plain, (BF16,),
                       tm=2048, tn=1024, n_out=n_zx, name="ssm_in_proj")
    y = _ssd_scan(zx, dt, dtt, conv_w, conv_b.reshape(1, -1), dt_bias, a_log, d_skip, norm_g, batch)
    return _fused_matmul([y], [None], [(w_out[layer].astype(BF16), None, "bf16")], [0], [h], [next_gain],
                         _ep_residual_norm, (F32, BF16), tm=512, tn=h.shape[1], single_buffer_weights=True,
                         name="ssm_out_proj")


def _moba_mixer(h, u, w_qkv, w_o, layer, next_gain, batch):
    qkv = _fused_matmul([u], [None], [(w_qkv, layer, "f32")], [0], [], [], _ep_plain, (BF16,),
                        tm=2048, tn=1024, name="attn_qkv")
    o = _moba_attention(qkv, batch)
    d = h.shape[1]
    return _fused_matmul([o], [None], [(w_o[layer].astype(BF16), None, "bf16")], [0], [h], [next_gain],
                         _ep_residual_norm, (F32, BF16), tm=512, tn=d, single_buffer_weights=True,
                         name="attn_out")


def _ffn(h, u, layer, w_gate, w_up, w_down):
    ws = [(w_gate, layer, "f32"), (w_up, layer, "f32")]
    a = _fused_matmul([u], [None], ws, [0, 0], [], [], _ep_swiglu, (BF16,), tm=1024, tn=512, name="ffn_up")
    return _fused_matmul([a], [None], [(w_down, layer, "f32")], [0], [h], [], _ep_residual, (F32,),
                         tm=512, tn=1024, single_buffer_weights=True, name="ffn_down")


def _ple(h, p, layer, gain, w_pgate_l, w_pproj_l, out_gain, last):
    d = h.shape[1]
    ws = [(w_pgate_l.astype(BF16), None, "bf16"), (w_pproj_l.astype(BF16), None, "bf16")]
    if last:
        return _fused_matmul([h, p], [gain, None], ws, [0, 1], [], [out_gain], _ep_ple_final, (F32,),
                             tm=512, tn=d, single_buffer_weights=True, x_layer=layer, name="ple_final")
    return _fused_matmul([h, p], [gain, None], ws, [0, 1], [], [out_gain], _ep_ple_norm, (F32, BF16),
                         tm=512, tn=d, single_buffer_weights=True, x_layer=layer, name="ple")


def kernel(x, p, mix_norm_g, ffn_norm_g, ple_norm_g, ssm_w_in, ssm_conv_w, ssm_conv_b, ssm_dt_bias, ssm_a_log, ssm_d, ssm_norm_g, ssm_w_out, attn_w_qkv, attn_w_o, ffn_w_gate, ffn_w_up, ffn_w_down, ple_w_proj, ple_w_gate, final_norm_g):
    batch, seq, d = x.shape
    m = batch * seq
    depth = p.shape[0]
    ssm_w_in_t = jnp.swapaxes(ssm_w_in, 1, 2)
    h = x.reshape(m, d)
    u = None
    for i in range(depth):
        j = i // 2
        if i % 2 == 0:
            h, u = _mamba2_mixer(h, u, mix_norm_g[i], ssm_w_in_t, j, ssm_conv_w[j], ssm_conv_b[j], ssm_dt_bias[j],
                                 ssm_a_log[j], ssm_d[j], ssm_norm_g[j], ssm_w_out, ffn_norm_g[i], batch)
        else:
            h, u = _moba_mixer(h, u, attn_w_qkv, attn_w_o, j, ffn_norm_g[i], batch)
        h = _ffn(h, u, i, ffn_w_gate, ffn_w_up, ffn_w_down)
        last = i == depth - 1
        out = _ple(h, p.reshape(depth, m, -1), i, ple_norm_g[i], ple_w_gate[i], ple_w_proj[i],
                   final_norm_g if last else mix_norm_g[i + 1], last)
        if not last:
            h, u = out
    return out.reshape(batch, seq, d)
```

```python
import functools

import jax
import jax.numpy as jnp
from jax import lax
from jax.experimental import pallas as pl
from jax.experimental.pallas import tpu as pltpu

NORM_EPS = 1e-6

SSM_HEAD_DIM = 64
SSM_GROUPS = 8
SSM_STATE = 128
CONV_WIDTH = 4
SSD_CHUNK = 128

ATTN_HEADS = 16
MOBA_BLOCK = 256
MOBA_TOPK = 3
ATTN_CHUNK_BLOCKS = 4
ATTN_HEADS_PER_STEP = 4
MASK_BIAS = -1e30
LOG2_E = 1.4426950408889634

LANES = 128
SUBLANES = 8
VMEM_BYTES = 64 * 1024 * 1024
VMEM_UNSCOPED_BYTES = 4 * 1024 * 1024
COMPILER_SCRATCH_BYTES = 12 * 1024 * 1024

F32 = jnp.float32
BF16 = jnp.bfloat16


def _nbytes(shape, dtype):
    n = 1
    for s in shape:
        n *= s
    return n * jnp.dtype(dtype).itemsize


def _vmem_limit(block_bytes, scratch_bytes):
    need = 2 * block_bytes + scratch_bytes + COMPILER_SCRATCH_BYTES
    return int(min(need, VMEM_BYTES - VMEM_UNSCOPED_BYTES))


def _sigmoid(v):
    return 0.5 * jnp.tanh(0.5 * v) + 0.5


def _silu(v):
    hv = 0.5 * v
    return hv * jnp.tanh(hv) + hv


def _softplus(v):
    return jnp.maximum(v, 0.0) + jnp.log1p(jnp.exp(-jnp.abs(v)))


def _rmsnorm_rows(x, g):
    ms = jnp.mean(x * x, axis=-1, keepdims=True)
    return x * lax.rsqrt(ms + NORM_EPS) * g


def _cast_groups(w_x, w_kinds):
    groups = []
    for k, kind in enumerate(w_kinds):
        if kind == "bf16":
            continue
        mate = next((g for g in groups if kind == "f32" and w_kinds[g[0]] == "f32" and w_x[g[0]] == w_x[k]), None)
        if mate is None:
            groups.append([k])
        else:
            mate.append(k)
    return groups


def _mm_body(*refs, normed, w_x, w_kinds, n_extra, n_rows, n_out, epilogue):
    n_x, n_g, n_w = len(normed), sum(normed), len(w_x)
    groups = _cast_groups(w_x, w_kinds)
    refs = list(refs)
    x_refs = [refs.pop(0) for _ in range(n_x)]
    g_refs = [refs.pop(0) for _ in range(n_g)]
    w_refs = [refs.pop(0) for _ in range(n_w)]
    e_refs = [refs.pop(0) for _ in range(n_extra)]
    r_refs = [refs.pop(0) for _ in range(n_rows)]
    o_refs = [refs.pop(0) for _ in range(n_out)]
    wb_refs = [refs.pop(0) for _ in groups]
    tn = o_refs[0].shape[1]

    @pl.when(pl.program_id(1) == 0)
    def _():
        for grp, wb_ref in zip(groups, wb_refs):
            if w_kinds[grp[0]] == "f32_t":
                wb_ref[...] = w_refs[grp[0]][...].astype(BF16)
            else:
                for slot, k in enumerate(grp):
                    wb_ref[:, slot * tn:(slot + 1) * tn] = w_refs[k][...].astype(BF16)

    xraw = [x_ref[...] for x_ref in x_refs]
    xs = []
    for x, is_normed in zip(xraw, normed):
        if is_normed:
            x = _rmsnorm_rows(x, g_refs.pop(0)[...])
        xs.append(x.astype(BF16))
    accs = [None] * n_w
    for grp, wb_ref in zip(groups, wb_refs):
        x = xs[w_x[grp[0]]]
        if w_kinds[grp[0]] == "f32_t":
            accs[grp[0]] = lax.dot_general(x, wb_ref[...], (((1,), (1,)), ((), ())), preferred_element_type=F32)
        else:
            acc = jnp.dot(x, wb_ref[...], preferred_element_type=F32)
            for slot, k in enumerate(grp):
                accs[k] = acc[:, slot * tn:(slot + 1) * tn]
    for k, kind in enumerate(w_kinds):
        if kind == "bf16":
            accs[k] = jnp.dot(xs[w_x[k]], w_refs[k][...], preferred_element_type=F32)
    outs = epilogue(accs, [e_ref[...] for e_ref in e_refs], [r_ref[...] for r_ref in r_refs], xraw)
    for o_ref, o in zip(o_refs, outs):
        o_ref[...] = o.astype(o_ref.dtype)


def _fused_matmul(xs, gains, ws, w_x, extras, rows, epilogue, out_dtypes, tm, tn, n_out=None,
                  single_buffer_weights=False, x_layer=0, name="matmul"):
    m = xs[0].shape[0]
    if n_out is None:
        w0, _, kind0 = ws[0]
        n = w0.shape[1] if kind0 == "f32_t" else w0.shape[-1]
    else:
        n = n_out
    grid = (n // tn, m // tm)
    in_specs, block_bytes, scratch, scratch_bytes = [], 0, [], 0
    for x in xs:
        k = x.shape[-1]
        if x.ndim == 3:
            in_specs.append(pl.BlockSpec((None, tm, k), functools.partial(lambda j, i, l: (l, i, 0), l=x_layer)))
        else:
            in_specs.append(pl.BlockSpec((tm, k), lambda j, i: (i, 0)))
        block_bytes += _nbytes((tm, k), x.dtype)
    gain_rows = [g.reshape(1, -1) for g in gains if g is not None]
    for g in gain_rows:
        in_specs.append(pl.BlockSpec(g.shape, lambda j, i: (0, 0)))
    w_mode = dict(pipeline_mode=pl.Buffered(1)) if single_buffer_weights else {}
    buffers = 1 if single_buffer_weights else 2
    for w, layer, kind in ws:
        if kind == "bf16":
            k = w.shape[0]
            in_specs.append(pl.BlockSpec((k, tn), lambda j, i: (0, j), **w_mode))
            block_bytes += _nbytes((k, tn), BF16) * buffers // 2
            continue
        if kind == "f32_t":
            k = w.shape[2]
            shape = (tn, k)
            in_specs.append(pl.BlockSpec((None, tn, k), functools.partial(lambda j, i, l: (l, j, 0), l=layer),
                                         **w_mode))
        else:
            k = w.shape[1]
            shape = (k, tn)
            in_specs.append(pl.BlockSpec((None, k, tn), functools.partial(lambda j, i, l: (l, 0, j), l=layer),
                                         **w_mode))
        block_bytes += _nbytes(shape, F32) * buffers // 2
    for grp in _cast_groups(w_x, [kind for _, _, kind in ws]):
        w, _, kind = ws[grp[0]]
        shape = (tn, w.shape[2]) if kind == "f32_t" else (w.shape[1], len(grp) * tn)
        scratch.append(pltpu.VMEM(shape, BF16))
        scratch_bytes += _nbytes(shape, BF16)
    for e in extras:
        in_specs.append(pl.BlockSpec((tm, tn), lambda j, i: (i, j)))
        block_bytes += _nbytes((tm, tn), e.dtype)
    row_params = [r.reshape(1, -1) for r in rows]
    for r in row_params:
        in_specs.append(pl.BlockSpec((1, tn), lambda j, i: (0, j)))
    for dt in out_dtypes:
        block_bytes += _nbytes((tm, tn), dt)
    body = functools.partial(_mm_body, normed=tuple(g is not None for g in gains), w_x=tuple(w_x),
                             w_kinds=tuple(kind for _, _, kind in ws), n_extra=len(extras),
                             n_rows=len(rows), n_out=len(out_dtypes), epilogue=epilogue)
    outs = pl.pallas_call(
        body,
        grid=grid,
        in_specs=in_specs,
        out_specs=[pl.BlockSpec((tm, tn), lambda j, i: (i, j)) for _ in out_dtypes],
        out_shape=[jax.ShapeDtypeStruct((m, n), dt) for dt in out_dtypes],
        scratch_shapes=scratch,
        compiler_params=pltpu.CompilerParams(
            dimension_semantics=("arbitrary", "arbitrary"),
            vmem_limit_bytes=_vmem_limit(block_bytes, scratch_bytes)),
        name=name,
    )(*xs, *gain_rows, *[w for w, _, _ in ws], *extras, *row_params)
    return outs[0] if len(outs) == 1 else outs


def _ep_plain(accs, extras, rows, xraw):
    return (accs[0],)


def _ep_residual(accs, extras, rows, xraw):
    return (extras[0] + accs[0],)


def _ep_residual_norm(accs, extras, rows, xraw):
    h = extras[0] + accs[0]
    return h, _rmsnorm_rows(h, rows[0])


def _ep_swiglu(accs, extras, rows, xraw):
    return (_silu(accs[0]) * accs[1],)


def _ple_update(accs, xraw):
    return xraw[0] + _sigmoid(accs[0]) * accs[1]


def _ep_ple_norm(accs, extras, rows, xraw):
    h = _ple_update(accs, xraw)
    return h, _rmsnorm_rows(h, rows[0])


def _ep_ple_final(accs, extras, rows, xraw):
    return (_rmsnorm_rows(_ple_update(accs, xraw), rows[0]),)


def _dt_body(*refs, normed):
    nt = (((1,), (1,)), ((), ()))
    if normed:
        x_ref, g_ref, wt_ref, u_ref, dt_ref, dtt_ref = refs
        u = _rmsnorm_rows(x_ref[...], g_ref[...]).astype(BF16)
        u_ref[...] = u
    else:
        x_ref, wt_ref, dt_ref, dtt_ref = refs
        u = x_ref[...]
    wt = wt_ref[...].astype(BF16)
    dt_ref[...] = lax.dot_general(u, wt, nt, preferred_element_type=F32)
    dtt_ref[...] = lax.dot_general(wt, u, nt, preferred_element_type=F32)


def _dt_proj(x, gain, w_dt_t, tm=1024):
    m, k = x.shape
    normed = gain is not None
    wt_pad = jnp.pad(w_dt_t, ((0, LANES - w_dt_t.shape[0]), (0, 0)))
    row_spec = pl.BlockSpec((tm, k), lambda i: (i, 0))
    in_specs = [row_spec] + ([pl.BlockSpec((1, k), lambda i: (0, 0))] if normed else [])
    in_specs.append(pl.BlockSpec((LANES, k), lambda i: (0, 0)))
    out_specs = [pl.BlockSpec((tm, LANES), lambda i: (i, 0)), pl.BlockSpec((LANES, tm), lambda i: (0, i))]
    out_shape = [jax.ShapeDtypeStruct((m, LANES), F32), jax.ShapeDtypeStruct((LANES, m), F32)]
    if normed:
        out_specs.insert(0, row_spec)
        out_shape.insert(0, jax.ShapeDtypeStruct((m, k), BF16))
    blocks = (_nbytes((tm, k), x.dtype) + _nbytes((LANES, k), F32) + 2 * _nbytes((tm, LANES), F32)
              + (_nbytes((tm, k), BF16) if normed else 0))
    args = (x, gain.reshape(1, k), wt_pad) if normed else (x, wt_pad)
    return pl.pallas_call(
        functools.partial(_dt_body, normed=normed),
        grid=(m // tm,),
        in_specs=in_specs,
        out_specs=out_specs,
        out_shape=out_shape,
        compiler_params=pltpu.CompilerParams(
            dimension_semantics=("arbitrary",), vmem_limit_bytes=_vmem_limit(blocks, 0)),
        name="dt_proj",
    )(*args)


def _split3(v):
    hi = v.astype(BF16)
    r1 = v - hi.astype(F32)
    mid = r1.astype(BF16)
    lo = (r1 - mid.astype(F32)).astype(BF16)
    return hi, mid, lo


def _conv_silu(src_ref, halo_ref, buf_ref, w_ref, b_ref, g):
    q = SSD_CHUNK
    cur = src_ref[...].astype(F32)
    buf_ref[0:SUBLANES, :] = halo_ref[g]
    buf_ref[SUBLANES:SUBLANES + q, :] = cur
    halo_ref[g] = cur[q - SUBLANES:q, :]
    acc = b_ref[...] + w_ref[CONV_WIDTH - 1:CONV_WIDTH, :] * cur
    for k in range(CONV_WIDTH - 1):
        off = SUBLANES - (CONV_WIDTH - 1) + k
        acc = acc + w_ref[k:k + 1, :] * buf_ref[off:off + q, :]
    return _silu(acc)


def _ssd_body(z_ref, x_ref, b_ref, c_ref, dt_ref, dtt_ref,
              cwx_ref, cwb_ref, cwc_ref, cbx_ref, cbb_ref, cbc_ref,
              dtb_r_ref, dtb_c_ref, alog_r_ref, alog_c_ref, d_r_ref, ng_ref, e64_ref,
              y_ref,
              state_ref, hx_ref, hb_ref, hc_ref, xbuf_ref, bbuf_ref, cbuf_ref, dts3_ref, cs_ref, cst_ref):
    t = pl.program_id(1)
    q = SSD_CHUNK
    gw = x_ref.shape[1] // SSM_GROUPS
    hpg = gw // SSM_HEAD_DIM
    row = lax.broadcasted_iota(jnp.int32, (q, q), 0)
    col = lax.broadcasted_iota(jnp.int32, (q, q), 1)
    tril = row >= col

    @pl.when(t == 0)
    def _():
        state_ref[...] = jnp.zeros(state_ref.shape, F32)
        hx_ref[...] = jnp.zeros(hx_ref.shape, F32)
        hb_ref[...] = jnp.zeros(hb_ref.shape, F32)
        hc_ref[...] = jnp.zeros(hc_ref.shape, F32)

    dts = _softplus(dt_ref[...] + dtb_r_ref[...])
    lo_tri = tril.astype(BF16)
    cs = jnp.zeros((q, LANES), F32)
    for part in _split3(dts * (-jnp.exp(alog_r_ref[...]))):
        cs = cs + jnp.dot(lo_tri, part, preferred_element_type=F32)
    dts3_ref[...] = jnp.concatenate(_split3(dts), axis=1)
    cs_ref[...] = cs
    up_tri = (row <= col).astype(BF16)
    cst = jnp.zeros((LANES, q), F32)
    for part in _split3(_softplus(dtt_ref[...] + dtb_c_ref[...]) * (-jnp.exp(alog_c_ref[...]))):
        cst = cst + jnp.dot(part, up_tri, preferred_element_type=F32)
    cst_ref[...] = cst
    cs3 = jnp.concatenate(_split3(cs), axis=1)

    lane = lax.broadcasted_iota(jnp.int32, (q, LANES), 1)
    d3 = jnp.concatenate(_split3(jnp.broadcast_to(d_r_ref[...], (SUBLANES, LANES))), axis=1)
    for gi in range(SSM_GROUPS):
        xcols = slice(gi * gw, (gi + 1) * gw)
        ncols = slice(gi * SSM_STATE, (gi + 1) * SSM_STATE)

        xs = _conv_silu(x_ref.at[:, xcols], hx_ref, xbuf_ref.at[gi], cwx_ref.at[:, xcols],
                        cbx_ref.at[:, xcols], gi)
        bm = _conv_silu(b_ref.at[:, ncols], hb_ref, bbuf_ref.at[gi], cwb_ref.at[:, ncols],
                        cbb_ref.at[:, ncols], gi).astype(BF16)
        cm = _conv_silu(c_ref.at[:, ncols], hc_ref, cbuf_ref.at[gi], cwc_ref.at[:, ncols],
                        cbc_ref.at[:, ncols], gi).astype(BF16)

        e64 = e64_ref[gi]
        dt_x = jnp.dot(dts3_ref[...], e64, preferred_element_type=F32)
        cs_x = jnp.dot(cs3, e64, preferred_element_type=F32)
        d_x = jnp.dot(d3, e64, preferred_element_type=F32)[0:1, :]
        cs_last = cs_x[q - 1:q, :]

        xdt = xs * dt_x
        xdt_b = xdt.astype(BF16)
        cb = lax.dot_general(cm, bm, (((1,), (1,)), ((), ())), preferred_element_type=F32)
        y_pairs = []
        for j in range(hpg // 2):
            xp = xdt_b[:, j * LANES:(j + 1) * LANES]
            outs = []
            for hh in range(2):
                h = 2 * j + hh
                head = gi * hpg + h
                colb = jnp.broadcast_to(cs_ref[:, head:head + 1], (q, LANES))
                rowb = cst_ref[head:head + 1, :]
                decay = jnp.exp(jnp.where(tril, colb - rowb, -jnp.inf))
                outs.append(jnp.dot((cb * decay).astype(BF16), xp, preferred_element_type=F32))
            y_pairs.append(jnp.where(lane < SSM_HEAD_DIM, outs[0], outs[1]))
        y = jnp.concatenate(y_pairs, axis=1)

        st = state_ref[gi]
        y = y + jnp.dot(cm, st.astype(BF16), preferred_element_type=F32) * jnp.exp(cs_x)
        xsc = (xdt * jnp.exp(cs_last - cs_x)).astype(BF16)
        s_new = lax.dot_general(bm, xsc, (((0,), (0,)), ((), ())), preferred_element_type=F32)
        state_ref[gi] = st * jnp.exp(cs_last) + s_new

        y = y + d_x * xs
        yg = y * _silu(z_ref[:, xcols].astype(F32))
        y_ref[:, xcols] = _rmsnorm_rows(yg, ng_ref[:, xcols]).astype(y_ref.dtype)


def _expansion_matrices(heads_per_group, width):
    k = lax.broadcasted_iota(jnp.int32, (SSM_GROUPS, 3 * LANES, heads_per_group * width), 1) % LANES
    c = lax.broadcasted_iota(jnp.int32, (SSM_GROUPS, 3 * LANES, heads_per_group * width), 2) // width
    gi = lax.broadcasted_iota(jnp.int32, (SSM_GROUPS, 3 * LANES, heads_per_group * width), 0)
    return (k == gi * heads_per_group + c).astype(BF16)


def _ssd_scan(zx, dt, dtt, conv_w, conv_b, dt_bias, a_log, d_skip, norm_g, batch):
    m = zx.shape[0]
    q = SSD_CHUNK
    gn = SSM_GROUPS * SSM_STATE
    d_inner = (zx.shape[1] - 2 * gn) // 2
    gw = d_inner // SSM_GROUPS
    hpg = gw // SSM_HEAD_DIM
    nt = m // batch // q
    x_blk, b_blk, c_blk = 1, 2 * d_inner // gn, 2 * d_inner // gn + 1
    cwb_blk, cwc_blk = d_inner // gn, d_inner // gn + 1

    def pad_row(v):
        return jnp.pad(v, (0, LANES - v.shape[0])).reshape(1, LANES)

    def pad_col(v):
        return jnp.pad(v, (0, LANES - v.shape[0])).reshape(LANES, 1)

    e64 = _expansion_matrices(hpg, SSM_HEAD_DIM)
    rows = lambda blk: (lambda b, t: (b * nt + t, blk))
    cols = lambda blk: (lambda b, t: (0, blk))
    in_specs = [
        pl.BlockSpec((q, d_inner), rows(0)),
        pl.BlockSpec((q, d_inner), rows(x_blk)),
        pl.BlockSpec((q, gn), rows(b_blk)),
        pl.BlockSpec((q, gn), rows(c_blk)),
        pl.BlockSpec((q, LANES), rows(0)),
        pl.BlockSpec((LANES, q), lambda b, t: (0, b * nt + t)),
        pl.BlockSpec((CONV_WIDTH, d_inner), cols(0)),
        pl.BlockSpec((CONV_WIDTH, gn), cols(cwb_blk)),
        pl.BlockSpec((CONV_WIDTH, gn), cols(cwc_blk)),
        pl.BlockSpec((1, d_inner), cols(0)),
        pl.BlockSpec((1, gn), cols(cwb_blk)),
        pl.BlockSpec((1, gn), cols(cwc_blk)),
        pl.BlockSpec((1, LANES), cols(0)),
        pl.BlockSpec((LANES, 1), cols(0)),
        pl.BlockSpec((1, LANES), cols(0)),
        pl.BlockSpec((LANES, 1), cols(0)),
        pl.BlockSpec((1, LANES), cols(0)),
        pl.BlockSpec((1, d_inner), cols(0)),
        pl.BlockSpec(e64.shape, lambda b, t: (0, 0, 0)),
    ]
    scratch_shapes = [
        ((SSM_GROUPS, SSM_STATE, gw), F32),
        ((SSM_GROUPS, SUBLANES, gw), F32),
        ((SSM_GROUPS, SUBLANES, SSM_STATE), F32),
        ((SSM_GROUPS, SUBLANES, SSM_STATE), F32),
        ((SSM_GROUPS, SUBLANES + q, gw), F32),
        ((SSM_GROUPS, SUBLANES + q, SSM_STATE), F32),
        ((SSM_GROUPS, SUBLANES + q, SSM_STATE), F32),
        ((q, 3 * LANES), BF16),
        ((q, LANES), F32),
        ((LANES, q), F32),
    ]
    block_bytes = (3 * _nbytes((q, d_inner), BF16) + 2 * _nbytes((q, gn), BF16) + 2 * _nbytes((q, LANES), F32)
                   + _nbytes(e64.shape, BF16) + (CONV_WIDTH + 2) * _nbytes((SUBLANES, d_inner + 2 * gn), F32))
    return pl.pallas_call(
        _ssd_body,
        grid=(batch, nt),
        in_specs=in_specs,
        out_specs=pl.BlockSpec((q, d_inner), rows(0)),
        out_shape=jax.ShapeDtypeStruct((m, d_inner), BF16),
        scratch_shapes=[pltpu.VMEM(s, d) for s, d in scratch_shapes],
        compiler_params=pltpu.CompilerParams(
            dimension_semantics=("arbitrary", "arbitrary"),
            vmem_limit_bytes=_vmem_limit(block_bytes, sum(_nbytes(s, d) for s, d in scratch_shapes))),
        name="ssd_scan",
    )(zx, zx, zx, zx, dt, dtt, conv_w, conv_w, conv_w, conv_b, conv_b, conv_b,
      pad_row(dt_bias), pad_col(dt_bias), pad_row(a_log), pad_col(a_log), pad_row(d_skip),
      norm_g.reshape(1, d_inner), e64)


def _fold_rows(v, op):
    r, c = v.shape
    v3 = v.reshape(r // SUBLANES, SUBLANES, c)
    return jnp.max(v3, axis=0) if op == "max" else jnp.sum(v3, axis=0)


def _moba_body(q_ref, k_ref, v_ref, o_ref,
               kmean_ref, kaug_ref, vt_ref, qaug_ref, s_ref, m_ref, l_ref, acc_ref):
    qi = pl.program_id(2)
    bs = MOBA_BLOCK
    dh = LANES
    seq = k_ref.shape[0]
    nb = seq // bs
    heads = range(q_ref.shape[1] // dh)
    cbk = ATTN_CHUNK_BLOCKS
    ck = cbk * bs
    c2 = dh ** -0.5 * LOG2_E
    nt = (((1,), (1,)), ((), ()))

    @pl.when(qi == 0)
    def _():
        lane = lax.broadcasted_iota(jnp.int32, (bs, LANES), 1)
        for hh in heads:
            for n in range(nb):
                kb = k_ref[n * bs:(n + 1) * bs, hh * dh:(hh + 1) * dh]
                kmean_ref[hh, n:n + 1, :] = jnp.mean(kb.astype(F32), axis=0, keepdims=True)
                kaug_ref[hh, n * bs:(n + 1) * bs, 0:dh] = kb
                kaug_ref[hh, n * bs:(n + 1) * bs, dh:dh + LANES] = jnp.where(lane == n, 1.0, 0.0).astype(BF16)
                vt_ref[hh, n] = v_ref[n * bs:(n + 1) * bs, hh * dh:(hh + 1) * dh].astype(F32).T.astype(BF16)

    own = pl.multiple_of(qi * bs, bs)
    blk = lax.broadcasted_iota(jnp.int32, (nb, bs), 0)
    key = lax.broadcasted_iota(jnp.int32, (bs, bs), 0)
    qry = lax.broadcasted_iota(jnp.int32, (bs, bs), 1)
    for hh in heads:
        q = q_ref[:, hh * dh:(hh + 1) * dh]
        gate = lax.dot_general(kmean_ref[hh], q.astype(F32), nt,
                               precision=lax.Precision.HIGHEST, preferred_element_type=F32)
        gate = jnp.where(blk < qi, gate, -jnp.inf)
        rank = jnp.zeros((nb, bs), F32)
        for mm in range(nb):
            gm = gate[mm:mm + 1, :]
            beats = jnp.where(gm > gate, 1.0, jnp.where(jnp.logical_and(gm == gate, blk > mm), 1.0, 0.0))
            rank = rank + beats
        chosen = jnp.logical_and(rank < MOBA_TOPK, gate > -jnp.inf)
        bias_t = jnp.where(chosen, 0.0, MASK_BIAS)
        bias_q = jnp.concatenate([bias_t, jnp.zeros((LANES - nb, bs), F32)], axis=0).T
        qaug_ref[hh, :, 0:dh] = q
        qaug_ref[hh, :, dh:dh + LANES] = bias_q.astype(BF16)
        s = lax.dot_general(k_ref[pl.ds(own, bs), hh * dh:(hh + 1) * dh], q, nt,
                            preferred_element_type=F32) * c2
        s = jnp.where(key <= qry, s, MASK_BIAS)
        s_ref[hh, seq:seq + bs, :] = s
        m_ref[hh] = _fold_rows(s, "max")

    n_chunks = lax.shift_right_logical(qi + (cbk - 1), cbk.bit_length() - 1)
    for n in range(1, nb // cbk + 1):
        @pl.when(n_chunks == n)
        def _(n=n):
            for c in range(n):
                for hh in heads:
                    s = lax.dot_general(kaug_ref[hh, c * ck:(c + 1) * ck, :], qaug_ref[hh], nt,
                                        preferred_element_type=F32) * c2
                    s_ref[hh, c * ck:(c + 1) * ck, :] = s
                    m_ref[hh] = jnp.maximum(m_ref[hh], _fold_rows(s, "max"))

    for hh in heads:
        m = jnp.max(m_ref[hh], axis=0, keepdims=True)
        m_ref[hh] = jnp.broadcast_to(m, (SUBLANES, bs))
        p = jnp.exp2(s_ref[hh, seq:seq + bs, :] - m)
        l_ref[hh] = _fold_rows(p, "sum")
        acc_ref[hh] = jnp.dot(vt_ref[hh, qi], p.astype(BF16), preferred_element_type=F32)

    for n in range(1, nb // cbk + 1):
        @pl.when(n_chunks == n)
        def _(n=n):
            for hh in heads:
                acc = acc_ref[hh]
                lsum = l_ref[hh]
                for c in range(n):
                    p = jnp.exp2(s_ref[hh, c * ck:(c + 1) * ck, :] - m_ref[hh, 0:1, :])
                    lsum = lsum + _fold_rows(p, "sum")
                    pb = p.astype(BF16)
                    for j in range(cbk):
                        acc = acc + jnp.dot(vt_ref[hh, c * cbk + j], pb[j * bs:(j + 1) * bs, :],
                                            preferred_element_type=F32)
                l_ref[hh] = lsum
                acc_ref[hh] = acc

    for hh in heads:
        l = jnp.sum(l_ref[hh], axis=0, keepdims=True)
        o_ref[:, hh * dh:(hh + 1) * dh] = (acc_ref[hh] / l).T.astype(o_ref.dtype)


def _moba_attention(qkv, batch):
    m = qkv.shape[0]
    seq = m // batch
    dh = qkv.shape[1] // (3 * ATTN_HEADS)
    bs = MOBA_BLOCK
    nq = seq // bs
    hb = ATTN_HEADS_PER_STEP
    ng = ATTN_HEADS // hb
    assert dh == LANES and nq % ATTN_CHUNK_BLOCKS == 0 and nq <= LANES
    assert ATTN_CHUNK_BLOCKS & (ATTN_CHUNK_BLOCKS - 1) == 0
    block_bytes = 2 * _nbytes((bs, hb * dh), BF16) + 2 * _nbytes((seq, hb * dh), BF16)
    scratch_shapes = [
        ((hb, nq, dh), F32),
        ((hb, seq, dh + LANES), BF16),
        ((hb, nq, dh, bs), BF16),
        ((hb, bs, dh + LANES), BF16),
        ((hb, seq + bs, bs), F32),
        ((hb, SUBLANES, bs), F32),
        ((hb, SUBLANES, bs), F32),
        ((hb, dh, bs), F32),
    ]
    return pl.pallas_call(
        _moba_body,
        grid=(batch, ng, nq),
        in_specs=[pl.BlockSpec((bs, hb * dh), lambda b, h, i: (b * nq + i, h)),
                  pl.BlockSpec((seq, hb * dh), lambda b, h, i: (b, ng + h)),
                  pl.BlockSpec((seq, hb * dh), lambda b, h, i: (b, 2 * ng + h))],
        out_specs=pl.BlockSpec((bs, hb * dh), lambda b, h, i: (b * nq + i, h)),
        out_shape=jax.ShapeDtypeStruct((m, ATTN_HEADS * dh), BF16),
        scratch_shapes=[pltpu.VMEM(s, d) for s, d in scratch_shapes],
        compiler_params=pltpu.CompilerParams(
            dimension_semantics=("arbitrary", "arbitrary", "arbitrary"),
            vmem_limit_bytes=_vmem_limit(block_bytes, sum(_nbytes(s, d) for s, d in scratch_shapes))),
        name="moba_attention",
    )(qkv, qkv, qkv)


def _mamba2_mixer(h, u, gain, w_in_t, layer, conv_w, conv_b, dt_bias, a_log, d_skip, norm_g, w_out,
                  next_gain, batch):
    d_inner = w_out.shape[1]
    n_zx = 2 * d_inner + 2 * SSM_GROUPS * SSM_STATE
    w_dt_t = w_in_t[layer][n_zx:, :]
    if u is None:
        u, dt, dtt = _dt_proj(h, gain, w_dt_t)
    else:
        dt, dtt = _dt_proj(u, None, w_dt_t)
    zx = _fused_matmul([u], [None], [(w_in_t, layer, "f32_t")], [0], [], [], _ep_plain, (BF16,),
                       tm=2048, tn=1024, n_out=n_zx, name="ssm_in_proj")
    y = _ssd_scan(zx, dt, dtt, conv_w, conv_b.reshape(1, -1), dt_bias, a_log, d_skip, norm_g, batch)
    return _fused_matmul([y], [None], [(w_out[layer].astype(BF16), None, "bf16")], [0], [h], [next_gain],
                         _ep_residual_norm, (F32, BF16), tm=512, tn=h.shape[1], single_buffer_weights=True,
                         name="ssm_out_proj")


def _moba_mixer(h, u, w_qkv, w_o, layer, next_gain, batch):
    qkv = _fused_matmul([u], [None], [(w_qkv, layer, "f32")], [0], [], [], _ep_plain, (BF16,),
                        tm=2048, tn=1024, name="attn_qkv")
    o = _moba_attention(qkv, batch)
    d = h.shape[1]
    return _fused_matmul([o], [None], [(w_o[layer].astype(BF16), None, "bf16")], [0], [h], [next_gain],
                         _ep_residual_norm, (F32, BF16), tm=512, tn=d, single_buffer_weights=True,
                         name="attn_out")


def _ffn(h, u, layer, w_gate, w_up, w_down):
    ws = [(w_gate, layer, "f32"), (w_up, layer, "f32")]
    a = _fused_matmul([u], [None], ws, [0, 0], [], [], _ep_swiglu, (BF16,), tm=1024, tn=512, name="ffn_up")
    return _fused_matmul([a], [None], [(w_down, layer, "f32")], [0], [h], [], _ep_residual, (F32,),
                         tm=512, tn=1024, single_buffer_weights=True, name="ffn_down")


def _ple(h, p, layer, gain, w_pgate_l, w_pproj_l, out_gain, last):
    d = h.shape[1]
    ws = [(w_pgate_l.astype(BF16), None, "bf16"), (w_pproj_l.astype(BF16), None, "bf16")]
    if last:
        return _fused_matmul([h, p], [gain, None], ws, [0, 1], [], [out_gain], _ep_ple_final, (F32,),
                             tm=512, tn=d, single_buffer_weights=True, x_layer=layer, name="ple_final")
    return _fused_matmul([h, p], [gain, None], ws, [0, 1], [], [out_gain], _ep_ple_norm, (F32, BF16),
                         tm=512, tn=d, single_buffer_weights=True, x_layer=layer, name="ple")


def kernel(x, p, mix_norm_g, ffn_norm_g, ple_norm_g, ssm_w_in, ssm_conv_w, ssm_conv_b, ssm_dt_bias, ssm_a_log, ssm_d, ssm_norm_g, ssm_w_out, attn_w_qkv, attn_w_o, ffn_w_gate, ffn_w_up, ffn_w_down, ple_w_proj, ple_w_gate, final_norm_g):
    batch, seq, d = x.shape
    m = batch * seq
    depth = p.shape[0]
    ssm_w_in_t = jnp.swapaxes(ssm_w_in, 1, 2)
    h = x.reshape(m, d)
    u = None
    for i in range(depth):
        j = i // 2
        if i % 2 == 0:
            h, u = _mamba2_mixer(h, u, mix_norm_g[i], ssm_w_in_t, j, ssm_conv_w[j], ssm_conv_b[j], ssm_dt_bias[j],
                                 ssm_a_log[j], ssm_d[j], ssm_norm_g[j], ssm_w_out, ffn_norm_g[i], batch)
        else:
            h, u = _moba_mixer(h, u, attn_w_qkv, attn_w_o, j, ffn_norm_g[i], batch)
        h = _ffn(h, u, i, ffn_w_gate, ffn_w_up, ffn_w_down)
        last = i == depth - 1
        out = _ple(h, p.reshape(depth, m, -1), i, ple_norm_g[i], ple_w_gate[i], ple_w_proj[i],
                   final_norm_g if last else mix_norm_g[i + 1], last)
        if not last:
            h, u = out
    return out.reshape(batch, seq, d)
```

```python
import functools

import jax
import jax.numpy as jnp
from jax import lax
from jax.experimental import pallas as pl
from jax.experimental.pallas import tpu as pltpu

NORM_EPS = 1e-6

SSM_HEAD_DIM = 64
SSM_GROUPS = 8
SSM_STATE = 128
CONV_WIDTH = 4
SSD_CHUNK = 128

ATTN_HEADS = 16
MOBA_BLOCK = 256
MOBA_TOPK = 3
ATTN_CHUNK_BLOCKS = 4
ATTN_HEADS_PER_STEP = 4
MASK_BIAS = -1e30
LOG2_E = 1.4426950408889634

LANES = 128
SUBLANES = 8
VMEM_BYTES = 64 * 1024 * 1024
VMEM_UNSCOPED_BYTES = 4 * 1024 * 1024
COMPILER_SCRATCH_BYTES = 12 * 1024 * 1024

F32 = jnp.float32
BF16 = jnp.bfloat16


def _nbytes(shape, dtype):
    n = 1
    for s in shape:
        n *= s
    return n * jnp.dtype(dtype).itemsize


def _vmem_limit(block_bytes, scratch_bytes):
    need = 2 * block_bytes + scratch_bytes + COMPILER_SCRATCH_BYTES
    return int(min(need, VMEM_BYTES - VMEM_UNSCOPED_BYTES))


def _sigmoid(v):
    return 0.5 * jnp.tanh(0.5 * v) + 0.5


def _silu(v):
    hv = 0.5 * v
    return hv * jnp.tanh(hv) + hv


def _softplus(v):
    return jnp.maximum(v, 0.0) + jnp.log1p(jnp.exp(-jnp.abs(v)))


def _rmsnorm_rows(x, g):
    ms = jnp.mean(x * x, axis=-1, keepdims=True)
    return x * lax.rsqrt(ms + NORM_EPS) * g


def _cast_groups(w_x, w_kinds):
    groups = []
    for k, kind in enumerate(w_kinds):
        if kind == "bf16":
            continue
        mate = next((g for g in groups if kind == "f32" and w_kinds[g[0]] == "f32" and w_x[g[0]] == w_x[k]), None)
        if mate is None:
            groups.append([k])
        else:
            mate.append(k)
    return groups


def _mm_body(*refs, normed, w_x, w_kinds, n_extra, n_rows, n_out, epilogue):
    n_x, n_g, n_w = len(normed), sum(normed), len(w_x)
    groups = _cast_groups(w_x, w_kinds)
    refs = list(refs)
    x_refs = [refs.pop(0) for _ in range(n_x)]
    g_refs = [refs.pop(0) for _ in range(n_g)]
    w_refs = [refs.pop(0) for _ in range(n_w)]
    e_refs = [refs.pop(0) for _ in range(n_extra)]
    r_refs = [refs.pop(0) for _ in range(n_rows)]
    o_refs = [refs.pop(0) for _ in range(n_out)]
    wb_refs = [refs.pop(0) for _ in groups]
    tn = o_refs[0].shape[1]

    @pl.when(pl.program_id(1) == 0)
    def _():
        for grp, wb_ref in zip(groups, wb_refs):
            if w_kinds[grp[0]] == "f32_t":
                wb_ref[...] = w_refs[grp[0]][...].astype(BF16)
            else:
                for slot, k in enumerate(grp):
                    wb_ref[:, slot * tn:(slot + 1) * tn] = w_refs[k][...].astype(BF16)

    xraw = [x_ref[...] for x_ref in x_refs]
    xs = []
    for x, is_normed in zip(xraw, normed):
        if is_normed:
            x = _rmsnorm_rows(x, g_refs.pop(0)[...])
        xs.append(x.astype(BF16))
    accs = [None] * n_w
    for grp, wb_ref in zip(groups, wb_refs):
        x = xs[w_x[grp[0]]]
        if w_kinds[grp[0]] == "f32_t":
            accs[grp[0]] = lax.dot_general(x, wb_ref[...], (((1,), (1,)), ((), ())), preferred_element_type=F32)
        else:
            acc = jnp.dot(x, wb_ref[...], preferred_element_type=F32)
            for slot, k in enumerate(grp):
                accs[k] = acc[:, slot * tn:(slot + 1) * tn]
    for k, kind in enumerate(w_kinds):
        if kind == "bf16":
            accs[k] = jnp.dot(xs[w_x[k]], w_refs[k][...], preferred_element_type=F32)
    outs = epilogue(accs, [e_ref[...] for e_ref in e_refs], [r_ref[...] for r_ref in r_refs], xraw)
    for o_ref, o in zip(o_refs, outs):
        o_ref[...] = o.astype(o_ref.dtype)


def _fused_matmul(xs, gains, ws, w_x, extras, rows, epilogue, out_dtypes, tm, tn, n_out=None,
                  single_buffer_weights=False, x_layer=0, name="matmul"):
    m = xs[0].shape[0]
    if n_out is None:
        w0, _, kind0 = ws[0]
        n = w0.shape[1] if kind0 == "f32_t" else w0.shape[-1]
    else:
        n = n_out
    grid = (n // tn, m // tm)
    in_specs, block_bytes, scratch, scratch_bytes = [], 0, [], 0
    for x in xs:
        k = x.shape[-1]
        if x.ndim == 3:
            in_specs.append(pl.BlockSpec((None, tm, k), functools.partial(lambda j, i, l: (l, i, 0), l=x_layer)))
        else:
            in_specs.append(pl.BlockSpec((tm, k), lambda j, i: (i, 0)))
        block_bytes += _nbytes((tm, k), x.dtype)
    gain_rows = [g.reshape(1, -1) for g in gains if g is not None]
    for g in gain_rows:
        in_specs.append(pl.BlockSpec(g.shape, lambda j, i: (0, 0)))
    w_mode = dict(pipeline_mode=pl.Buffered(1)) if single_buffer_weights else {}
    buffers = 1 if single_buffer_weights else 2
    for w, layer, kind in ws:
        if kind == "bf16":
            k = w.shape[0]
            in_specs.append(pl.BlockSpec((k, tn), lambda j, i: (0, j), **w_mode))
            block_bytes += _nbytes((k, tn), BF16) * buffers // 2
            continue
        if kind == "f32_t":
            k = w.shape[2]
            shape = (tn, k)
            in_specs.append(pl.BlockSpec((None, tn, k), functools.partial(lambda j, i, l: (l, j, 0), l=layer),
                                         **w_mode))
        else:
            k = w.shape[1]
            shape = (k, tn)
            in_specs.append(pl.BlockSpec((None, k, tn), functools.partial(lambda j, i, l: (l, 0, j), l=layer),
                                         **w_mode))
        block_bytes += _nbytes(shape, F32) * buffers // 2
    for grp in _cast_groups(w_x, [kind for _, _, kind in ws]):
        w, _, kind = ws[grp[0]]
        shape = (tn, w.shape[2]) if kind == "f32_t" else (w.shape[1], len(grp) * tn)
        scratch.append(pltpu.VMEM(shape, BF16))
        scratch_bytes += _nbytes(shape, BF16)
    for e in extras:
        in_specs.append(pl.BlockSpec((tm, tn), lambda j, i: (i, j)))
        block_bytes += _nbytes((tm, tn), e.dtype)
    row_params = [r.reshape(1, -1) for r in rows]
    for r in row_params:
        in_specs.append(pl.BlockSpec((1, tn), lambda j, i: (0, j)))
    for dt in out_dtypes:
        block_bytes += _nbytes((tm, tn), dt)
    body = functools.partial(_mm_body, normed=tuple(g is not None for g in gains), w_x=tuple(w_x),
                             w_kinds=tuple(kind for _, _, kind in ws), n_extra=len(extras),
                             n_rows=len(rows), n_out=len(out_dtypes), epilogue=epilogue)
    outs = pl.pallas_call(
        body,
        grid=grid,
        in_specs=in_specs,
        out_specs=[pl.BlockSpec((tm, tn), lambda j, i: (i, j)) for _ in out_dtypes],
        out_shape=[jax.ShapeDtypeStruct((m, n), dt) for dt in out_dtypes],
        scratch_shapes=scratch,
        compiler_params=pltpu.CompilerParams(
            dimension_semantics=("arbitrary", "arbitrary"),
            vmem_limit_bytes=_vmem_limit(block_bytes, scratch_bytes)),
        name=name,
    )(*xs, *gain_rows, *[w for w, _, _ in ws], *extras, *row_params)
    return outs[0] if len(outs) == 1 else outs


def _ep_plain(accs, extras, rows, xraw):
    return (accs[0],)


def _ep_residual(accs, extras, rows, xraw):
    return (extras[0] + accs[0],)


def _ep_residual_norm(accs, extras, rows, xraw):
    h = extras[0] + accs[0]
    return h, _rmsnorm_rows(h, rows[0])


def _ep_swiglu(accs, extras, rows, xraw):
    return (_silu(accs[0]) * accs[1],)


def _ple_update(accs, xraw):
    return xraw[0] + _sigmoid(accs[0]) * accs[1]


def _ep_ple_norm(accs, extras, rows, xraw):
    h = _ple_update(accs, xraw)
    return h, _rmsnorm_rows(h, rows[0])


def _ep_ple_final(accs, extras, rows, xraw):
    return (_rmsnorm_rows(_ple_update(accs, xraw), rows[0]),)


def _dt_body(*refs, normed):
    nt = (((1,), (1,)), ((), ()))
    if normed:
        x_ref, g_ref, wt_ref, u_ref, dt_ref, dtt_ref = refs
        u = _rmsnorm_rows(x_ref[...], g_ref[...]).astype(BF16)
        u_ref[...] = u
    else:
        x_ref, wt_ref, dt_ref, dtt_ref = refs
        u = x_ref[...]
    wt = wt_ref[...].astype(BF16)
    dt_ref[...] = lax.dot_general(u, wt, nt, preferred_element_type=F32)
    dtt_ref[...] = lax.dot_general(wt, u, nt, preferred_element_type=F32)


def _dt_proj(x, gain, w_dt_t, tm=1024):
    m, k = x.shape
    normed = gain is not None
    wt_pad = jnp.pad(w_dt_t, ((0, LANES - w_dt_t.shape[0]), (0, 0)))
    row_spec = pl.BlockSpec((tm, k), lambda i: (i, 0))
    in_specs = [row_spec] + ([pl.BlockSpec((1, k), lambda i: (0, 0))] if normed else [])
    in_specs.append(pl.BlockSpec((LANES, k), lambda i: (0, 0)))
    out_specs = [pl.BlockSpec((tm, LANES), lambda i: (i, 0)), pl.BlockSpec((LANES, tm), lambda i: (0, i))]
    out_shape = [jax.ShapeDtypeStruct((m, LANES), F32), jax.ShapeDtypeStruct((LANES, m), F32)]
    if normed:
        out_specs.insert(0, row_spec)
        out_shape.insert(0, jax.ShapeDtypeStruct((m, k), BF16))
    blocks = (_nbytes((tm, k), x.dtype) + _nbytes((LANES, k), F32) + 2 * _nbytes((tm, LANES), F32)
              + (_nbytes((tm, k), BF16) if normed else 0))
    args = (x, gain.reshape(1, k), wt_pad) if normed else (x, wt_pad)
    return pl.pallas_call(
        functools.partial(_dt_body, normed=normed),
        grid=(m // tm,),
        in_specs=in_specs,
        out_specs=out_specs,
        out_shape=out_shape,
        compiler_params=pltpu.CompilerParams(
            dimension_semantics=("arbitrary",), vmem_limit_bytes=_vmem_limit(blocks, 0)),
        name="dt_proj",
    )(*args)


def _split3(v):
    hi = v.astype(BF16)
    r1 = v - hi.astype(F32)
    mid = r1.astype(BF16)
    lo = (r1 - mid.astype(F32)).astype(BF16)
    return hi, mid, lo


def _conv_silu(src_ref, halo_ref, buf_ref, w_ref, b_ref, g):
    q = SSD_CHUNK
    cur = src_ref[...].astype(F32)
    buf_ref[0:SUBLANES, :] = halo_ref[g]
    buf_ref[SUBLANES:SUBLANES + q, :] = cur
    halo_ref[g] = cur[q - SUBLANES:q, :]
    acc = b_ref[...] + w_ref[CONV_WIDTH - 1:CONV_WIDTH, :] * cur
    for k in range(CONV_WIDTH - 1):
        off = SUBLANES - (CONV_WIDTH - 1) + k
        acc = acc + w_ref[k:k + 1, :] * buf_ref[off:off + q, :]
    return _silu(acc)


def _ssd_body(z_ref, x_ref, b_ref, c_ref, dt_ref, dtt_ref,
              cwx_ref, cwb_ref, cwc_ref, cbx_ref, cbb_ref, cbc_ref,
              dtb_r_ref, dtb_c_ref, alog_r_ref, alog_c_ref, d_r_ref, ng_ref, e64_ref,
              y_ref,
              state_ref, hx_ref, hb_ref, hc_ref, xbuf_ref, bbuf_ref, cbuf_ref, dts3_ref, cs_ref, cst_ref):
    t = pl.program_id(1)
    q = SSD_CHUNK
    gw = x_ref.shape[1] // SSM_GROUPS
    hpg = gw // SSM_HEAD_DIM
    row = lax.broadcasted_iota(jnp.int32, (q, q), 0)
    col = lax.broadcasted_iota(jnp.int32, (q, q), 1)
    tril = row >= col

    @pl.when(t == 0)
    def _():
        state_ref[...] = jnp.zeros(state_ref.shape, F32)
        hx_ref[...] = jnp.zeros(hx_ref.shape, F32)
        hb_ref[...] = jnp.zeros(hb_ref.shape, F32)
        hc_ref[...] = jnp.zeros(hc_ref.shape, F32)

    dts = _softplus(dt_ref[...] + dtb_r_ref[...])
    lo_tri = tril.astype(BF16)
    cs = jnp.zeros((q, LANES), F32)
    for part in _split3(dts * (-jnp.exp(alog_r_ref[...]))):
        cs = cs + jnp.dot(lo_tri, part, preferred_element_type=F32)
    dts3_ref[...] = jnp.concatenate(_split3(dts), axis=1)
    cs_ref[...] = cs
    up_tri = (row <= col).astype(BF16)
    cst = jnp.zeros((LANES, q), F32)
    for part in _split3(_softplus(dtt_ref[...] + dtb_c_ref[...]) * (-jnp.exp(alog_c_ref[...]))):
        cst = cst + jnp.dot(part, up_tri, preferred_element_type=F32)
    cst_ref[...] = cst
    cs3 = jnp.concatenate(_split3(cs), axis=1)

    lane = lax.broadcasted_iota(jnp.int32, (q, LANES), 1)
    d3 = jnp.concatenate(_split3(jnp.broadcast_to(d_r_ref[...], (SUBLANES, LANES))), axis=1)
    for gi in range(SSM_GROUPS):
        xcols = slice(gi * gw, (gi + 1) * gw)
        ncols = slice(gi * SSM_STATE, (gi + 1) * SSM_STATE)

        xs = _conv_silu(x_ref.at[:, xcols], hx_ref, xbuf_ref.at[gi], cwx_ref.at[:, xcols],
                        cbx_ref.at[:, xcols], gi)
        bm = _conv_silu(b_ref.at[:, ncols], hb_ref, bbuf_ref.at[gi], cwb_ref.at[:, ncols],
                        cbb_ref.at[:, ncols], gi).astype(BF16)
        cm = _conv_silu(c_ref.at[:, ncols], hc_ref, cbuf_ref.at[gi], cwc_ref.at[:, ncols],
                        cbc_ref.at[:, ncols], gi).astype(BF16)

        e64 = e64_ref[gi]
        dt_x = jnp.dot(dts3_ref[...], e64, preferred_element_type=F32)
        cs_x = jnp.dot(cs3, e64, preferred_element_type=F32)
        d_x = jnp.dot(d3, e64, preferred_element_type=F32)[0:1, :]
        cs_last = cs_x[q - 1:q, :]

        xdt = xs * dt_x
        xdt_b = xdt.astype(BF16)
        cb = lax.dot_general(cm, bm, (((1,), (1,)), ((), ())), preferred_element_type=F32)
        y_pairs = []
        for j in range(hpg // 2):
            xp = xdt_b[:, j * LANES:(j + 1) * LANES]
            outs = []
            for hh in range(2):
                h = 2 * j + hh
                head = gi * hpg + h
                colb = jnp.broadcast_to(cs_ref[:, head:head + 1], (q, LANES))
                rowb = cst_ref[head:head + 1, :]
                decay = jnp.exp(jnp.where(tril, colb - rowb, -jnp.inf))
                outs.append(jnp.dot((cb * decay).astype(BF16), xp, preferred_element_type=F32))
            y_pairs.append(jnp.where(lane < SSM_HEAD_DIM, outs[0], outs[1]))
        y = jnp.concatenate(y_pairs, axis=1)

        st = state_ref[gi]
        y = y + jnp.dot(cm, st.astype(BF16), preferred_element_type=F32) * jnp.exp(cs_x)
        xsc = (xdt * jnp.exp(cs_last - cs_x)).astype(BF16)
        s_new = lax.dot_general(bm, xsc, (((0,), (0,)), ((), ())), preferred_element_type=F32)
        state_ref[gi] = st * jnp.exp(cs_last) + s_new

        y = y + d_x * xs
        yg = y * _silu(z_ref[:, xcols].astype(F32))
        y_ref[:, xcols] = _rmsnorm_rows(yg, ng_ref[:, xcols]).astype(y_ref.dtype)


def _expansion_matrices(heads_per_group, width):
    k = lax.broadcasted_iota(jnp.int32, (SSM_GROUPS, 3 * LANES, heads_per_group * width), 1) % LANES
    c = lax.broadcasted_iota(jnp.int32, (SSM_GROUPS, 3 * LANES, heads_per_group * width), 2) // width
    gi = lax.broadcasted_iota(jnp.int32, (SSM_GROUPS, 3 * LANES, heads_per_group * width), 0)
    return (k == gi * heads_per_group + c).astype(BF16)


def _ssd_scan(zx, dt, dtt, conv_w, conv_b, dt_bias, a_log, d_skip, norm_g, batch):
    m = zx.shape[0]
    q = SSD_CHUNK
    gn = SSM_GROUPS * SSM_STATE
    d_inner = (zx.shape[1] - 2 * gn) // 2
    gw = d_inner // SSM_GROUPS
    hpg = gw // SSM_HEAD_DIM
    nt = m // batch // q
    x_blk, b_blk, c_blk = 1, 2 * d_inner // gn, 2 * d_inner // gn + 1
    cwb_blk, cwc_blk = d_inner // gn, d_inner // gn + 1

    def pad_row(v):
        return jnp.pad(v, (0, LANES - v.shape[0])).reshape(1, LANES)

    def pad_col(v):
        return jnp.pad(v, (0, LANES - v.shape[0])).reshape(LANES, 1)

    e64 = _expansion_matrices(hpg, SSM_HEAD_DIM)
    rows = lambda blk: (lambda b, t: (b * nt + t, blk))
    cols = lambda blk: (lambda b, t: (0, blk))
    in_specs = [
        pl.BlockSpec((q, d_inner), rows(0)),
        pl.BlockSpec((q, d_inner), rows(x_blk)),
        pl.BlockSpec((q, gn), rows(b_blk)),
        pl.BlockSpec((q, gn), rows(c_blk)),
        pl.BlockSpec((q, LANES), rows(0)),
        pl.BlockSpec((LANES, q), lambda b, t: (0, b * nt + t)),
        pl.BlockSpec((CONV_WIDTH, d_inner), cols(0)),
        pl.BlockSpec((CONV_WIDTH, gn), cols(cwb_blk)),
        pl.BlockSpec((CONV_WIDTH, gn), cols(cwc_blk)),
        pl.BlockSpec((1, d_inner), cols(0)),
        pl.BlockSpec((1, gn), cols(cwb_blk)),
        pl.BlockSpec((1, gn), cols(cwc_blk)),
        pl.BlockSpec((1, LANES), cols(0)),
        pl.BlockSpec((LANES, 1), cols(0)),
        pl.BlockSpec((1, LANES), cols(0)),
        pl.BlockSpec((LANES, 1), cols(0)),
        pl.BlockSpec((1, LANES), cols(0)),
        pl.BlockSpec((1, d_inner), cols(0)),
        pl.BlockSpec(e64.shape, lambda b, t: (0, 0, 0)),
    ]
    scratch_shapes = [
        ((SSM_GROUPS, SSM_STATE, gw), F32),
        ((SSM_GROUPS, SUBLANES, gw), F32),
        ((SSM_GROUPS, SUBLANES, SSM_STATE), F32),
        ((SSM_GROUPS, SUBLANES, SSM_STATE), F32),
        ((SSM_GROUPS, SUBLANES + q, gw), F32),
        ((SSM_GROUPS, SUBLANES + q, SSM_STATE), F32),
        ((SSM_GROUPS, SUBLANES + q, SSM_STATE), F32),
        ((q, 3 * LANES), BF16),
        ((q, LANES), F32),
        ((LANES, q), F32),
    ]
    block_bytes = (3 * _nbytes((q, d_inner), BF16) + 2 * _nbytes((q, gn), BF16) + 2 * _nbytes((q, LANES), F32)
                   + _nbytes(e64.shape, BF16) + (CONV_WIDTH + 2) * _nbytes((SUBLANES, d_inner + 2 * gn), F32))
    return pl.pallas_call(
        _ssd_body,
        grid=(batch, nt),
        in_specs=in_specs,
        out_specs=pl.BlockSpec((q, d_inner), rows(0)),
        out_shape=jax.ShapeDtypeStruct((m, d_inner), BF16),
        scratch_shapes=[pltpu.VMEM(s, d) for s, d in scratch_shapes],
        compiler_params=pltpu.CompilerParams(
            dimension_semantics=("arbitrary", "arbitrary"),
            vmem_limit_bytes=_vmem_limit(block_bytes, sum(_nbytes(s, d) for s, d in scratch_shapes))),
        name="ssd_scan",
    )(zx, zx, zx, zx, dt, dtt, conv_w, conv_w, conv_w, conv_b, conv_b, conv_b,
      pad_row(dt_bias), pad_col(dt_bias), pad_row(a_log), pad_col(a_log), pad_row(d_skip),
      norm_g.reshape(1, d_inner), e64)


def _fold_rows(v, op):
    r, c = v.shape
    v3 = v.reshape(r // SUBLANES, SUBLANES, c)
    return jnp.max(v3, axis=0) if op == "max" else jnp.sum(v3, axis=0)


def _moba_body(q_ref, k_ref, v_ref, o_ref,
               kmean_ref, kaug_ref, vt_ref, qaug_ref, s_ref, m_ref, l_ref, acc_ref):
    qi = pl.program_id(2)
    bs = MOBA_BLOCK
    dh = LANES
    seq = k_ref.shape[0]
    nb = seq // bs
    heads = range(q_ref.shape[1] // dh)
    cbk = ATTN_CHUNK_BLOCKS
    ck = cbk * bs
    c2 = dh ** -0.5 * LOG2_E
    nt = (((1,), (1,)), ((), ()))

    @pl.when(qi == 0)
    def _():
        lane = lax.broadcasted_iota(jnp.int32, (bs, LANES), 1)
        for hh in heads:
            for n in range(nb):
                kb = k_ref[n * bs:(n + 1) * bs, hh * dh:(hh + 1) * dh]
                kmean_ref[hh, n:n + 1, :] = jnp.mean(kb.astype(F32), axis=0, keepdims=True)
                kaug_ref[hh, n * bs:(n + 1) * bs, 0:dh] = kb
                kaug_ref[hh, n * bs:(n + 1) * bs, dh:dh + LANES] = jnp.where(lane == n, 1.0, 0.0).astype(BF16)
                vt_ref[hh, n] = v_ref[n * bs:(n + 1) * bs, hh * dh:(hh + 1) * dh].astype(F32).T.astype(BF16)

    n_chunks = lax.shift_right_logical(qi + (cbk - 1), cbk.bit_length() - 1)

    def step(n):
        own = pl.multiple_of(qi * bs, bs)
        blk = lax.broadcasted_iota(jnp.int32, (nb, bs), 0)
        key = lax.broadcasted_iota(jnp.int32, (bs, bs), 0)
        qry = lax.broadcasted_iota(jnp.int32, (bs, bs), 1)
        for hh in heads:
            q = q_ref[:, hh * dh:(hh + 1) * dh]
            gate = lax.dot_general(kmean_ref[hh], q.astype(F32), nt,
                                   precision=lax.Precision.HIGHEST, preferred_element_type=F32)
            gate = jnp.where(blk < qi, gate, -jnp.inf)
            rank = jnp.zeros((nb, bs), F32)
            for mm in range(nb):
                gm = gate[mm:mm + 1, :]
                beats = jnp.where(gm > gate, 1.0, jnp.where(jnp.logical_and(gm == gate, blk > mm), 1.0, 0.0))
                rank = rank + beats
            chosen = jnp.logical_and(rank < MOBA_TOPK, gate > -jnp.inf)
            bias_t = jnp.where(chosen, 0.0, MASK_BIAS)
            bias_q = jnp.concatenate([bias_t, jnp.zeros((LANES - nb, bs), F32)], axis=0).T
            qaug_ref[hh, :, 0:dh] = q
            qaug_ref[hh, :, dh:dh + LANES] = bias_q.astype(BF16)
            s = lax.dot_general(k_ref[pl.ds(own, bs), hh * dh:(hh + 1) * dh], q, nt,
                                preferred_element_type=F32) * c2
            s = jnp.where(key <= qry, s, MASK_BIAS)
            s_ref[hh, seq:seq + bs, :] = s
            m_ref[hh] = _fold_rows(s, "max")

        for c in range(n):
            for hh in heads:
                s = lax.dot_general(kaug_ref[hh, c * ck:(c + 1) * ck, :], qaug_ref[hh], nt,
                                    preferred_element_type=F32) * c2
                s_ref[hh, c * ck:(c + 1) * ck, :] = s
                m_ref[hh] = jnp.maximum(m_ref[hh], _fold_rows(s, "max"))

        for hh in heads:
            m = jnp.max(m_ref[hh], axis=0, keepdims=True)
            m_ref[hh] = jnp.broadcast_to(m, (SUBLANES, bs))
            p = jnp.exp2(s_ref[hh, seq:seq + bs, :] - m)
            l_ref[hh] = _fold_rows(p, "sum")
            acc_ref[hh] = jnp.dot(vt_ref[hh, qi], p.astype(BF16), preferred_element_type=F32)

        for hh in heads:
            acc = acc_ref[hh]
            lsum = l_ref[hh]
            for c in range(n):
                p = jnp.exp2(s_ref[hh, c * ck:(c + 1) * ck, :] - m_ref[hh, 0:1, :])
                lsum = lsum + _fold_rows(p, "sum")
                pb = p.astype(BF16)
                for j in range(cbk):
                    acc = acc + jnp.dot(vt_ref[hh, c * cbk + j], pb[j * bs:(j + 1) * bs, :],
                                        preferred_element_type=F32)
            l_ref[hh] = lsum
            acc_ref[hh] = acc

        for hh in heads:
            l = jnp.sum(l_ref[hh], axis=0, keepdims=True)
            o_ref[:, hh * dh:(hh + 1) * dh] = (acc_ref[hh] / l).T.astype(o_ref.dtype)

    for n in range(nb // cbk + 1):
        pl.when(n_chunks == n)(functools.partial(step, n))


def _moba_attention(qkv, batch):
    m = qkv.shape[0]
    seq = m // batch
    dh = qkv.shape[1] // (3 * ATTN_HEADS)
    bs = MOBA_BLOCK
    nq = seq // bs
    hb = ATTN_HEADS_PER_STEP
    ng = ATTN_HEADS // hb
    assert dh == LANES and nq % ATTN_CHUNK_BLOCKS == 0 and nq <= LANES
    assert ATTN_CHUNK_BLOCKS & (ATTN_CHUNK_BLOCKS - 1) == 0
    block_bytes = 2 * _nbytes((bs, hb * dh), BF16) + 2 * _nbytes((seq, hb * dh), BF16)
    scratch_shapes = [
        ((hb, nq, dh), F32),
        ((hb, seq, dh + LANES), BF16),
        ((hb, nq, dh, bs), BF16),
        ((hb, bs, dh + LANES), BF16),
        ((hb, seq + bs, bs), F32),
        ((hb, SUBLANES, bs), F32),
        ((hb, SUBLANES, bs), F32),
        ((hb, dh, bs), F32),
    ]
    return pl.pallas_call(
        _moba_body,
        grid=(batch, ng, nq),
        in_specs=[pl.BlockSpec((bs, hb * dh), lambda b, h, i: (b * nq + i, h)),
                  pl.BlockSpec((seq, hb * dh), lambda b, h, i: (b, ng + h)),
                  pl.BlockSpec((seq, hb * dh), lambda b, h, i: (b, 2 * ng + h))],
        out_specs=pl.BlockSpec((bs, hb * dh), lambda b, h, i: (b * nq + i, h)),
        out_shape=jax.ShapeDtypeStruct((m, ATTN_HEADS * dh), BF16),
        scratch_shapes=[pltpu.VMEM(s, d) for s, d in scratch_shapes],
        compiler_params=pltpu.CompilerParams(
            dimension_semantics=("arbitrary", "arbitrary", "arbitrary"),
            vmem_limit_bytes=_vmem_limit(block_bytes, sum(_nbytes(s, d) for s, d in scratch_shapes))),
        name="moba_attention",
    )(qkv, qkv, qkv)


def _mamba2_mixer(h, u, gain, w_in_t, layer, conv_w, conv_b, dt_bias, a_log, d_skip, norm_g, w_out,
                  next_gain, batch):
    d_inner = w_out.shape[1]
    n_zx = 2 * d_inner + 2 * SSM_GROUPS * SSM_STATE
    w_dt_t = w_in_t[layer][n_zx:, :]
    if u is None:
        u, dt, dtt = _dt_proj(h, gain, w_dt_t)
    else:
        dt, dtt = _dt_proj(u, None, w_dt_t)
    zx = _fused_matmul([u], [None], [(w_in_t, layer, "f32_t")], [0], [], [], _ep_plain, (BF16,),
                       tm=2048, tn=1024, n_out=n_zx, name="ssm_in_proj")
    y = _ssd_scan(zx, dt, dtt, conv_w, conv_b.reshape(1, -1), dt_bias, a_log, d_skip, norm_g, batch)
    return _fused_matmul([y], [None], [(w_out[layer].astype(BF16), None, "bf16")], [0], [h], [next_gain],
                         _ep_residual_norm, (F32, BF16), tm=512, tn=h.shape[1], single_buffer_weights=True,
                         name="ssm_out_proj")


def _moba_mixer(h, u, w_qkv, w_o, layer, next_gain, batch):
    qkv = _fused_matmul([u], [None], [(w_qkv, layer, "f32")], [0], [], [], _ep_plain, (BF16,),
                        tm=2048, tn=1024, name="attn_qkv")
    o = _moba_attention(qkv, batch)
    d = h.shape[1]
    return _fused_matmul([o], [None], [(w_o[layer].astype(BF16), None, "bf16")], [0], [h], [next_gain],
                         _ep_residual_norm, (F32, BF16), tm=512, tn=d, single_buffer_weights=True,
                         name="attn_out")


def _ffn(h, u, layer, w_gate, w_up, w_down):
    ws = [(w_gate, layer, "f32"), (w_up, layer, "f32")]
    a = _fused_matmul([u], [None], ws, [0, 0], [], [], _ep_swiglu, (BF16,), tm=1024, tn=512, name="ffn_up")
    return _fused_matmul([a], [None], [(w_down, layer, "f32")], [0], [h], [], _ep_residual, (F32,),
                         tm=512, tn=1024, single_buffer_weights=True, name="ffn_down")


def _ple(h, p, layer, gain, w_pgate_l, w_pproj_l, out_gain, last):
    d = h.shape[1]
    ws = [(w_pgate_l.astype(BF16), None, "bf16"), (w_pproj_l.astype(BF16), None, "bf16")]
    if last:
        return _fused_matmul([h, p], [gain, None], ws, [0, 1], [], [out_gain], _ep_ple_final, (F32,),
                             tm=512, tn=d, single_buffer_weights=True, x_layer=layer, name="ple_final")
    return _fused_matmul([h, p], [gain, None], ws, [0, 1], [], [out_gain], _ep_ple_norm, (F32, BF16),
                         tm=512, tn=d, single_buffer_weights=True, x_layer=layer, name="ple")


def kernel(x, p, mix_norm_g, ffn_norm_g, ple_norm_g, ssm_w_in, ssm_conv_w, ssm_conv_b, ssm_dt_bias, ssm_a_log, ssm_d, ssm_norm_g, ssm_w_out, attn_w_qkv, attn_w_o, ffn_w_gate, ffn_w_up, ffn_w_down, ple_w_proj, ple_w_gate, final_norm_g):
    batch, seq, d = x.shape
    m = batch * seq
    depth = p.shape[0]
    ssm_w_in_t = jnp.swapaxes(ssm_w_in, 1, 2)
    h = x.reshape(m, d)
    u = None
    for i in range(depth):
        j = i // 2
        if i % 2 == 0:
            h, u = _mamba2_mixer(h, u, mix_norm_g[i], ssm_w_in_t, j, ssm_conv_w[j], ssm_conv_b[j], ssm_dt_bias[j],
                                 ssm_a_log[j], ssm_d[j], ssm_norm_g[j], ssm_w_out, ffn_norm_g[i], batch)
        else:
            h, u = _moba_mixer(h, u, attn_w_qkv, attn_w_o, j, ffn_norm_g[i], batch)
        h = _ffn(h, u, i, ffn_w_gate, ffn_w_up, ffn_w_down)
        last = i == depth - 1
        out = _ple(h, p.reshape(depth, m, -1), i, ple_norm_g[i], ple_w_gate[i], ple_w_proj[i],
                   final_norm_g if last else mix_norm_g[i + 1], last)
        if not last:
            h, u = out
    return out.reshape(batch, seq, d)
```
